```python
import math
import jax
import jax.numpy as jnp
from jax import lax
import numpy as np

D_MODEL = 2048
BATCH = 4
SEQ = 4096
DEPTH = 2

F32 = jnp.float32
N_EVEN = (DEPTH + 1) // 2
N_ODD = DEPTH // 2

DN_ALPHA = (2.0 * DEPTH) ** 0.25
DN_BETA = (8.0 * DEPTH) ** -0.25
LN_EPS = 1e-5
RMS_EPS = 1e-6
NEG_INF = -1e30
MIX_WIDTH = D_MODEL

MLA_HEADS = 8
MLA_NOPE = 128
MLA_ROPE = 64
MLA_V = 128
MLA_QK = MLA_NOPE + MLA_ROPE
MLA_Q_RANK = 512
MLA_KV_RANK = 256
ROPE_THETA = 10000.0
Q_BLOCK = 128

S5_WIDTH = MIX_WIDTH - MLA_HEADS * MLA_V
S5_GROUP = 16
S5_GROUPS = S5_WIDTH // S5_GROUP
S5_STATE = 64
S5_DT_MIN = 1e-3
S5_DT_MAX = 1e-1
EVEN_IN = MLA_Q_RANK + MLA_KV_RANK + MLA_ROPE + S5_WIDTH

DIL_PATTERNS = ((128, 1), (512, 4), (2048, 16))
N_DIL = len(DIL_PATTERNS)
DIL_HEADS = 8
DIL_HEAD_DIM = 64
DIL_WIDTH = DIL_HEADS * DIL_HEAD_DIM
DIL_IN = N_DIL * 3 * DIL_WIDTH

T5_BUCKETS = 32
T5_MAX_DIST = 2048
T5_HEADS = N_DIL * DIL_HEADS

RW_WIDTH = MIX_WIDTH - DIL_WIDTH
RW_HEAD = 64
RW_HEADS = RW_WIDTH // RW_HEAD
RW_LORA_W = 64
RW_LORA_A = 64
RW_LORA_G = 224
RW_GN_EPS = 64e-5
RW_IN = 3 * RW_WIDTH + RW_LORA_W + RW_LORA_A + RW_LORA_G
RW_SPLITS = [RW_WIDTH, 2 * RW_WIDTH, 3 * RW_WIDTH, 3 * RW_WIDTH + RW_LORA_W, 3 * RW_WIDTH + RW_LORA_W + RW_LORA_A]
ODD_IN = DIL_IN + RW_IN

N_EXPERTS = 16
N_EXPERT_GROUPS = 4
EXPERTS_PER_GROUP = N_EXPERTS // N_EXPERT_GROUPS
TOP_K = 2
D_EXPERT = 512

kernel_name = "hybrid_mla_s5_dilated_rwkv7_moe"


def _layernorm(x, g, b):
    xf = x.astype(F32)
    mu = xf.mean(-1, keepdims=True)
    var = jnp.square(xf - mu).mean(-1, keepdims=True)
    return (xf - mu) * lax.rsqrt(var + LN_EPS) * g + b


def _rmsnorm(x, g):
    xf = x.astype(F32)
    return xf * lax.rsqrt(jnp.mean(xf * xf, -1, keepdims=True) + RMS_EPS) * g


def _rope_tables(seq):
    inv = ROPE_THETA ** (-jnp.arange(0, MLA_ROPE, 2, dtype=F32) / MLA_ROPE)
    ang = jnp.arange(seq, dtype=F32)[:, None] * inv[None]
    return jnp.cos(ang), jnp.sin(ang)


def _rope(x, cos, sin):
    x1, x2 = jnp.split(x.astype(F32), 2, axis=-1)
    return jnp.concatenate([x1 * cos - x2 * sin, x1 * sin + x2 * cos], axis=-1)


def _mla(q_c, kv_c, k_r, q_norm, w_uq, kv_norm, w_ukv):
    B_, S_, _ = q_c.shape
    q = (_rmsnorm(q_c, q_norm) @ w_uq).astype(F32).reshape(B_, S_, MLA_HEADS, MLA_QK)
    kv = (_rmsnorm(kv_c, kv_norm) @ w_ukv).astype(F32).reshape(B_, S_, MLA_HEADS, MLA_NOPE + MLA_V)
    cos, sin = _rope_tables(S_)
    q_nope = q[..., :MLA_NOPE]
    q_rope = _rope(q[..., MLA_NOPE:], cos[None, :, None], sin[None, :, None])
    k_rope = _rope(k_r, cos[None], sin[None])
    k_nope, v = kv[..., :MLA_NOPE], kv[..., MLA_NOPE:]
    scale = MLA_QK ** -0.5
    outs = []
    for i in range(S_ // Q_BLOCK):
        q0, q1 = i * Q_BLOCK, (i + 1) * Q_BLOCK
        s = (jnp.einsum('bqhc,bkhc->bhqk', q_nope[:, q0:q1], k_nope[:, :q1])
             + jnp.einsum('bqhr,bkr->bhqk', q_rope[:, q0:q1], k_rope[:, :q1])) * scale
        causal = jnp.arange(q1)[None, :] <= jnp.arange(q0, q1)[:, None]
        p = jax.nn.softmax(jnp.where(causal, s, NEG_INF), axis=-1)
        outs.append(jnp.einsum('bhqk,bkhd->bqhd', p, v[:, :q1]))
    return jnp.concatenate(outs, axis=1).reshape(B_, S_, MLA_HEADS * MLA_V)


def _lin_rec(left, right):
    a_l, b_l = left
    a_r, b_r = right
    return a_l * a_r, a_r * b_l + b_r


def _s5(u, lam_re, lam_im, b_re, b_im, c_re, c_im, d_skip, log_dt, w_glu):
    B_, S_, _ = u.shape
    uf = u.astype(F32).reshape(B_, S_, S5_GROUPS, S5_GROUP)
    lam = lax.complex(lam_re.astype(F32), lam_im.astype(F32))
    dt = jnp.exp(log_dt.astype(F32))[:, None]
    lam_bar = jnp.exp(lam * dt)
    b_bar = ((lam_bar - 1.0) / lam)[..., None] * lax.complex(b_re.astype(F32), b_im.astype(F32))
    bu = jnp.einsum('bsgc,gpc->bsgp', uf.astype(jnp.complex64), b_bar)
    a = jnp.broadcast_to(lam_bar[None, None], (1, S_, S5_GROUPS, S5_STATE))
    _, states = lax.associative_scan(_lin_rec, (a, bu), axis=1)
    c = lax.complex(c_re.astype(F32), c_im.astype(F32))
    y = jnp.einsum('bsgp,gcp->bsgc', states, c).real + d_skip.astype(F32) * uf
    y = jax.nn.gelu(y.reshape(B_, S_, S5_WIDTH))
    return y * jax.nn.sigmoid(y @ w_glu)


def _t5_bucket(dist):
    exact = T5_BUCKETS // 2
    logd = jnp.log(jnp.maximum(dist, 1).astype(F32) / exact) / math.log(T5_MAX_DIST / exact)
    large = jnp.minimum(exact + (logd * (T5_BUCKETS - exact)).astype(jnp.int32), T5_BUCKETS - 1)
    return jnp.where(dist < exact, dist, large)


def _dilated_branch(q, k, v, bias_tab, dil, span):
    B_, S_, H, Dh = q.shape
    L = S_ // dil
    nb = -(-L // span)
    Lp = nb * span

    def streams(t):
        return t.reshape(B_, L, dil, H, Dh).transpose(0, 2, 1, 3, 4)

    qb = jnp.pad(streams(q), ((0, 0), (0, 0), (0, Lp - L), (0, 0), (0, 0))).reshape(B_, dil, nb, span, H, Dh)

    def band_blocks(t):
        t = jnp.pad(streams(t), ((0, 0), (0, 0), (span, Lp - L), (0, 0), (0, 0))).reshape(B_, dil, nb + 1, span, H, Dh)
        return jnp.concatenate([t[:, :, :-1], t[:, :, 1:]], axis=3)

    kb, vb = band_blocks(k), band_blocks(v)
    qi = jnp.arange(span)[:, None]
    ku = jnp.arange(2 * span)[None, :]
    delta = span + qi - ku
    key_pos = (jnp.arange(nb)[:, None, None] - 1) * span + ku[None]
    valid = ((delta >= 0) & (delta <= span))[None] & (key_pos >= 0)
    bias = bias_tab.astype(F32)[_t5_bucket(jnp.clip(delta, 0, span) * dil)]
    s = jnp.einsum('bgnihc,bgnuhc->bgnihu', qb, kb) * (Dh ** -0.5) + bias.transpose(0, 2, 1)[None, None, None]
    s = jnp.where(valid[None, None, :, :, None, :], s, NEG_INF)
    m = s.max(-1)
    p = jnp.exp(s - m[..., None])
    l = p.sum(-1)
    acc = jnp.einsum('bgnihu,bgnuhc->bgnihc', p, vb)

    def unstream(t):
        t = t.reshape(B_, dil, Lp, *t.shape[4:])[:, :, :L]
        return jnp.swapaxes(t, 1, 2).reshape(B_, S_, *t.shape[3:])

    return unstream(acc), unstream(m), unstream(l)


def _dilated_mixture(zc, rel_bias):
    B_, S_, _ = zc.shape
    z = zc.astype(F32).reshape(B_, S_, N_DIL, 3, DIL_HEADS, DIL_HEAD_DIM)
    accs, ms, ls = [], [], []
    for gi, (win, dil) in enumerate(DIL_PATTERNS):
        acc, m, l = _dilated_branch(z[:, :, gi, 0], z[:, :, gi, 1], z[:, :, gi, 2],
                                    rel_bias[:, gi * DIL_HEADS:(gi + 1) * DIL_HEADS], dil, win // dil)
        accs.append(acc)
        ms.append(m)
        ls.append(l)
    m_all = jnp.stack(ms)
    wts = jnp.exp(m_all - m_all.max(0))
    num = sum(wts[i][..., None] * accs[i] for i in range(N_DIL))
    den = (wts * jnp.stack(ls)).sum(0)
    return (num / den[..., None]).reshape(B_, S_, DIL_WIDTH)


def _rwkv7(zr, mu, w0, w2, a0, a2, g2, k_k, k_a, r_k, lnx_g, lnx_b):
    B_, S_, _ = zr.shape
    zf = zr.astype(F32)
    prev = jnp.pad(zf[:, :-1], ((0, 0), (1, 0), (0, 0)))
    xs = zf + (prev - zf) * mu
    r, k, v, wd, ad, gd = jnp.split(xs, RW_SPLITS, axis=-1)
    w_log = -jax.nn.softplus(-(w0 + jnp.tanh(wd) @ w2)) - 0.5
    decay = jnp.exp(-jnp.exp(w_log))
    a = jax.nn.sigmoid(a0 + ad @ a2)
    g = jax.nn.sigmoid(gd) @ g2

    def heads(t):
        return t.reshape(B_, S_, RW_HEADS, RW_HEAD)

    kk = heads(k * k_k)
    kk = kk / jnp.maximum(jnp.sqrt(jnp.sum(kk * kk, -1, keepdims=True)), 1e-12)
    k = k * (1.0 + (a - 1.0) * k_a)
    r_h, k_h, v_h, w_h, a_h = heads(r), heads(k), heads(v), heads(decay), heads(a)
    b_h = kk * a_h

    def step(state, inp):
        rt, wt, kt, vt, at, bt = inp
        sa = jnp.einsum('bhij,bhj->bhi', state, at)
        state = state * wt[:, :, None, :] + sa[..., None] * bt[:, :, None, :] + vt[..., None] * kt[:, :, None, :]
        return state, jnp.einsum('bhij,bhj->bhi', state, rt)

    init = jnp.zeros((B_, RW_HEADS, RW_HEAD, RW_HEAD), F32)
    seq = (r_h, w_h, k_h, v_h, -kk, b_h)
    _, y = lax.scan(step, init, tuple(jnp.moveaxis(t, 1, 0) for t in seq))
    y = jnp.moveaxis(y, 0, 1)
    mean = y.mean(-1, keepdims=True)
    var = jnp.square(y - mean).mean(-1, keepdims=True)
    y = ((y - mean) * lax.rsqrt(var + RW_GN_EPS)).reshape(B_, S_, RW_WIDTH) * lnx_g + lnx_b
    bonus = jnp.sum(r_h * k_h * r_k, -1, keepdims=True) * v_h
    y = y + bonus.reshape(B_, S_, RW_WIDTH)
    return y * g


def _moe(h, router_w, router_b, w_gate_up, w_down):
    B_, S_, D = h.shape
    t = h.reshape(B_ * S_, D)
    s = jax.nn.sigmoid((t @ router_w).astype(F32))
    sb = s + router_b.astype(F32)
    sbg = sb.reshape(-1, N_EXPERT_GROUPS, EXPERTS_PER_GROUP)
    gscore = lax.top_k(sbg, TOP_K)[0].sum(-1)
    gmask = jax.nn.one_hot(jnp.argmax(gscore, -1), N_EXPERT_GROUPS, dtype=F32)
    masked = jnp.where(gmask[..., None] > 0, sbg, NEG_INF).reshape(-1, N_EXPERTS)
    _, eidx = lax.top_k(masked, TOP_K)
    s_sel = jnp.take_along_axis(s, eidx, axis=1)
    gw = s_sel / s_sel.sum(-1, keepdims=True)
    gates = (jax.nn.one_hot(eidx, N_EXPERTS, dtype=F32) * gw[..., None]).sum(1)
    y = jnp.zeros((B_ * S_, D), F32)
    for e in range(N_EXPERTS):
        gt, up = jnp.split(t @ w_gate_up[e], 2, axis=-1)
        y = y + gates[:, e:e + 1] * ((jax.nn.silu(gt) * up) @ w_down[e])
    return y.reshape(B_, S_, D)


def setup_inputs(seed: int = 0) -> dict:
    key = jax.random.key(seed)
    ks = iter(jax.random.split(key, 64))
    D = D_MODEL

    def nrm(shape, scale):
        return jax.random.normal(next(ks), shape, F32) * scale

    def gain(shape):
        return 1.0 + nrm(shape, 0.01)

    def unif(shape, lo, hi):
        return jax.random.uniform(next(ks), shape, F32, minval=lo, maxval=hi)

    return {
        "x": nrm((BATCH, SEQ, D), 1.0),
        "c": nrm((BATCH, D), 1.0),
        "ada_w": nrm((DEPTH, D, 6 * D), 0.1 * D ** -0.5),
        "ada_b": nrm((DEPTH, 6 * D), 0.01),
        "ln_mix_g": gain((DEPTH, D)),
        "ln_mix_b": nrm((DEPTH, D), 0.01),
        "ln_ffn_g": gain((DEPTH, D)),
        "ln_ffn_b": nrm((DEPTH, D), 0.01),
        "router_w": nrm((D, N_EXPERTS), D ** -0.5),
        "router_b": nrm((N_EXPERTS,), 0.01),
        "moe_w_gate_up": nrm((DEPTH, N_EXPERTS, D, 2 * D_EXPERT), D ** -0.5),
        "moe_w_down": nrm((DEPTH, N_EXPERTS, D_EXPERT, D), DN_BETA * D_EXPERT ** -0.5),
        "rel_bias": nrm((T5_BUCKETS, T5_HEADS), 0.5),
        "ev_w_in": nrm((N_EVEN, D, EVEN_IN), D ** -0.5),
        "mla_q_norm": gain((N_EVEN, MLA_Q_RANK)),
        "mla_w_uq": nrm((N_EVEN, MLA_Q_RANK, MLA_HEADS * MLA_QK), MLA_Q_RANK ** -0.5),
        "mla_kv_norm": gain((N_EVEN, MLA_KV_RANK)),
        "mla_w_ukv": nrm((N_EVEN, MLA_KV_RANK, MLA_HEADS * (MLA_NOPE + MLA_V)), MLA_KV_RANK ** -0.5),
        "s5_lambda_re": -0.5 + nrm((N_EVEN, S5_GROUPS, S5_STATE), 0.01),
        "s5_lambda_im": jnp.tile(jnp.pi * jnp.arange(S5_STATE, dtype=F32), (N_EVEN, S5_GROUPS, 1)),
        "s5_b_re": nrm((N_EVEN, S5_GROUPS, S5_STATE, S5_GROUP), (2 * S5_GROUP) ** -0.5),
        "s5_b_im": nrm((N_EVEN, S5_GROUPS, S5_STATE, S5_GROUP), (2 * S5_GROUP) ** -0.5),
        "s5_c_re": nrm((N_EVEN, S5_GROUPS, S5_GROUP, S5_STATE), S5_STATE ** -0.5),
        "s5_c_im": nrm((N_EVEN, S5_GROUPS, S5_GROUP, S5_STATE), S5_STATE ** -0.5),
        "s5_d": nrm((N_EVEN, S5_GROUPS, S5_GROUP), 1.0),
        "s5_log_dt": unif((N_EVEN, S5_GROUPS), math.log(S5_DT_MIN), math.log(S5_DT_MAX)),
        "s5_w_glu": nrm((N_EVEN, S5_WIDTH, S5_WIDTH), S5_WIDTH ** -0.5),
        "ev_w_out": nrm((N_EVEN, MIX_WIDTH, D), DN_BETA * MIX_WIDTH ** -0.5),
        "od_w_in": nrm((N_ODD, D, ODD_IN), D ** -0.5),
        "rw_mu": unif((N_ODD, RW_IN), 0.0, 1.0),
        "rw_w0": unif((N_ODD, RW_WIDTH), -5.0, 1.0),
        "rw_w2": nrm((N_ODD, RW_LORA_W, RW_WIDTH), 0.1 * RW_LORA_W ** -0.5),
        "rw_a0": nrm((N_ODD, RW_WIDTH), 0.1),
        "rw_a2": nrm((N_ODD, RW_LORA_A, RW_WIDTH), 0.1 * RW_LORA_A ** -0.5),
        "rw_g2": nrm((N_ODD, RW_LORA_G, RW_WIDTH), RW_LORA_G ** -0.5),
        "rw_k_k": 0.85 + nrm((N_ODD, RW_WIDTH), 0.05),
        "rw_k_a": gain((N_ODD, RW_WIDTH)),
        "rw_r_k": nrm((N_ODD, RW_HEADS, RW_HEAD), 0.1),
        "rw_lnx_g": gain((N_ODD, RW_WIDTH)),
        "rw_lnx_b": nrm((N_ODD, RW_WIDTH), 0.01),
        "od_w_out": nrm((N_ODD, MIX_WIDTH, D), DN_BETA * MIX_WIDTH ** -0.5),
    }


def reference(x, c, ada_w, ada_b, ln_mix_g, ln_mix_b, ln_ffn_g, ln_ffn_b, router_w, router_b,
              moe_w_gate_up, moe_w_down, rel_bias, ev_w_in, mla_q_norm, mla_w_uq, mla_kv_norm, mla_w_ukv,
              s5_lambda_re, s5_lambda_im, s5_b_re, s5_b_im, s5_c_re, s5_c_im, s5_d, s5_log_dt, s5_w_glu,
              ev_w_out, od_w_in, rw_mu, rw_w0, rw_w2, rw_a0, rw_a2, rw_g2, rw_k_k, rw_k_a, rw_r_k,
              rw_lnx_g, rw_lnx_b, od_w_out):
    dt = x.dtype
    cond = jax.nn.silu(c.astype(F32))
    q_split = [MLA_Q_RANK, MLA_Q_RANK + MLA_KV_RANK, MLA_Q_RANK + MLA_KV_RANK + MLA_ROPE]
    for layer in range(DEPTH):
        mod = (cond @ ada_w[layer] + ada_b[layer])[:, None, :]
        sh_m, sc_m, g_m, sh_f, sc_f, g_f = jnp.split(mod, 6, axis=-1)
        h = (x * (1.0 + sc_m) + sh_m).astype(dt)
        if layer % 2 == 0:
            e = layer // 2
            q_c, kv_c, k_r, u = jnp.split(h @ ev_w_in[e], q_split, axis=-1)
            att = _mla(q_c, kv_c, k_r, mla_q_norm[e], mla_w_uq[e], mla_kv_norm[e], mla_w_ukv[e])
            ssm = _s5(u, s5_lambda_re[e], s5_lambda_im[e], s5_b_re[e], s5_b_im[e], s5_c_re[e], s5_c_im[e],
                      s5_d[e], s5_log_dt[e], s5_w_glu[e])
            y = jnp.concatenate([att, ssm], axis=-1).astype(dt) @ ev_w_out[e]
        else:
            o = layer // 2
            z = h @ od_w_in[o]
            att = _dilated_mixture(z[..., :DIL_IN], rel_bias)
            tm = _rwkv7(z[..., DIL_IN:], rw_mu[o], rw_w0[o], rw_w2[o], rw_a0[o], rw_a2[o], rw_g2[o],
                        rw_k_k[o], rw_k_a[o], rw_r_k[o], rw_lnx_g[o], rw_lnx_b[o])
            y = jnp.concatenate([att, tm], axis=-1).astype(dt) @ od_w_out[o]
        x = _layernorm(DN_ALPHA * x + (1.0 + g_m) * y, ln_mix_g[layer], ln_mix_b[layer]).astype(dt)
        h = (x * (1.0 + sc_f) + sh_f).astype(dt)
        y = _moe(h, router_w, router_b, moe_w_gate_up[layer], moe_w_down[layer])
        x = _layernorm(DN_ALPHA * x + (1.0 + g_f) * y, ln_ffn_g[layer], ln_ffn_b[layer]).astype(dt)
    return x
```

```python
import functools
import math

import jax
import jax.numpy as jnp
from jax import lax
from jax.experimental import pallas as pl
from jax.experimental.pallas import tpu as pltpu

F32 = jnp.float32
BF16 = jnp.bfloat16

DEPTH = 2
DN_ALPHA = (2.0 * DEPTH) ** 0.25
LN_EPS = 1e-5
RMS_EPS = 1e-6
NEG_INF = -1e30

MLA_HEADS = 8
MLA_NOPE = 128
MLA_ROPE = 64
MLA_V = 128
MLA_QK = MLA_NOPE + MLA_ROPE
MLA_Q_RANK = 512
MLA_KV_RANK = 256
ROPE_THETA = 10000.0
MLA_HEAD_PAD = 256

S5_GROUP = 16
S5_STATE = 64
S5_CHUNK = 16

DIL_PATTERNS = ((128, 1), (512, 4), (2048, 16))
DIL_SPAN = 128
DIL_HEADS = 8
DIL_HEAD_DIM = 64
DIL_WIDTH = DIL_HEADS * DIL_HEAD_DIM
DIL_IN = len(DIL_PATTERNS) * 3 * DIL_WIDTH
T5_BUCKETS = 32
T5_MAX_DIST = 2048

RW_HEAD = 64
RW_LORA_W = 64
RW_LORA_A = 64
RW_LORA_G = 224
RW_GN_EPS = 64e-5
RW_CHUNK = 64
RW_MISC = 512

N_EXPERTS = 16
N_EXPERT_GROUPS = 4
EXPERTS_PER_GROUP = 4
D_EXPERT = 512
N_PAIRS = 6
N_CLASSES = N_EXPERT_GROUPS * N_PAIRS
MOE_TILE = 256

LANES = 128
VMEM_LIMIT = 56 * 1024 * 1024


def _cp(*sem):
    return pltpu.CompilerParams(dimension_semantics=sem, vmem_limit_bytes=VMEM_LIMIT)


def _dot(a, b):
    return jnp.dot(a, b, preferred_element_type=F32)


def _dot_nt(a, b):
    return lax.dot_general(a, b, (((1,), (1,)), ((), ())), preferred_element_type=F32)


def _split_bf16(x, parts):
    out = []
    for _ in range(parts):
        hi = x.astype(BF16)
        out.append(hi)
        x = x - hi.astype(F32)
    return out


def _dot_exact_rhs(x, w_bf16, parts=3):
    acc = None
    for p in _split_bf16(x, parts):
        t = _dot(p, w_bf16)
        acc = t if acc is None else acc + t
    return acc


def _dot_exact_lhs(w_bf16, x, parts=3):
    acc = None
    for p in _split_bf16(x, parts):
        t = _dot(w_bf16, p)
        acc = t if acc is None else acc + t
    return acc


def _layernorm(x, g, b):
    mu = jnp.mean(x, -1, keepdims=True)
    d = x - mu
    var = jnp.mean(d * d, -1, keepdims=True)
    return d * lax.rsqrt(var + LN_EPS) * g + b


def _ada_body(c_ref, w_ref, b_ref, o_ref):
    c = c_ref[...]
    cond = c * jax.nn.sigmoid(c)
    o_ref[0] = _dot_exact_rhs(cond, w_ref[0].astype(BF16), 2) + b_ref[0]


def _ada(c, ada_w, ada_b):
    depth, d, n = ada_w.shape
    b = c.shape[0]
    rows = -(-b // 8) * 8
    cp = jnp.pad(c, ((0, rows - b), (0, 0)))
    tn = 1024
    out = pl.pallas_call(
        _ada_body,
        grid=(depth, n // tn),
        in_specs=[
            pl.BlockSpec((rows, d), lambda l, j: (0, 0)),
            pl.BlockSpec((1, d, tn), lambda l, j: (l, 0, j)),
            pl.BlockSpec((1, 1, tn), lambda l, j: (l, 0, j)),
        ],
        out_specs=pl.BlockSpec((1, rows, tn), lambda l, j: (l, 0, j)),
        out_shape=jax.ShapeDtypeStruct((depth, rows, n), F32),
        compiler_params=_cp("parallel", "parallel"),
        name="ada",
    )(cp, ada_w, ada_b.reshape(depth, 1, n))
    return out[:, :b].reshape(depth, b, 6, d)


def _front0_body(x_ref, mod_ref, cs_ref, win_ref, qn_ref, kvn_ref, wq1_ref, wq2_ref, wkn_ref,
                 wv_ref, q_ref, k_ref, v_ref, u_ref):
    m = mod_ref[0]
    h = (x_ref[...] * (1.0 + m[1:2]) + m[0:1]).astype(BF16)
    z = _dot(h, win_ref[...])
    cs = cs_ref[...]
    c2 = cs[:, :LANES]
    s2 = cs[:, LANES:]
    q_c = z[:, :MLA_Q_RANK]
    kv_c = z[:, MLA_Q_RANK:MLA_Q_RANK + MLA_KV_RANK]
    o = MLA_Q_RANK + MLA_KV_RANK
    krope = (z[:, o:o + LANES] * c2 + z[:, o + LANES:o + 2 * LANES] * s2).astype(BF16)
    u_ref[...] = z[:, o + 2 * LANES:]
    hq = (q_c * lax.rsqrt(jnp.mean(q_c * q_c, -1, keepdims=True) + RMS_EPS) * qn_ref[...]).astype(BF16)
    hkv = (kv_c * lax.rsqrt(jnp.mean(kv_c * kv_c, -1, keepdims=True) + RMS_EPS) * kvn_ref[...]).astype(BF16)
    full = _dot(hq, wq1_ref[...])
    sw = _dot(hq, wq2_ref[...])
    kn = _dot(hkv, wkn_ref[...])
    v_ref[...] = _dot(hkv, wv_ref[...]).astype(BF16)
    for j in range(MLA_HEADS):
        a = j * MLA_HEAD_PAD
        q_ref[:, a:a + LANES] = full[:, a:a + LANES].astype(BF16)
        q_ref[:, a + LANES:a + 2 * LANES] = (
            full[:, a + LANES:a + 2 * LANES] * c2 + sw[:, j * LANES:(j + 1) * LANES] * s2).astype(BF16)
        k_ref[:, a:a + LANES] = kn[:, j * LANES:(j + 1) * LANES].astype(BF16)
        k_ref[:, a + LANES:a + 2 * LANES] = krope


def _rope_swapped(w):
    half = w.shape[-1] // 2
    return jnp.concatenate([-w[..., half:], w[..., :half]], -1)


def _front0(x2, mod, seq, ev_w_in, q_norm, w_uq, kv_norm, w_ukv):
    t, d = x2.shape
    tm = 256
    per_b = seq // tm
    zpad = jnp.zeros((d, LANES - MLA_ROPE), F32)
    o = MLA_Q_RANK + MLA_KV_RANK
    w_kr = ev_w_in[:, o:o + MLA_ROPE]
    win = jnp.concatenate([ev_w_in[:, :o], w_kr, zpad, _rope_swapped(w_kr), zpad,
                           ev_w_in[:, o + MLA_ROPE:]], 1).astype(BF16)
    scale = MLA_QK ** -0.5
    wq = w_uq.reshape(MLA_Q_RANK, MLA_HEADS, MLA_QK) * scale
    zq = jnp.zeros((MLA_Q_RANK, MLA_HEADS, MLA_HEAD_PAD - MLA_QK), F32)
    wq1 = jnp.concatenate([wq, zq], -1).reshape(MLA_Q_RANK, -1).astype(BF16)
    wq2 = jnp.concatenate([_rope_swapped(wq[..., MLA_NOPE:]), zq], -1).reshape(MLA_Q_RANK, -1).astype(BF16)
    wkv = w_ukv.reshape(MLA_KV_RANK, MLA_HEADS, MLA_NOPE + MLA_V)
    wkn = wkv[..., :MLA_NOPE].reshape(MLA_KV_RANK, -1).astype(BF16)
    wv = wkv[..., MLA_NOPE:].reshape(MLA_KV_RANK, -1).astype(BF16)
    inv = ROPE_THETA ** (-jnp.arange(0, MLA_ROPE, 2, dtype=F32) / MLA_ROPE)
    ang = jnp.arange(seq, dtype=F32)[:, None] * inv[None]
    zt = jnp.zeros((seq, LANES - MLA_ROPE), F32)
    cs = jnp.concatenate([jnp.cos(ang), jnp.cos(ang), zt, jnp.sin(ang), jnp.sin(ang), zt], 1)
    const = lambda i: (0, 0)
    hq_w = MLA_HEADS * MLA_HEAD_PAD
    hv_w = MLA_HEADS * MLA_V
    s5w = ev_w_in.shape[1] - o - MLA_ROPE
    return pl.pallas_call(
        _front0_body,
        grid=(t // tm,),
        in_specs=[
            pl.BlockSpec((tm, d), lambda i: (i, 0)),
            pl.BlockSpec((1, 6, d), lambda i: (i // per_b, 0, 0)),
            pl.BlockSpec((tm, 2 * LANES), lambda i: (i % per_b, 0)),
            pl.BlockSpec(win.shape, const),
            pl.BlockSpec((1, MLA_Q_RANK), const),
            pl.BlockSpec((1, MLA_KV_RANK), const),
            pl.BlockSpec(wq1.shape, const),
            pl.BlockSpec(wq2.shape, const),
            pl.BlockSpec(wkn.shape, const),
            pl.BlockSpec(wv.shape, const),
        ],
        out_specs=[
            pl.BlockSpec((tm, hq_w), lambda i: (i, 0)),
            pl.BlockSpec((tm, hq_w), lambda i: (i, 0)),
            pl.BlockSpec((tm, hv_w), lambda i: (i, 0)),
            pl.BlockSpec((tm, s5w), lambda i: (i, 0)),
        ],
        out_shape=[
            jax.ShapeDtypeStruct((t, hq_w), BF16),
            jax.ShapeDtypeStruct((t, hq_w), BF16),
            jax.ShapeDtypeStruct((t, hv_w), BF16),
            jax.ShapeDtypeStruct((t, s5w), F32),
        ],
        compiler_params=_cp("parallel"),
        name="front0",
    )(x2, mod, cs, win, q_norm.reshape(1, -1), kv_norm.reshape(1, -1), wq1, wq2, wkn, wv)


def _flash_body(q_ref, k_ref, v_ref, o_ref, *, tile):
    i = pl.program_id(2)
    q = q_ref[...]

    def step(j, carry, diagonal):
        m, l, acc = carry
        start = pl.multiple_of(j * tile, tile)
        k = k_ref[pl.ds(start, tile), :]
        v = v_ref[pl.ds(start, tile), :]
        s = _dot_nt(q, k)
        if diagonal:
            row = lax.broadcasted_iota(jnp.int32, s.shape, 0)
            col = lax.broadcasted_iota(jnp.int32, s.shape, 1)
            s = jnp.where(col <= row, s, NEG_INF)
        m_new = jnp.maximum(m, jnp.max(s, -1, keepdims=True))
        alpha = jnp.exp(m - m_new)
        p = jnp.exp(s - m_new)
        l = l * alpha + jnp.sum(p, -1, keepdims=True)
        acc = acc * alpha + _dot(p.astype(BF16), v)
        return m_new, l, acc

    init = (jnp.full((tile, 1), NEG_INF, F32), jnp.zeros((tile, 1), F32),
            jnp.zeros((tile, MLA_V), F32))
    carry = lax.fori_loop(0, i, lambda j, c: step(j, c, False), init)
    _, l, acc = step(i, carry, True)
    o_ref[...] = (acc / l).astype(BF16)


def _flash(q, k, v, batch, seq):
    t = q.shape[0]
    tile = 512
    nq = seq // tile
    return pl.pallas_call(
        functools.partial(_flash_body, tile=tile),
        grid=(batch, MLA_HEADS, nq),
        in_specs=[
            pl.BlockSpec((tile, MLA_HEAD_PAD), lambda b, h, i: (b * nq + i, h)),
            pl.BlockSpec((seq, MLA_HEAD_PAD), lambda b, h, i: (b, h)),
            pl.BlockSpec((seq, MLA_V), lambda b, h, i: (b, h)),
        ],
        out_specs=pl.BlockSpec((tile, MLA_V), lambda b, h, i: (b * nq + i, h)),
        out_shape=jax.ShapeDtypeStruct((t, MLA_HEADS * MLA_V), BF16),
        compiler_params=_cp("parallel", "parallel", "arbitrary"),
        name="mla_attention",
    )(q, k, v)


def _s5_body(u_ref, t_ref, w_ref, v_ref, ac_ref, bc_ref, y_ref, *, n_chunks, n_steps):
    u = u_ref[0]
    z = _dot(u, w_ref[0])
    cidx = lax.rem(lax.broadcasted_iota(jnp.int32, z.shape, 0), n_chunks)
    half = S5_STATE
    for step in range(n_steps):
        d = 1 << step
        zs = jnp.where(cidx >= d, pltpu.roll(z, d, 0), 0.0)
        z = z + zs * ac_ref[0, step:step + 1, :] + pltpu.roll(zs, half, 1) * bc_ref[0, step:step + 1, :]
    xprev = jnp.where(cidx >= 1, pltpu.roll(z, 1, 0), 0.0)
    y_ref[0] = _dot(u, t_ref[0]) + _dot(xprev.astype(BF16), v_ref[0])


def _s5_matrices(lam_re, lam_im, b_re, b_im, c_re, c_im, log_dt, n_steps):
    lc = S5_CHUNK
    lam = lax.complex(lam_re.astype(F32), lam_im.astype(F32))
    dt = jnp.exp(log_dt.astype(F32))[:, None]
    lam_bar = jnp.exp(lam * dt)
    b_bar = ((lam_bar - 1.0) / lam)[..., None] * lax.complex(b_re.astype(F32), b_im.astype(F32))
    c = lax.complex(c_re.astype(F32), c_im.astype(F32))
    j = jnp.arange(lc + 1, dtype=F32)[:, None, None]
    pows = jnp.exp(j * (lam * dt)[None])
    kern = jnp.einsum("gop,jgp,gpi->jgoi", c, pows[:lc], b_bar).real
    tt = jnp.arange(lc)
    lag = tt[None, :] - tt[:, None]
    tm = jnp.where((lag >= 0)[:, :, None, None, None], kern[jnp.clip(lag, 0, lc - 1)], 0.0)
    tm = tm.transpose(2, 0, 4, 1, 3).reshape(-1, lc * S5_GROUP, lc * S5_GROUP)
    wm = pows[lc - 1 - tt][..., None] * b_bar[None]
    wm = wm.transpose(1, 0, 3, 2).reshape(-1, lc * S5_GROUP, S5_STATE)
    wm = jnp.concatenate([wm.real, wm.imag], -1)
    cv = c[None] * pows[1:lc + 1][:, :, None, :]
    cv = cv.transpose(1, 3, 0, 2).reshape(-1, S5_STATE, lc * S5_GROUP)
    vm = jnp.concatenate([cv.real, -cv.imag], 1)
    stride = (lc * (2.0 ** jnp.arange(n_steps, dtype=F32)))[None, :, None]
    ap = jnp.exp(stride * (lam * dt)[:, None, :])
    ac = jnp.concatenate([ap.real, ap.real], -1)
    bc = jnp.concatenate([-ap.imag, ap.imag], -1)
    return tm.astype(BF16), wm.astype(BF16), vm.astype(BF16), ac, bc


def _s5_scan(u, batch, seq, mats):
    tm, wm, vm, ac, bc = mats
    g = tm.shape[0]
    lc = S5_CHUNK
    nc = seq // lc
    n_steps = ac.shape[1]
    rows = batch * nc
    cols = lc * S5_GROUP
    ut = u.reshape(batch, nc, lc, g, S5_GROUP).transpose(3, 0, 1, 2, 4).reshape(g, rows, cols).astype(BF16)
    per_g = lambda i: (i, 0, 0)
    y = pl.pallas_call(
        functools.partial(_s5_body, n_chunks=nc, n_steps=n_steps),
        grid=(g,),
        in_specs=[
            pl.BlockSpec((1, rows, cols), per_g),
            pl.BlockSpec((1, cols, cols), per_g),
            pl.BlockSpec((1, cols, 2 * S5_STATE), per_g),
            pl.BlockSpec((1, 2 * S5_STATE, cols), per_g),
            pl.BlockSpec((1, n_steps, 2 * S5_STATE), per_g),
            pl.BlockSpec((1, n_steps, 2 * S5_STATE), per_g),
        ],
        out_specs=pl.BlockSpec((1, rows, cols), per_g),
        out_shape=jax.ShapeDtypeStruct((g, rows, cols), F32),
        compiler_params=_cp("parallel"),
        name="s5_scan",
    )(ut, tm, wm, vm, ac, bc)
    return y.reshape(g, batch, nc, lc, S5_GROUP).transpose(1, 2, 3, 0, 4).reshape(batch * seq, g * S5_GROUP)


def _s5_glu_body(y_ref, u_ref, d_ref, w_ref, o_ref):
    y = y_ref[...] + d_ref[...] * u_ref[...]
    y = 0.5 * y * (1.0 + jnp.tanh(math.sqrt(2.0 / math.pi) * (y + 0.044715 * (y * y * y))))
    gate = jax.nn.sigmoid(_dot(y.astype(BF16), w_ref[...]))
    o_ref[...] = (y * gate).astype(BF16)


def _s5_glu(y, u, d_skip, w_glu):
    t, n = y.shape
    tm = 512
    row = lambda i: (i, 0)
    const = lambda i: (0, 0)
    return pl.pallas_call(
        _s5_glu_body,
        grid=(t // tm,),
        in_specs=[pl.BlockSpec((tm, n), row), pl.BlockSpec((tm, n), row),
                  pl.BlockSpec((1, n), const), pl.BlockSpec((n, n), const)],
        out_specs=pl.BlockSpec((tm, n), row),
        out_shape=jax.ShapeDtypeStruct((t, n), BF16),
        compiler_params=_cp("parallel"),
        name="s5_glu",
    )(y, u, d_skip.reshape(1, n), w_glu.astype(BF16))


def _outproj_body(a1_ref, a2_ref, w1_ref, w2_ref, x_ref, mod_ref, g_ref, b_ref, rwh_ref, rwl_ref,
                  xo_ref, h_ref, lg_ref):
    y = _dot(a1_ref[...], w1_ref[...]) + _dot(a2_ref[...], w2_ref[...])
    m = mod_ref[0]
    xn = _layernorm(DN_ALPHA * x_ref[...] + (1.0 + m[2:3]) * y, g_ref[...], b_ref[...])
    xo_ref[...] = xn
    h = xn * (1.0 + m[4:5]) + m[3:4]
    h_ref[...] = h.astype(BF16)
    h_hi, h_lo = _split_bf16(h, 2)
    lg_ref[...] = _dot(h_hi, rwh_ref[...]) + (_dot(h_lo, rwh_ref[...]) + _dot(h_hi, rwl_ref[...]))


def _outproj(a1, a2, w_out, x2, mod, ln_g, ln_b, router_w, seq):
    t, d = x2.shape
    k1 = a1.shape[1]
    k2 = a2.shape[1]
    tm = 256
    per_b = seq // tm
    w1 = w_out[:k1].astype(BF16)
    w2 = w_out[k1:].astype(BF16)
    rwh, rwl = _split_bf16(router_w, 2)
    ne = router_w.shape[1]
    row = lambda i: (i, 0)
    const = lambda i: (0, 0)
    return pl.pallas_call(
        _outproj_body,
        grid=(t // tm,),
        in_specs=[
            pl.BlockSpec((tm, k1), row), pl.BlockSpec((tm, k2), row),
            pl.BlockSpec((k1, d), const), pl.BlockSpec((k2, d), const),
            pl.BlockSpec((tm, d), row),
            pl.BlockSpec((1, 6, d), lambda i: (i // per_b, 0, 0)),
            pl.BlockSpec((1, d), const), pl.BlockSpec((1, d), const),
            pl.BlockSpec((d, ne), const), pl.BlockSpec((d, ne), const),
        ],
        out_specs=[pl.BlockSpec((tm, d), row), pl.BlockSpec((tm, d), row), pl.BlockSpec((tm, ne), row)],
        out_shape=[jax.ShapeDtypeStruct((t, d), F32), jax.ShapeDtypeStruct((t, d), BF16),
                   jax.ShapeDtypeStruct((t, ne), F32)],
        compiler_params=_cp("parallel"),
        name="outproj_ln",
    )(a1, a2, w1, w2, x2, mod, ln_g.reshape(1, d), ln_b.reshape(1, d), rwh, rwl)


def _route_body(lg_ref, rb_ref, cls_ref, glo_ref, ghi_ref):
    s = jax.nn.sigmoid(lg_ref[...])
    sb = s + rb_ref[...]
    rows = [sb[e:e + 1, :] for e in range(N_EXPERTS)]
    raw = [s[e:e + 1, :] for e in range(N_EXPERTS)]
    gscore = []
    for g in range(N_EXPERT_GROUPS):
        v = rows[g * EXPERTS_PER_GROUP:(g + 1) * EXPERTS_PER_GROUP]
        best = None
        for a in range(EXPERTS_PER_GROUP):
            for b in range(a + 1, EXPERTS_PER_GROUP):
                pair = v[a] + v[b]
                best = pair if best is None else jnp.maximum(best, pair)
        gscore.append(best)
    bg = jnp.zeros_like(gscore[0], dtype=jnp.int32)
    bs = gscore[0]
    for g in range(1, N_EXPERT_GROUPS):
        upd = gscore[g] > bs
        bg = jnp.where(upd, g, bg)
        bs = jnp.where(upd, gscore[g], bs)

    def pick(vals, k):
        out = vals[k]
        for g in range(1, N_EXPERT_GROUPS):
            out = jnp.where(bg == g, vals[g * EXPERTS_PER_GROUP + k], out)
        return out

    v = [pick(rows, k) for k in range(EXPERTS_PER_GROUP)]
    r = [pick(raw, k) for k in range(EXPERTS_PER_GROUP)]
    i1 = jnp.zeros_like(bg)
    b1 = v[0]
    for k in range(1, EXPERTS_PER_GROUP):
        upd = v[k] > b1
        i1 = jnp.where(upd, k, i1)
        b1 = jnp.where(upd, v[k], b1)
    i2 = jnp.zeros_like(bg)
    b2 = jnp.full_like(b1, -jnp.inf)
    for k in range(EXPERTS_PER_GROUP):
        upd = (i1 != k) & (v[k] > b2)
        i2 = jnp.where(upd, k, i2)
        b2 = jnp.where(upd, v[k], b2)
    s1 = r[0]
    s2 = r[0]
    for k in range(1, EXPERTS_PER_GROUP):
        s1 = jnp.where(i1 == k, r[k], s1)
        s2 = jnp.where(i2 == k, r[k], s2)
    den = s1 + s2
    g1 = s1 / den
    g2 = s2 / den
    lo = jnp.minimum(i1, i2)
    hi = jnp.maximum(i1, i2)
    base = jnp.where(lo == 0, 0, jnp.where(lo == 1, 3, 5))
    cls_ref[...] = bg * N_PAIRS + base + hi - lo - 1
    first_lo = i1 < i2
    glo_ref[...] = jnp.where(first_lo, g1, g2)
    ghi_ref[...] = jnp.where(first_lo, g2, g1)


def _route(logits, router_b):
    t, ne = logits.shape
    tt = min(2048, t)
    col = lambda i: (0, i)
    cls, glo, ghi = pl.pallas_call(
        _route_body,
        grid=(t // tt,),
        in_specs=[pl.BlockSpec((ne, tt), col), pl.BlockSpec((ne, 1), lambda i: (0, 0))],
        out_specs=[pl.BlockSpec((1, tt), col)] * 3,
        out_shape=[jax.ShapeDtypeStruct((1, t), jnp.int32), jax.ShapeDtypeStruct((1, t), F32),
                   jax.ShapeDtypeStruct((1, t), F32)],
        compiler_params=_cp("parallel"),
        name="moe_route",
    )(logits.T, router_b.reshape(ne, 1).astype(F32))
    return cls[0], glo[0], ghi[0]


def _moe_body(e1_ref, e2_ref, valid_ref, h_ref, g_ref, wg1_ref, wg2_ref, wd1_ref, wd2_ref, o_ref):
    i = pl.program_id(0)

    @pl.when(valid_ref[i] == 1)
    def _():
        h = h_ref[...]
        g = g_ref[...]

        def expert(wg_ref, wd_ref):
            gu = _dot(h, wg_ref[0])
            gt = gu[:, :D_EXPERT]
            act = (gt * jax.nn.sigmoid(gt) * gu[:, D_EXPERT:]).astype(BF16)
            return _dot(act, wd_ref[0])

        y = g[:, 0:1] * expert(wg1_ref, wd1_ref) + g[:, 1:2] * expert(wg2_ref, wd2_ref)
        o_ref[...] = y.astype(BF16)

    @pl.when(valid_ref[i] == 0)
    def _():
        o_ref[...] = jnp.zeros_like(o_ref)


_PAIR_LO = (0, 0, 0, 1, 1, 2)
_PAIR_HI = (1, 2, 3, 2, 3, 3)


def _moe(h, cls, glo, ghi, w_gate_up, w_down):
    t, d = h.shape
    tm = MOE_TILE
    n_tiles = t // tm + N_CLASSES
    rows = n_tiles * tm
    onehot = (cls[:, None] == jnp.arange(N_CLASSES)[None, :]).astype(jnp.int32)
    csum = jnp.cumsum(onehot, 0)
    rank = jnp.take_along_axis(csum, cls[:, None], 1)[:, 0] - 1
    counts = csum[-1]
    padded = -(-counts // tm) * tm
    ends = jnp.cumsum(padded)
    dest = (ends - padded)[cls] + rank
    src = jnp.zeros((rows,), jnp.int32).at[dest].set(jnp.arange(t, dtype=jnp.int32))
    tile_start = jnp.arange(n_tiles, dtype=jnp.int32) * tm
    valid = (tile_start < ends[-1]).astype(jnp.int32)
    last_cls = jnp.max(jnp.where(counts > 0, jnp.arange(N_CLASSES), 0))
    tile_cls = jnp.minimum(jnp.searchsorted(ends, tile_start, side="right"), last_cls).astype(jnp.int32)
    grp = tile_cls // N_PAIRS
    pair = tile_cls % N_PAIRS
    e1 = (grp * EXPERTS_PER_GROUP + jnp.asarray(_PAIR_LO, jnp.int32)[pair]).astype(jnp.int32)
    e2 = (grp * EXPERTS_PER_GROUP + jnp.asarray(_PAIR_HI, jnp.int32)[pair]).astype(jnp.int32)
    hs = jnp.take(h, src, axis=0)
    gs = jnp.take(jnp.stack([glo, ghi], 1), src, axis=0)
    wgu = w_gate_up.astype(BF16)
    wdn = w_down.astype(BF16)
    ys = pl.pallas_call(
        _moe_body,
        grid_spec=pltpu.PrefetchScalarGridSpec(
            num_scalar_prefetch=3,
            grid=(n_tiles,),
            in_specs=[
                pl.BlockSpec((tm, d), lambda i, e1, e2, va: (i, 0)),
                pl.BlockSpec((tm, 2), lambda i, e1, e2, va: (i, 0)),
                pl.BlockSpec((1, d, 2 * D_EXPERT), lambda i, e1, e2, va: (e1[i], 0, 0)),
                pl.BlockSpec((1, d, 2 * D_EXPERT), lambda i, e1, e2, va: (e2[i], 0, 0)),
                pl.BlockSpec((1, D_EXPERT, d), lambda i, e1, e2, va: (e1[i], 0, 0)),
                pl.BlockSpec((1, D_EXPERT, d), lambda i, e1, e2, va: (e2[i], 0, 0)),
            ],
            out_specs=pl.BlockSpec((tm, d), lambda i, e1, e2, va: (i, 0)),
        ),
        out_shape=jax.ShapeDtypeStruct((rows, d), BF16),
        compiler_params=_cp("arbitrary"),
        name="moe_experts",
    )(e1, e2, valid, hs, gs, wgu, wgu, wdn, wdn)
    return jnp.take(ys, dest, axis=0)


def _resid_ln_body(x_ref, y_ref, mod_ref, g_ref, b_ref, o_ref):
    m = mod_ref[0]
    o_ref[...] = _layernorm(DN_ALPHA * x_ref[...] + (1.0 + m[5:6]) * y_ref[...].astype(F32),
                            g_ref[...], b_ref[...])


def _resid_ln(x2, y, mod, ln_g, ln_b, seq):
    t, d = x2.shape
    tm = 512
    per_b = seq // tm
    row = lambda i: (i, 0)
    const = lambda i: (0, 0)
    return pl.pallas_call(
        _resid_ln_body,
        grid=(t // tm,),
        in_specs=[pl.BlockSpec((tm, d), row), pl.BlockSpec((tm, d), row),
                  pl.BlockSpec((1, 6, d), lambda i: (i // per_b, 0, 0)),
                  pl.BlockSpec((1, d), const), pl.BlockSpec((1, d), const)],
        out_specs=pl.BlockSpec((tm, d), row),
        out_shape=jax.ShapeDtypeStruct((t, d), F32),
        compiler_params=_cp("parallel"),
        name="resid_ln",
    )(x2, y, mod, ln_g.reshape(1, d), ln_b.reshape(1, d))


def _front1_body(x_ref, mod_ref, w_ref, o_ref, h_scr):
    @pl.when(pl.program_id(1) == 0)
    def _():
        m = mod_ref[0]
        h_scr[...] = (x_ref[...] * (1.0 + m[1:2]) + m[0:1]).astype(BF16)

    o_ref[...] = _dot(h_scr[...], w_ref[...]).astype(BF16)


def _front1(x2, mod, od_w_in, seq):
    t, d = x2.shape
    tm = min(1024, seq)
    tn = 512
    per_b = seq // tm
    n_in = od_w_in.shape[1]
    n = -(-n_in // tn) * tn
    w = jnp.pad(od_w_in, ((0, 0), (0, n - n_in))).astype(BF16)
    return pl.pallas_call(
        _front1_body,
        grid=(t // tm, n // tn),
        in_specs=[pl.BlockSpec((tm, d), lambda i, j: (i, 0)),
                  pl.BlockSpec((1, 6, d), lambda i, j: (i // per_b, 0, 0)),
                  pl.BlockSpec((d, tn), lambda i, j: (0, j))],
        out_specs=pl.BlockSpec((tm, tn), lambda i, j: (i, j)),
        out_shape=jax.ShapeDtypeStruct((t, n), BF16),
        scratch_shapes=[pltpu.VMEM((tm, d), BF16)],
        compiler_params=_cp("parallel", "arbitrary"),
        name="front1",
    )(x2, mod, w)


def _dil_body(q_ref, kp_ref, kc_ref, vp_ref, vc_ref, bias_ref, o_ref, lse_ref):
    n = pl.program_id(2)
    span = DIL_SPAN
    q = q_ref[0]
    k = jnp.concatenate([kp_ref[0], kc_ref[0]], 0)
    v = jnp.concatenate([vp_ref[0], vc_ref[0]], 0)
    qi = lax.broadcasted_iota(jnp.int32, (span, 2 * span), 0)
    ku = lax.broadcasted_iota(jnp.int32, (span, 2 * span), 1)
    valid = (ku >= qi) & (ku <= qi + span) & ((n >= 1) | (ku >= span))
    lane = lax.broadcasted_iota(jnp.int32, (span, LANES), 1)
    first = lane < DIL_HEAD_DIM
    lse_all = jnp.zeros((span, LANES), F32)
    scale = DIL_HEAD_DIM ** -0.5
    for hp in range(DIL_HEADS // 2):
        sl = slice(hp * LANES, (hp + 1) * LANES)
        q2, k2, v2 = q[:, sl], k[:, sl], v[:, sl]
        outs = []
        for hh in range(2):
            head = 2 * hp + hh
            qm = jnp.where(first if hh == 0 else jnp.logical_not(first), q2, jnp.zeros_like(q2))
            s = _dot_nt(qm, k2) * scale + bias_ref[head]
            s = jnp.where(valid, s, NEG_INF)
            m = jnp.max(s, -1, keepdims=True)
            p = jnp.exp(s - m)
            l = jnp.sum(p, -1, keepdims=True)
            outs.append(_dot(p.astype(BF16), v2) / l)
            lse_all = jnp.where(lane == head, m + jnp.log(l), lse_all)
        o_ref[0, :, sl] = jnp.where(first, outs[0], outs[1]).astype(BF16)
    lse_ref[0] = lse_all


def _t5_bucket(dist):
    exact = T5_BUCKETS // 2
    logd = jnp.log(jnp.maximum(dist, 1).astype(F32) / exact) / math.log(T5_MAX_DIST / exact)
    large = jnp.minimum(exact + (logd * (T5_BUCKETS - exact)).astype(jnp.int32), T5_BUCKETS - 1)
    return jnp.where(dist < exact, dist, large)


def _dilated_group(z, rel_bias, gi, dil, batch, seq):
    span = DIL_SPAN
    n_cols = z.shape[1]
    blocks_per_row = n_cols // DIL_WIDTH
    l_stream = seq // dil
    nb = l_stream // span
    zv = z.reshape(batch, l_stream, dil * n_cols)
    qi = jnp.arange(span)[:, None]
    ku = jnp.arange(2 * span)[None, :]
    delta = jnp.clip(span + qi - ku, 0, span) * dil
    bias = rel_bias.astype(F32)[_t5_bucket(delta)][:, :, gi * DIL_HEADS:(gi + 1) * DIL_HEADS].transpose(2, 0, 1)

    def col(j):
        return lambda b, r, n: (b, n, r * blocks_per_row + gi * 3 + j)

    def col_prev(j):
        return lambda b, r, n: (b, jnp.maximum(n - 1, 0), r * blocks_per_row + gi * 3 + j)

    blk = (1, span, DIL_WIDTH)
    o, lse = pl.pallas_call(
        _dil_body,
        grid=(batch, dil, nb),
        in_specs=[pl.BlockSpec(blk, col(0)), pl.BlockSpec(blk, col_prev(1)), pl.BlockSpec(blk, col(1)),
                  pl.BlockSpec(blk, col_prev(2)), pl.BlockSpec(blk, col(2)),
                  pl.BlockSpec((DIL_HEADS, span, 2 * span), lambda b, r, n: (0, 0, 0))],
        out_specs=[pl.BlockSpec(blk, lambda b, r, n: (b, n, r)),
                   pl.BlockSpec((1, span, LANES), lambda b, r, n: (b, n, r))],
        out_shape=[jax.ShapeDtypeStruct((batch, l_stream, dil * DIL_WIDTH), BF16),
                   jax.ShapeDtypeStruct((batch, l_stream, dil * LANES), F32)],
        compiler_params=_cp("parallel", "parallel", "arbitrary"),
        name=f"dilated_attention_{dil}",
    )(zv, zv, zv, zv, zv, bias)
    return o.reshape(batch * seq, DIL_WIDTH), lse.reshape(batch * seq, LANES)


def _dil_merge_body(o1_ref, o2_ref, o3_ref, l1_ref, l2_ref, l3_ref, e_ref, out_ref):
    lses = [l1_ref[...], l2_ref[...], l3_ref[...]]
    top = jnp.maximum(jnp.maximum(lses[0], lses[1]), lses[2])
    num = None
    den = None
    for o_ref, lse in zip((o1_ref, o2_ref, o3_ref), lses):
        w = _dot_exact_rhs(jnp.exp(lse - top), e_ref[...], 2)
        num = w * o_ref[...].astype(F32) if num is None else num + w * o_ref[...].astype(F32)
        den = w if den is None else den + w
    out_ref[...] = (num / den).astype(BF16)


def _dil_merge(outs, lses):
    t = outs[0].shape[0]
    tm = 512
    row = lambda i: (i, 0)
    head_of_lane = jnp.arange(DIL_WIDTH) // DIL_HEAD_DIM
    expand = (jnp.arange(LANES)[:, None] == head_of_lane[None, :]).astype(BF16)
    return pl.pallas_call(
        _dil_merge_body,
        grid=(t // tm,),
        in_specs=[pl.BlockSpec((tm, DIL_WIDTH), row)] * 3 + [pl.BlockSpec((tm, LANES), row)] * 3
        + [pl.BlockSpec((LANES, DIL_WIDTH), lambda i: (0, 0))],
        out_specs=pl.BlockSpec((tm, DIL_WIDTH), row),
        out_shape=jax.ShapeDtypeStruct((t, DIL_WIDTH), BF16),
        compiler_params=_cp("parallel"),
        name="dilated_merge",
    )(*outs, *lses, expand)


def _rwkv_body(zr_ref, zk_ref, zv_ref, zm_ref, mur_ref, muk_ref, muv_ref, mum_ref, w0_ref, w2_ref,
               a0_ref, a2_ref, g2_ref, kk_ref, ka_ref, rk_ref, lng_ref, lnb_ref, o_ref,
               st_ref, pr_ref, pk_ref, pv_ref, pm_ref, xr_s, xk_s, xv_s, lw_s, as_s, g_s, cum_s):
    lc = RW_CHUNK
    n_slabs = st_ref.shape[0]

    @pl.when(pl.program_id(1) == 0)
    def _():
        st_ref[...] = jnp.zeros_like(st_ref)
        pr_ref[...] = jnp.zeros_like(pr_ref)
        pk_ref[...] = jnp.zeros_like(pk_ref)
        pv_ref[...] = jnp.zeros_like(pv_ref)
        pm_ref[...] = jnp.zeros_like(pm_ref)

    def shifted(z_ref, prev_ref, mu_ref):
        z = z_ref[...].astype(F32)
        first_row = lax.broadcasted_iota(jnp.int32, z.shape, 0) == 0
        prev = jnp.where(first_row, prev_ref[...], pltpu.roll(z, 1, 0))
        prev_ref[...] = z[lc - 1:lc, :]
        return z + (prev - z) * mu_ref[...]

    xr_s[...] = shifted(zr_ref, pr_ref, mur_ref)
    xk_s[...] = shifted(zk_ref, pk_ref, muk_ref)
    xv_s[...] = shifted(zv_ref, pv_ref, muv_ref)
    xm = shifted(zm_ref, pm_ref, mum_ref)
    lora_in = xm[:, :LANES]
    wl = w0_ref[...] + _dot(jnp.tanh(lora_in).astype(BF16), w2_ref[...])
    w_log = -(jnp.maximum(-wl, 0.0) + jnp.log(1.0 + jnp.exp(-jnp.abs(wl)))) - 0.5
    lw = -jnp.exp(w_log)
    lw_s[...] = lw
    as_s[...] = jax.nn.sigmoid(a0_ref[...] + _dot(lora_in.astype(BF16), a2_ref[...]))
    g_s[...] = _dot(jax.nn.sigmoid(xm[:, LANES:]).astype(BF16), g2_ref[...])
    ti = lax.broadcasted_iota(jnp.int32, (lc, lc), 0)
    si = lax.broadcasted_iota(jnp.int32, (lc, lc), 1)
    cum_s[...] = _dot_exact_lhs((si <= ti).astype(BF16), lw, 3)

    row = lax.broadcasted_iota(jnp.int32, (2 * lc, 2 * lc), 0)
    colm = lax.broadcasted_iota(jnp.int32, (2 * lc, 2 * lc), 1)
    same_head = (row >= lc) == (colm >= lc)
    t_in = row & (lc - 1)
    s_in = colm & (lc - 1)
    strict = same_head & (s_in < t_in)
    incl = same_head & (s_in <= t_in)
    eye = row == colm
    ones_bd = same_head.astype(BF16)
    lane = lax.broadcasted_iota(jnp.int32, (lc, LANES), 1)
    first = lane < RW_HEAD

    def seg(x):
        return _dot_exact_rhs(x, ones_bd, 2)

    def stack_heads(x):
        return jnp.concatenate([jnp.where(first, x, 0.0), jnp.where(first, 0.0, x)], 0)

    for hp in range(n_slabs):
        sl = slice(hp * LANES, (hp + 1) * LANES)
        r_p = xr_s[:, sl]
        k_p = xk_s[:, sl]
        v_p = xv_s[:, sl]
        lw_p = lw_s[:, sl]
        a_sig = as_s[:, sl]
        cum = cum_s[:, sl]
        kk = k_p * kk_ref[:, sl]
        kk = kk / jnp.maximum(jnp.sqrt(seg(kk * kk)), 1e-12)
        k2 = k_p * (1.0 + (a_sig - 1.0) * ka_ref[:, sl])
        b_p = kk * a_sig
        cl = cum[lc - 1:lc, :]
        e_neg = jnp.exp(-cum)
        e_end = jnp.exp(cl - cum)
        at = -kk * jnp.exp(cum - lw_p)
        rt = r_p * jnp.exp(cum)
        v_b = v_p.astype(BF16)
        lhs = jnp.concatenate([stack_heads(at), stack_heads(rt)], 0).astype(BF16)
        bt = (b_p * e_neg).astype(BF16)
        kt = (k2 * e_neg).astype(BF16)
        rhs = jnp.concatenate([bt, bt, kt, kt], 0)
        qm = _dot_nt(lhs, rhs)
        a_ab = jnp.where(strict, qm[:2 * lc, :2 * lc], 0.0)
        a_ak = jnp.where(strict, qm[:2 * lc, 2 * lc:], 0.0)
        a_rb = jnp.where(incl, qm[2 * lc:, :2 * lc], 0.0)
        a_rk = jnp.where(incl, qm[2 * lc:, 2 * lc:], 0.0)
        tinv = jnp.where(eye, 1.0, 0.0) + a_ab
        pw = a_ab
        for _ in range(int(math.log2(lc)) - 1):
            pw_b = pw.astype(BF16)
            pw = _dot(pw_b, pw_b)
            tinv = tinv + _dot(tinv.astype(BF16), pw.astype(BF16))
        state = st_ref[hp]
        ars = _dot(lhs, state.astype(BF16))
        vs = jnp.concatenate([v_b, v_b], 0)
        gmat = _dot(a_ak.astype(BF16), vs) + ars[:2 * lc]
        us = _dot(tinv.astype(BF16), gmat.astype(BF16))
        ys = _dot(jnp.concatenate([a_rb, a_rk], 1).astype(BF16),
                  jnp.concatenate([us.astype(BF16), vs], 0)) + ars[2 * lc:]
        u = jnp.where(first, us[:lc], us[lc:])
        y = jnp.where(first, ys[:lc], ys[lc:])
        bk = jnp.concatenate([b_p * e_end, k2 * e_end], 0)
        uv = jnp.concatenate([u.astype(BF16), v_b], 0)
        upd = _dot(bk.T.astype(BF16), uv)
        pl_col = jnp.sum(jnp.where(eye, jnp.broadcast_to(jnp.exp(cl), (2 * lc, LANES)), 0.0), 1, keepdims=True)
        st_ref[hp] = pl_col * state + jnp.where(same_head, upd, 0.0)
        mean = seg(y) * (1.0 / RW_HEAD)
        dy = y - mean
        var = seg(dy * dy) * (1.0 / RW_HEAD)
        yn = dy * lax.rsqrt(var + RW_GN_EPS) * lng_ref[:, sl] + lnb_ref[:, sl]
        bonus = seg(r_p * k2 * rk_ref[:, sl]) * v_p
        o_ref[:, sl] = ((yn + bonus) * g_s[:, sl]).astype(BF16)


def _rwkv(z, batch, seq, mu, w0, w2, a0, a2, g2, k_k, k_a, r_k, lnx_g, lnx_b):
    t = z.shape[0]
    lc = RW_CHUNK
    nc = seq // lc
    width = w0.shape[0]
    n_slabs = width // LANES
    col0 = DIL_IN // width
    misc_blk = (DIL_IN + 3 * width) // RW_MISC
    n_lora = RW_LORA_W + RW_LORA_A + RW_LORA_G
    mu_m = jnp.pad(mu[3 * width:], (0, RW_MISC - n_lora)).reshape(1, RW_MISC)
    w2p = jnp.pad(w2, ((0, LANES - RW_LORA_W), (0, 0))).astype(BF16)
    a2p = jnp.pad(a2, ((RW_LORA_W, LANES - RW_LORA_W - RW_LORA_A), (0, 0))).astype(BF16)
    g2p = jnp.pad(g2, ((0, RW_MISC - LANES - RW_LORA_G), (0, 0))).astype(BF16)
    vec = lambda a: a.reshape(1, width).astype(F32)
    const = lambda b, c: (0, 0)
    wide = lambda: pl.BlockSpec((1, width), const)
    return pl.pallas_call(
        _rwkv_body,
        grid=(batch, nc),
        in_specs=[
            pl.BlockSpec((lc, width), lambda b, c: (b * nc + c, col0)),
            pl.BlockSpec((lc, width), lambda b, c: (b * nc + c, col0 + 1)),
            pl.BlockSpec((lc, width), lambda b, c: (b * nc + c, col0 + 2)),
            pl.BlockSpec((lc, RW_MISC), lambda b, c: (b * nc + c, misc_blk)),
            wide(), wide(), wide(), pl.BlockSpec((1, RW_MISC), const),
            wide(), pl.BlockSpec((LANES, width), const),
            wide(), pl.BlockSpec((LANES, width), const),
            pl.BlockSpec((RW_MISC - LANES, width), const),
            wide(), wide(), wide(), wide(), wide(),
        ],
        out_specs=pl.BlockSpec((lc, width), lambda b, c: (b * nc + c, 0)),
        out_shape=jax.ShapeDtypeStruct((t, width), BF16),
        scratch_shapes=[pltpu.VMEM((n_slabs, LANES, LANES), F32),
                        pltpu.VMEM((1, width), F32), pltpu.VMEM((1, width), F32),
                        pltpu.VMEM((1, width), F32), pltpu.VMEM((1, RW_MISC), F32)]
        + [pltpu.VMEM((lc, width), F32)] * 7,
        compiler_params=_cp("parallel", "arbitrary"),
        name="rwkv7",
    )(z, z, z, z, vec(mu[:width]), vec(mu[width:2 * width]), vec(mu[2 * width:3 * width]), mu_m,
      vec(w0), w2p, vec(a0), a2p, g2p, vec(k_k), vec(k_a), vec(r_k), vec(lnx_g), vec(lnx_b))


def kernel(x, c, ada_w, ada_b, ln_mix_g, ln_mix_b, ln_ffn_g, ln_ffn_b, router_w, router_b, moe_w_gate_up, moe_w_down, rel_bias, ev_w_in, mla_q_norm, mla_w_uq, mla_kv_norm, mla_w_ukv, s5_lambda_re, s5_lambda_im, s5_b_re, s5_b_im, s5_c_re, s5_c_im, s5_d, s5_log_dt, s5_w_glu, ev_w_out, od_w_in, rw_mu, rw_w0, rw_w2, rw_a0, rw_a2, rw_g2, rw_k_k, rw_k_a, rw_r_k, rw_lnx_g, rw_lnx_b, od_w_out):
    batch, seq, d = x.shape
    assert seq % 2048 == 0 and d % LANES == 0
    x2 = x.reshape(batch * seq, d)
    mods = _ada(c, ada_w, ada_b)
    for layer in range(DEPTH):
        mod = mods[layer]
        if layer % 2 == 0:
            e = layer // 2
            q, k, v, u = _front0(x2, mod, seq, ev_w_in[e], mla_q_norm[e], mla_w_uq[e], mla_kv_norm[e],
                                 mla_w_ukv[e])
            att = _flash(q, k, v, batch, seq)
            n_steps = max(1, math.ceil(math.log2(seq // S5_CHUNK)))
            mats = _s5_matrices(s5_lambda_re[e], s5_lambda_im[e], s5_b_re[e], s5_b_im[e], s5_c_re[e],
                                s5_c_im[e], s5_log_dt[e], n_steps)
            ssm = _s5_glu(_s5_scan(u, batch, seq, mats), u, s5_d[e], s5_w_glu[e])
            x2, h, logits = _outproj(att, ssm, ev_w_out[e], x2, mod, ln_mix_g[layer], ln_mix_b[layer],
                                     router_w, seq)
        else:
            o = layer // 2
            z = _front1(x2, mod, od_w_in[o], seq)
            outs, lses = [], []
            for gi, (win, dil) in enumerate(DIL_PATTERNS):
                assert win // dil == DIL_SPAN
                og, lg = _dilated_group(z, rel_bias, gi, dil, batch, seq)
                outs.append(og)
                lses.append(lg)
            att = _dil_merge(outs, lses)
            tm_out = _rwkv(z, batch, seq, rw_mu[o], rw_w0[o], rw_w2[o], rw_a0[o], rw_a2[o], rw_g2[o],
                           rw_k_k[o], rw_k_a[o], rw_r_k[o], rw_lnx_g[o], rw_lnx_b[o])
            x2, h, logits = _outproj(att, tm_out, od_w_out[o], x2, mod, ln_mix_g[layer], ln_mix_b[layer],
                                     router_w, seq)
        cls, glo, ghi = _route(logits, router_b)
        y = _moe(h, cls, glo, ghi, moe_w_gate_up[layer], moe_w_down[layer])
        x2 = _resid_ln(x2, y, mod, ln_ffn_g[layer], ln_ffn_b[layer], seq)
    return x2.reshape(batch, seq, d)
```

```python
import functools
import math

import jax
import jax.numpy as jnp
from jax import lax
from jax.experimental import pallas as pl
from jax.experimental.pallas import tpu as pltpu

F32 = jnp.float32
BF16 = jnp.bfloat16

DEPTH = 2
DN_ALPHA = (2.0 * DEPTH) ** 0.25
LN_EPS = 1e-5
RMS_EPS = 1e-6
NEG_INF = -1e30

MLA_HEADS = 8
MLA_NOPE = 128
MLA_ROPE = 64
MLA_V = 128
MLA_QK = MLA_NOPE + MLA_ROPE
MLA_Q_RANK = 512
MLA_KV_RANK = 256
ROPE_THETA = 10000.0
MLA_HEAD_PAD = 256

S5_GROUP = 16
S5_STATE = 64
S5_CHUNK = 16

DIL_PATTERNS = ((128, 1), (512, 4), (2048, 16))
DIL_SPAN = 128
DIL_HEADS = 8
DIL_HEAD_DIM = 64
DIL_WIDTH = DIL_HEADS * DIL_HEAD_DIM
DIL_IN = len(DIL_PATTERNS) * 3 * DIL_WIDTH
T5_BUCKETS = 32
T5_MAX_DIST = 2048

RW_HEAD = 64
RW_LORA_W = 64
RW_LORA_A = 64
RW_LORA_G = 224
RW_GN_EPS = 64e-5
RW_CHUNK = 64
RW_MISC = 512

N_EXPERTS = 16
N_EXPERT_GROUPS = 4
EXPERTS_PER_GROUP = 4
D_EXPERT = 512
N_PAIRS = 6
N_CLASSES = N_EXPERT_GROUPS * N_PAIRS
MOE_TILE = 256

LANES = 128
VMEM_LIMIT = 56 * 1024 * 1024


def _cp(*sem):
    return pltpu.CompilerParams(dimension_semantics=sem, vmem_limit_bytes=VMEM_LIMIT)


def _dot(a, b):
    return jnp.dot(a, b, preferred_element_type=F32)


def _dot_nt(a, b):
    return lax.dot_general(a, b, (((1,), (1,)), ((), ())), preferred_element_type=F32)


def _split_bf16(x, parts):
    out = []
    for _ in range(parts):
        hi = x.astype(BF16)
        out.append(hi)
        x = x - hi.astype(F32)
    return out


def _dot_exact_rhs(x, w_bf16, parts=3):
    acc = None
    for p in _split_bf16(x, parts):
        t = _dot(p, w_bf16)
        acc = t if acc is None else acc + t
    return acc


def _dot_exact_lhs(w_bf16, x, parts=3):
    acc = None
    for p in _split_bf16(x, parts):
        t = _dot(w_bf16, p)
        acc = t if acc is None else acc + t
    return acc


def _layernorm(x, g, b):
    mu = jnp.mean(x, -1, keepdims=True)
    d = x - mu
    var = jnp.mean(d * d, -1, keepdims=True)
    return d * lax.rsqrt(var + LN_EPS) * g + b


def _ada_body(c_ref, w_ref, b_ref, o_ref):
    c = c_ref[...]
    cond = c * jax.nn.sigmoid(c)
    o_ref[0] = _dot_exact_rhs(cond, w_ref[0].astype(BF16), 2) + b_ref[0]


def _ada(c, ada_w, ada_b):
    depth, d, n = ada_w.shape
    b = c.shape[0]
    rows = -(-b // 8) * 8
    cp = jnp.pad(c, ((0, rows - b), (0, 0)))
    tn = 1024
    out = pl.pallas_call(
        _ada_body,
        grid=(depth, n // tn),
        in_specs=[
            pl.BlockSpec((rows, d), lambda l, j: (0, 0)),
            pl.BlockSpec((1, d, tn), lambda l, j: (l, 0, j)),
            pl.BlockSpec((1, 1, tn), lambda l, j: (l, 0, j)),
        ],
        out_specs=pl.BlockSpec((1, rows, tn), lambda l, j: (l, 0, j)),
        out_shape=jax.ShapeDtypeStruct((depth, rows, n), F32),
        compiler_params=_cp("parallel", "parallel"),
        name="ada",
    )(cp, ada_w, ada_b.reshape(depth, 1, n))
    return out[:, :b].reshape(depth, b, 6, d)


def _front0_body(x_ref, mod_ref, cs_ref, win_ref, qn_ref, kvn_ref, wq1_ref, wq2_ref, wkn_ref,
                 wv_ref, q_ref, k_ref, v_ref, u_ref):
    m = mod_ref[0]
    h = (x_ref[...] * (1.0 + m[1:2]) + m[0:1]).astype(BF16)
    z = _dot(h, win_ref[...])
    cs = cs_ref[...]
    c2 = cs[:, :LANES]
    s2 = cs[:, LANES:]
    q_c = z[:, :MLA_Q_RANK]
    kv_c = z[:, MLA_Q_RANK:MLA_Q_RANK + MLA_KV_RANK]
    o = MLA_Q_RANK + MLA_KV_RANK
    krope = (z[:, o:o + LANES] * c2 + z[:, o + LANES:o + 2 * LANES] * s2).astype(BF16)
    u_ref[...] = z[:, o + 2 * LANES:]
    hq = (q_c * lax.rsqrt(jnp.mean(q_c * q_c, -1, keepdims=True) + RMS_EPS) * qn_ref[...]).astype(BF16)
    hkv = (kv_c * lax.rsqrt(jnp.mean(kv_c * kv_c, -1, keepdims=True) + RMS_EPS) * kvn_ref[...]).astype(BF16)
    full = _dot(hq, wq1_ref[...])
    sw = _dot(hq, wq2_ref[...])
    kn = _dot(hkv, wkn_ref[...])
    v_ref[...] = _dot(hkv, wv_ref[...]).astype(BF16)
    for j in range(MLA_HEADS):
        a = j * MLA_HEAD_PAD
        q_ref[:, a:a + LANES] = full[:, a:a + LANES].astype(BF16)
        q_ref[:, a + LANES:a + 2 * LANES] = (
            full[:, a + LANES:a + 2 * LANES] * c2 + sw[:, j * LANES:(j + 1) * LANES] * s2).astype(BF16)
        k_ref[:, a:a + LANES] = kn[:, j * LANES:(j + 1) * LANES].astype(BF16)
        k_ref[:, a + LANES:a + 2 * LANES] = krope


def _rope_swapped(w):
    half = w.shape[-1] // 2
    return jnp.concatenate([-w[..., half:], w[..., :half]], -1)


def _front0(x2, mod, seq, ev_w_in, q_norm, w_uq, kv_norm, w_ukv):
    t, d = x2.shape
    tm = 256
    per_b = seq // tm
    zpad = jnp.zeros((d, LANES - MLA_ROPE), F32)
    o = MLA_Q_RANK + MLA_KV_RANK
    w_kr = ev_w_in[:, o:o + MLA_ROPE]
    win = jnp.concatenate([ev_w_in[:, :o], w_kr, zpad, _rope_swapped(w_kr), zpad,
                           ev_w_in[:, o + MLA_ROPE:]], 1).astype(BF16)
    scale = MLA_QK ** -0.5
    wq = w_uq.reshape(MLA_Q_RANK, MLA_HEADS, MLA_QK) * scale
    zq = jnp.zeros((MLA_Q_RANK, MLA_HEADS, MLA_HEAD_PAD - MLA_QK), F32)
    wq1 = jnp.concatenate([wq, zq], -1).reshape(MLA_Q_RANK, -1).astype(BF16)
    wq2 = jnp.concatenate([_rope_swapped(wq[..., MLA_NOPE:]), zq], -1).reshape(MLA_Q_RANK, -1).astype(BF16)
    wkv = w_ukv.reshape(MLA_KV_RANK, MLA_HEADS, MLA_NOPE + MLA_V)
    wkn = wkv[..., :MLA_NOPE].reshape(MLA_KV_RANK, -1).astype(BF16)
    wv = wkv[..., MLA_NOPE:].reshape(MLA_KV_RANK, -1).astype(BF16)
    inv = ROPE_THETA ** (-jnp.arange(0, MLA_ROPE, 2, dtype=F32) / MLA_ROPE)
    ang = jnp.arange(seq, dtype=F32)[:, None] * inv[None]
    zt = jnp.zeros((seq, LANES - MLA_ROPE), F32)
    cs = jnp.concatenate([jnp.cos(ang), jnp.cos(ang), zt, jnp.sin(ang), jnp.sin(ang), zt], 1)
    const = lambda i: (0, 0)
    hq_w = MLA_HEADS * MLA_HEAD_PAD
    hv_w = MLA_HEADS * MLA_V
    s5w = ev_w_in.shape[1] - o - MLA_ROPE
    return pl.pallas_call(
        _front0_body,
        grid=(t // tm,),
        in_specs=[
            pl.BlockSpec((tm, d), lambda i: (i, 0)),
            pl.BlockSpec((1, 6, d), lambda i: (i // per_b, 0, 0)),
            pl.BlockSpec((tm, 2 * LANES), lambda i: (i % per_b, 0)),
            pl.BlockSpec(win.shape, const),
            pl.BlockSpec((1, MLA_Q_RANK), const),
            pl.BlockSpec((1, MLA_KV_RANK), const),
            pl.BlockSpec(wq1.shape, const),
            pl.BlockSpec(wq2.shape, const),
            pl.BlockSpec(wkn.shape, const),
            pl.BlockSpec(wv.shape, const),
        ],
        out_specs=[
            pl.BlockSpec((tm, hq_w), lambda i: (i, 0)),
            pl.BlockSpec((tm, hq_w), lambda i: (i, 0)),
            pl.BlockSpec((tm, hv_w), lambda i: (i, 0)),
            pl.BlockSpec((tm, s5w), lambda i: (i, 0)),
        ],
        out_shape=[
            jax.ShapeDtypeStruct((t, hq_w), BF16),
            jax.ShapeDtypeStruct((t, hq_w), BF16),
            jax.ShapeDtypeStruct((t, hv_w), BF16),
            jax.ShapeDtypeStruct((t, s5w), F32),
        ],
        compiler_params=_cp("parallel"),
        name="front0",
    )(x2, mod, cs, win, q_norm.reshape(1, -1), kv_norm.reshape(1, -1), wq1, wq2, wkn, wv)


def _flash_body(q_ref, k_ref, v_ref, o_ref, *, tile):
    i = pl.program_id(2)
    q = q_ref[...]

    def step(j, carry, diagonal):
        m, l, acc = carry
        start = pl.multiple_of(j * tile, tile)
        k = k_ref[pl.ds(start, tile), :]
        v = v_ref[pl.ds(start, tile), :]
        s = _dot_nt(q, k)
        if diagonal:
            row = lax.broadcasted_iota(jnp.int32, s.shape, 0)
            col = lax.broadcasted_iota(jnp.int32, s.shape, 1)
            s = jnp.where(col <= row, s, NEG_INF)
        m_new = jnp.maximum(m, jnp.max(s, -1, keepdims=True))
        alpha = jnp.exp(m - m_new)
        p = jnp.exp(s - m_new)
        l = l * alpha + jnp.sum(p, -1, keepdims=True)
        acc = acc * alpha + _dot(p.astype(BF16), v)
        return m_new, l, acc

    init = (jnp.full((tile, 1), NEG_INF, F32), jnp.zeros((tile, 1), F32),
            jnp.zeros((tile, MLA_V), F32))
    carry = lax.fori_loop(0, i, lambda j, c: step(j, c, False), init)
    _, l, acc = step(i, carry, True)
    o_ref[...] = (acc / l).astype(BF16)


def _flash(q, k, v, batch, seq):
    t = q.shape[0]
    tile = 512
    nq = seq // tile
    return pl.pallas_call(
        functools.partial(_flash_body, tile=tile),
        grid=(batch, MLA_HEADS, nq),
        in_specs=[
            pl.BlockSpec((tile, MLA_HEAD_PAD), lambda b, h, i: (b * nq + i, h)),
            pl.BlockSpec((seq, MLA_HEAD_PAD), lambda b, h, i: (b, h)),
            pl.BlockSpec((seq, MLA_V), lambda b, h, i: (b, h)),
        ],
        out_specs=pl.BlockSpec((tile, MLA_V), lambda b, h, i: (b * nq + i, h)),
        out_shape=jax.ShapeDtypeStruct((t, MLA_HEADS * MLA_V), BF16),
        compiler_params=_cp("parallel", "parallel", "arbitrary"),
        name="mla_attention",
    )(q, k, v)


def _s5_body(u_ref, t_ref, w_ref, v_ref, ac_ref, bc_ref, y_ref, *, n_chunks, n_steps):
    u = u_ref[0]
    z = _dot(u, w_ref[0])
    cidx = lax.rem(lax.broadcasted_iota(jnp.int32, z.shape, 0), n_chunks)
    half = S5_STATE
    for step in range(n_steps):
        d = 1 << step
        zs = jnp.where(cidx >= d, pltpu.roll(z, d, 0), 0.0)
        z = z + zs * ac_ref[0, step:step + 1, :] + pltpu.roll(zs, half, 1) * bc_ref[0, step:step + 1, :]
    xprev = jnp.where(cidx >= 1, pltpu.roll(z, 1, 0), 0.0)
    y_ref[0] = _dot(u, t_ref[0]) + _dot(xprev.astype(BF16), v_ref[0])


def _s5_matrices(lam_re, lam_im, b_re, b_im, c_re, c_im, log_dt, n_steps):
    lc = S5_CHUNK
    lam = lax.complex(lam_re.astype(F32), lam_im.astype(F32))
    dt = jnp.exp(log_dt.astype(F32))[:, None]
    lam_bar = jnp.exp(lam * dt)
    b_bar = ((lam_bar - 1.0) / lam)[..., None] * lax.complex(b_re.astype(F32), b_im.astype(F32))
    c = lax.complex(c_re.astype(F32), c_im.astype(F32))
    j = jnp.arange(lc + 1, dtype=F32)[:, None, None]
    pows = jnp.exp(j * (lam * dt)[None])
    kern = jnp.einsum("gop,jgp,gpi->jgoi", c, pows[:lc], b_bar).real
    tt = jnp.arange(lc)
    lag = tt[None, :] - tt[:, None]
    tm = jnp.where((lag >= 0)[:, :, None, None, None], kern[jnp.clip(lag, 0, lc - 1)], 0.0)
    tm = tm.transpose(2, 0, 4, 1, 3).reshape(-1, lc * S5_GROUP, lc * S5_GROUP)
    wm = pows[lc - 1 - tt][..., None] * b_bar[None]
    wm = wm.transpose(1, 0, 3, 2).reshape(-1, lc * S5_GROUP, S5_STATE)
    wm = jnp.concatenate([wm.real, wm.imag], -1)
    cv = c[None] * pows[1:lc + 1][:, :, None, :]
    cv = cv.transpose(1, 3, 0, 2).reshape(-1, S5_STATE, lc * S5_GROUP)
    vm = jnp.concatenate([cv.real, -cv.imag], 1)
    stride = (lc * (2.0 ** jnp.arange(n_steps, dtype=F32)))[None, :, None]
    ap = jnp.exp(stride * (lam * dt)[:, None, :])
    ac = jnp.concatenate([ap.real, ap.real], -1)
    bc = jnp.concatenate([-ap.imag, ap.imag], -1)
    return tm.astype(BF16), wm.astype(BF16), vm.astype(BF16), ac, bc


def _s5_scan(u, batch, seq, mats):
    tm, wm, vm, ac, bc = mats
    g = tm.shape[0]
    lc = S5_CHUNK
    nc = seq // lc
    n_steps = ac.shape[1]
    rows = batch * nc
    cols = lc * S5_GROUP
    ut = u.reshape(batch, nc, lc, g, S5_GROUP).transpose(3, 0, 1, 2, 4).reshape(g, rows, cols).astype(BF16)
    per_g = lambda i: (i, 0, 0)
    y = pl.pallas_call(
        functools.partial(_s5_body, n_chunks=nc, n_steps=n_steps),
        grid=(g,),
        in_specs=[
            pl.BlockSpec((1, rows, cols), per_g),
            pl.BlockSpec((1, cols, cols), per_g),
            pl.BlockSpec((1, cols, 2 * S5_STATE), per_g),
            pl.BlockSpec((1, 2 * S5_STATE, cols), per_g),
            pl.BlockSpec((1, n_steps, 2 * S5_STATE), per_g),
            pl.BlockSpec((1, n_steps, 2 * S5_STATE), per_g),
        ],
        out_specs=pl.BlockSpec((1, rows, cols), per_g),
        out_shape=jax.ShapeDtypeStruct((g, rows, cols), F32),
        compiler_params=_cp("parallel"),
        name="s5_scan",
    )(ut, tm, wm, vm, ac, bc)
    return y.reshape(g, batch, nc, lc, S5_GROUP).transpose(1, 2, 3, 0, 4).reshape(batch * seq, g * S5_GROUP)


def _s5_glu_body(y_ref, u_ref, d_ref, w_ref, o_ref):
    y = y_ref[...] + d_ref[...] * u_ref[...]
    y = 0.5 * y * (1.0 + jnp.tanh(math.sqrt(2.0 / math.pi) * (y + 0.044715 * (y * y * y))))
    gate = jax.nn.sigmoid(_dot(y.astype(BF16), w_ref[...]))
    o_ref[...] = (y * gate).astype(BF16)


def _s5_glu(y, u, d_skip, w_glu):
    t, n = y.shape
    tm = 512
    row = lambda i: (i, 0)
    const = lambda i: (0, 0)
    return pl.pallas_call(
        _s5_glu_body,
        grid=(t // tm,),
        in_specs=[pl.BlockSpec((tm, n), row), pl.BlockSpec((tm, n), row),
                  pl.BlockSpec((1, n), const), pl.BlockSpec((n, n), const)],
        out_specs=pl.BlockSpec((tm, n), row),
        out_shape=jax.ShapeDtypeStruct((t, n), BF16),
        compiler_params=_cp("parallel"),
        name="s5_glu",
    )(y, u, d_skip.reshape(1, n), w_glu.astype(BF16))


def _outproj_body(a1_ref, a2_ref, w1_ref, w2_ref, x_ref, mod_ref, g_ref, b_ref, rwh_ref, rwl_ref,
                  xo_ref, h_ref, lg_ref):
    y = _dot(a1_ref[...], w1_ref[...]) + _dot(a2_ref[...], w2_ref[...])
    m = mod_ref[0]
    xn = _layernorm(DN_ALPHA * x_ref[...] + (1.0 + m[2:3]) * y, g_ref[...], b_ref[...])
    xo_ref[...] = xn
    h = xn * (1.0 + m[4:5]) + m[3:4]
    h_ref[...] = h.astype(BF16)
    h_hi, h_lo = _split_bf16(h, 2)
    lg_ref[...] = _dot(h_hi, rwh_ref[...]) + (_dot(h_lo, rwh_ref[...]) + _dot(h_hi, rwl_ref[...]))


def _outproj(a1, a2, w_out, x2, mod, ln_g, ln_b, router_w, seq):
    t, d = x2.shape
    k1 = a1.shape[1]
    k2 = a2.shape[1]
    tm = 256
    per_b = seq // tm
    w1 = w_out[:k1].astype(BF16)
    w2 = w_out[k1:].astype(BF16)
    rwh, rwl = _split_bf16(router_w, 2)
    ne = router_w.shape[1]
    row = lambda i: (i, 0)
    const = lambda i: (0, 0)
    return pl.pallas_call(
        _outproj_body,
        grid=(t // tm,),
        in_specs=[
            pl.BlockSpec((tm, k1), row), pl.BlockSpec((tm, k2), row),
            pl.BlockSpec((k1, d), const), pl.BlockSpec((k2, d), const),
            pl.BlockSpec((tm, d), row),
            pl.BlockSpec((1, 6, d), lambda i: (i // per_b, 0, 0)),
            pl.BlockSpec((1, d), const), pl.BlockSpec((1, d), const),
            pl.BlockSpec((d, ne), const), pl.BlockSpec((d, ne), const),
        ],
        out_specs=[pl.BlockSpec((tm, d), row), pl.BlockSpec((tm, d), row), pl.BlockSpec((tm, ne), row)],
        out_shape=[jax.ShapeDtypeStruct((t, d), F32), jax.ShapeDtypeStruct((t, d), BF16),
                   jax.ShapeDtypeStruct((t, ne), F32)],
        compiler_params=_cp("parallel"),
        name="outproj_ln",
    )(a1, a2, w1, w2, x2, mod, ln_g.reshape(1, d), ln_b.reshape(1, d), rwh, rwl)


def _route_body(lg_ref, rb_ref, cls_ref, glo_ref, ghi_ref):
    s = jax.nn.sigmoid(lg_ref[...])
    sb = s + rb_ref[...]
    rows = [sb[e:e + 1, :] for e in range(N_EXPERTS)]
    raw = [s[e:e + 1, :] for e in range(N_EXPERTS)]
    gscore = []
    for g in range(N_EXPERT_GROUPS):
        v = rows[g * EXPERTS_PER_GROUP:(g + 1) * EXPERTS_PER_GROUP]
        best = None
        for a in range(EXPERTS_PER_GROUP):
            for b in range(a + 1, EXPERTS_PER_GROUP):
                pair = v[a] + v[b]
                best = pair if best is None else jnp.maximum(best, pair)
        gscore.append(best)
    bg = jnp.zeros_like(gscore[0], dtype=jnp.int32)
    bs = gscore[0]
    for g in range(1, N_EXPERT_GROUPS):
        upd = gscore[g] > bs
        bg = jnp.where(upd, g, bg)
        bs = jnp.where(upd, gscore[g], bs)

    def pick(vals, k):
        out = vals[k]
        for g in range(1, N_EXPERT_GROUPS):
            out = jnp.where(bg == g, vals[g * EXPERTS_PER_GROUP + k], out)
        return out

    v = [pick(rows, k) for k in range(EXPERTS_PER_GROUP)]
    r = [pick(raw, k) for k in range(EXPERTS_PER_GROUP)]
    i1 = jnp.zeros_like(bg)
    b1 = v[0]
    for k in range(1, EXPERTS_PER_GROUP):
        upd = v[k] > b1
        i1 = jnp.where(upd, k, i1)
        b1 = jnp.where(upd, v[k], b1)
    i2 = jnp.zeros_like(bg)
    b2 = jnp.full_like(b1, -jnp.inf)
    for k in range(EXPERTS_PER_GROUP):
        upd = (i1 != k) & (v[k] > b2)
        i2 = jnp.where(upd, k, i2)
        b2 = jnp.where(upd, v[k], b2)
    s1 = r[0]
    s2 = r[0]
    for k in range(1, EXPERTS_PER_GROUP):
        s1 = jnp.where(i1 == k, r[k], s1)
        s2 = jnp.where(i2 == k, r[k], s2)
    den = s1 + s2
    g1 = s1 / den
    g2 = s2 / den
    lo = jnp.minimum(i1, i2)
    hi = jnp.maximum(i1, i2)
    base = jnp.where(lo == 0, 0, jnp.where(lo == 1, 3, 5))
    cls_ref[...] = bg * N_PAIRS + base + hi - lo - 1
    first_lo = i1 < i2
    glo_ref[...] = jnp.where(first_lo, g1, g2)
    ghi_ref[...] = jnp.where(first_lo, g2, g1)


def _route(logits, router_b):
    t, ne = logits.shape
    tt = min(2048, t)
    col = lambda i: (0, i)
    cls, glo, ghi = pl.pallas_call(
        _route_body,
        grid=(t // tt,),
        in_specs=[pl.BlockSpec((ne, tt), col), pl.BlockSpec((ne, 1), lambda i: (0, 0))],
        out_specs=[pl.BlockSpec((1, tt), col)] * 3,
        out_shape=[jax.ShapeDtypeStruct((1, t), jnp.int32), jax.ShapeDtypeStruct((1, t), F32),
                   jax.ShapeDtypeStruct((1, t), F32)],
        compiler_params=_cp("parallel"),
        name="moe_route",
    )(logits.T, router_b.reshape(ne, 1).astype(F32))
    return cls[0], glo[0], ghi[0]


def _moe_body(e1_ref, e2_ref, valid_ref, h_ref, g_ref, wg1_ref, wg2_ref, wd1_ref, wd2_ref, o_ref):
    i = pl.program_id(0)

    @pl.when(valid_ref[i] == 1)
    def _():
        h = h_ref[...]
        g = g_ref[...]

        def expert(wg_ref, wd_ref):
            gu = _dot(h, wg_ref[0, 0])
            gt = gu[:, :D_EXPERT]
            act = (gt * jax.nn.sigmoid(gt) * gu[:, D_EXPERT:]).astype(BF16)
            return _dot(act, wd_ref[0, 0])

        y = g[:, 0:1] * expert(wg1_ref, wd1_ref) + g[:, 1:2] * expert(wg2_ref, wd2_ref)
        o_ref[...] = y.astype(BF16)

    @pl.when(valid_ref[i] == 0)
    def _():
        o_ref[...] = jnp.zeros_like(o_ref)


_PAIR_LO = (0, 0, 0, 1, 1, 2)
_PAIR_HI = (1, 2, 3, 2, 3, 3)


def _moe(h, cls, glo, ghi, wgu, wdn, layer):
    t, d = h.shape
    tm = MOE_TILE
    n_tiles = t // tm + N_CLASSES
    rows = n_tiles * tm
    onehot = (cls[:, None] == jnp.arange(N_CLASSES)[None, :]).astype(jnp.int32)
    csum = jnp.cumsum(onehot, 0)
    rank = jnp.take_along_axis(csum, cls[:, None], 1)[:, 0] - 1
    counts = csum[-1]
    padded = -(-counts // tm) * tm
    ends = jnp.cumsum(padded)
    dest = (ends - padded)[cls] + rank
    src = jnp.zeros((rows,), jnp.int32).at[dest].set(jnp.arange(t, dtype=jnp.int32))
    tile_start = jnp.arange(n_tiles, dtype=jnp.int32) * tm
    valid = (tile_start < ends[-1]).astype(jnp.int32)
    last_cls = jnp.max(jnp.where(counts > 0, jnp.arange(N_CLASSES), 0))
    tile_cls = jnp.minimum(jnp.searchsorted(ends, tile_start, side="right"), last_cls).astype(jnp.int32)
    grp = tile_cls // N_PAIRS
    pair = tile_cls % N_PAIRS
    e1 = (grp * EXPERTS_PER_GROUP + jnp.asarray(_PAIR_LO, jnp.int32)[pair]).astype(jnp.int32)
    e2 = (grp * EXPERTS_PER_GROUP + jnp.asarray(_PAIR_HI, jnp.int32)[pair]).astype(jnp.int32)
    hs = h.at[src].get(mode="promise_in_bounds")
    gs = jnp.stack([glo, ghi], 1).at[src].get(mode="promise_in_bounds")
    ys = pl.pallas_call(
        _moe_body,
        grid_spec=pltpu.PrefetchScalarGridSpec(
            num_scalar_prefetch=3,
            grid=(n_tiles,),
            in_specs=[
                pl.BlockSpec((tm, d), lambda i, e1, e2, va: (i, 0)),
                pl.BlockSpec((tm, 2), lambda i, e1, e2, va: (i, 0)),
                pl.BlockSpec((1, 1, d, 2 * D_EXPERT), lambda i, e1, e2, va: (layer, e1[i], 0, 0)),
                pl.BlockSpec((1, 1, d, 2 * D_EXPERT), lambda i, e1, e2, va: (layer, e2[i], 0, 0)),
                pl.BlockSpec((1, 1, D_EXPERT, d), lambda i, e1, e2, va: (layer, e1[i], 0, 0)),
                pl.BlockSpec((1, 1, D_EXPERT, d), lambda i, e1, e2, va: (layer, e2[i], 0, 0)),
            ],
            out_specs=pl.BlockSpec((tm, d), lambda i, e1, e2, va: (i, 0)),
        ),
        out_shape=jax.ShapeDtypeStruct((rows, d), BF16),
        compiler_params=_cp("arbitrary"),
        name="moe_experts",
    )(e1, e2, valid, hs, gs, wgu, wgu, wdn, wdn)
    return ys.at[dest].get(mode="promise_in_bounds")


def _resid_ln_body(x_ref, y_ref, mod_ref, g_ref, b_ref, o_ref):
    m = mod_ref[0]
    o_ref[...] = _layernorm(DN_ALPHA * x_ref[...] + (1.0 + m[5:6]) * y_ref[...].astype(F32),
                            g_ref[...], b_ref[...])


def _resid_ln(x2, y, mod, ln_g, ln_b, seq):
    t, d = x2.shape
    tm = 512
    per_b = seq // tm
    row = lambda i: (i, 0)
    const = lambda i: (0, 0)
    return pl.pallas_call(
        _resid_ln_body,
        grid=(t // tm,),
        in_specs=[pl.BlockSpec((tm, d), row), pl.BlockSpec((tm, d), row),
                  pl.BlockSpec((1, 6, d), lambda i: (i // per_b, 0, 0)),
                  pl.BlockSpec((1, d), const), pl.BlockSpec((1, d), const)],
        out_specs=pl.BlockSpec((tm, d), row),
        out_shape=jax.ShapeDtypeStruct((t, d), F32),
        compiler_params=_cp("parallel"),
        name="resid_ln",
    )(x2, y, mod, ln_g.reshape(1, d), ln_b.reshape(1, d))


def _front1_body(x_ref, mod_ref, w_ref, o_ref, h_scr):
    @pl.when(pl.program_id(1) == 0)
    def _():
        m = mod_ref[0]
        h_scr[...] = (x_ref[...] * (1.0 + m[1:2]) + m[0:1]).astype(BF16)

    o_ref[...] = _dot(h_scr[...], w_ref[...]).astype(BF16)


def _front1(x2, mod, od_w_in, seq):
    t, d = x2.shape
    tm = min(1024, seq)
    tn = 512
    per_b = seq // tm
    n_in = od_w_in.shape[1]
    n = -(-n_in // tn) * tn
    w = jnp.pad(od_w_in, ((0, 0), (0, n - n_in))).astype(BF16)
    return pl.pallas_call(
        _front1_body,
        grid=(t // tm, n // tn),
        in_specs=[pl.BlockSpec((tm, d), lambda i, j: (i, 0)),
                  pl.BlockSpec((1, 6, d), lambda i, j: (i // per_b, 0, 0)),
                  pl.BlockSpec((d, tn), lambda i, j: (0, j))],
        out_specs=pl.BlockSpec((tm, tn), lambda i, j: (i, j)),
        out_shape=jax.ShapeDtypeStruct((t, n), BF16),
        scratch_shapes=[pltpu.VMEM((tm, d), BF16)],
        compiler_params=_cp("parallel", "arbitrary"),
        name="front1",
    )(x2, mod, w)


def _dil_body(q_ref, kp_ref, kc_ref, vp_ref, vc_ref, bias_ref, o_ref, lse_ref):
    n = pl.program_id(2)
    span = DIL_SPAN
    q = q_ref[0]
    k = jnp.concatenate([kp_ref[0], kc_ref[0]], 0)
    v = jnp.concatenate([vp_ref[0], vc_ref[0]], 0)
    qi = lax.broadcasted_iota(jnp.int32, (span, 2 * span), 0)
    ku = lax.broadcasted_iota(jnp.int32, (span, 2 * span), 1)
    valid = (ku >= qi) & (ku <= qi + span) & ((n >= 1) | (ku >= span))
    lane = lax.broadcasted_iota(jnp.int32, (span, LANES), 1)
    first = lane < DIL_HEAD_DIM
    lse_all = jnp.zeros((span, LANES), F32)
    scale = DIL_HEAD_DIM ** -0.5
    for hp in range(DIL_HEADS // 2):
        sl = slice(hp * LANES, (hp + 1) * LANES)
        q2, k2, v2 = q[:, sl], k[:, sl], v[:, sl]
        outs = []
        for hh in range(2):
            head = 2 * hp + hh
            qm = jnp.where(first if hh == 0 else jnp.logical_not(first), q2, jnp.zeros_like(q2))
            s = _dot_nt(qm, k2) * scale + bias_ref[head]
            s = jnp.where(valid, s, NEG_INF)
            m = jnp.max(s, -1, keepdims=True)
            p = jnp.exp(s - m)
            l = jnp.sum(p, -1, keepdims=True)
            outs.append(_dot(p.astype(BF16), v2) / l)
            lse_all = jnp.where(lane == head, m + jnp.log(l), lse_all)
        o_ref[0, :, sl] = jnp.where(first, outs[0], outs[1]).astype(BF16)
    lse_ref[0] = lse_all


def _t5_bucket(dist):
    exact = T5_BUCKETS // 2
    logd = jnp.log(jnp.maximum(dist, 1).astype(F32) / exact) / math.log(T5_MAX_DIST / exact)
    large = jnp.minimum(exact + (logd * (T5_BUCKETS - exact)).astype(jnp.int32), T5_BUCKETS - 1)
    return jnp.where(dist < exact, dist, large)


def _dilated_group(z, rel_bias, gi, dil, batch, seq):
    span = DIL_SPAN
    n_cols = z.shape[1]
    blocks_per_row = n_cols // DIL_WIDTH
    l_stream = seq // dil
    nb = l_stream // span
    zv = z.reshape(batch, l_stream, dil * n_cols)
    qi = jnp.arange(span)[:, None]
    ku = jnp.arange(2 * span)[None, :]
    delta = jnp.clip(span + qi - ku, 0, span) * dil
    onehot = (_t5_bucket(delta)[None] == jnp.arange(T5_BUCKETS)[:, None, None]).astype(F32)
    table = rel_bias.astype(F32)[:, gi * DIL_HEADS:(gi + 1) * DIL_HEADS]
    bias = jnp.einsum("kh,kqu->hqu", table, onehot, precision=lax.Precision.HIGHEST)

    def col(j):
        return lambda b, r, n: (b, n, r * blocks_per_row + gi * 3 + j)

    def col_prev(j):
        return lambda b, r, n: (b, jnp.maximum(n - 1, 0), r * blocks_per_row + gi * 3 + j)

    blk = (1, span, DIL_WIDTH)
    o, lse = pl.pallas_call(
        _dil_body,
        grid=(batch, dil, nb),
        in_specs=[pl.BlockSpec(blk, col(0)), pl.BlockSpec(blk, col_prev(1)), pl.BlockSpec(blk, col(1)),
                  pl.BlockSpec(blk, col_prev(2)), pl.BlockSpec(blk, col(2)),
                  pl.BlockSpec((DIL_HEADS, span, 2 * span), lambda b, r, n: (0, 0, 0))],
        out_specs=[pl.BlockSpec(blk, lambda b, r, n: (b, n, r)),
                   pl.BlockSpec((1, span, LANES), lambda b, r, n: (b, n, r))],
        out_shape=[jax.ShapeDtypeStruct((batch, l_stream, dil * DIL_WIDTH), BF16),
                   jax.ShapeDtypeStruct((batch, l_stream, dil * LANES), F32)],
        compiler_params=_cp("parallel", "parallel", "arbitrary"),
        name=f"dilated_attention_{dil}",
    )(zv, zv, zv, zv, zv, bias)
    return o.reshape(batch * seq, DIL_WIDTH), lse.reshape(batch * seq, LANES)


def _dil_merge_body(o1_ref, o2_ref, o3_ref, l1_ref, l2_ref, l3_ref, e_ref, out_ref):
    lses = [l1_ref[...], l2_ref[...], l3_ref[...]]
    top = jnp.maximum(jnp.maximum(lses[0], lses[1]), lses[2])
    num = None
    den = None
    for o_ref, lse in zip((o1_ref, o2_ref, o3_ref), lses):
        w = _dot_exact_rhs(jnp.exp(lse - top), e_ref[...], 2)
        num = w * o_ref[...].astype(F32) if num is None else num + w * o_ref[...].astype(F32)
        den = w if den is None else den + w
    out_ref[...] = (num / den).astype(BF16)


def _dil_merge(outs, lses):
    t = outs[0].shape[0]
    tm = 512
    row = lambda i: (i, 0)
    head_of_lane = jnp.arange(DIL_WIDTH) // DIL_HEAD_DIM
    expand = (jnp.arange(LANES)[:, None] == head_of_lane[None, :]).astype(BF16)
    return pl.pallas_call(
        _dil_merge_body,
        grid=(t // tm,),
        in_specs=[pl.BlockSpec((tm, DIL_WIDTH), row)] * 3 + [pl.BlockSpec((tm, LANES), row)] * 3
        + [pl.BlockSpec((LANES, DIL_WIDTH), lambda i: (0, 0))],
        out_specs=pl.BlockSpec((tm, DIL_WIDTH), row),
        out_shape=jax.ShapeDtypeStruct((t, DIL_WIDTH), BF16),
        compiler_params=_cp("parallel"),
        name="dilated_merge",
    )(*outs, *lses, expand)


def _rwkv_body(zr_ref, zk_ref, zv_ref, zm_ref, mur_ref, muk_ref, muv_ref, mum_ref, w0_ref, w2_ref,
               a0_ref, a2_ref, g2_ref, kk_ref, ka_ref, rk_ref, lng_ref, lnb_ref, o_ref,
               st_ref, pr_ref, pk_ref, pv_ref, pm_ref, xr_s, xk_s, xv_s, lw_s, as_s, g_s, cum_s):
    lc = RW_CHUNK
    n_slabs = st_ref.shape[0]

    @pl.when(pl.program_id(1) == 0)
    def _():
        st_ref[...] = jnp.zeros_like(st_ref)
        pr_ref[...] = jnp.zeros_like(pr_ref)
        pk_ref[...] = jnp.zeros_like(pk_ref)
        pv_ref[...] = jnp.zeros_like(pv_ref)
        pm_ref[...] = jnp.zeros_like(pm_ref)

    def shifted(z_ref, prev_ref, mu_ref):
        z = z_ref[...].astype(F32)
        first_row = lax.broadcasted_iota(jnp.int32, z.shape, 0) == 0
        prev = jnp.where(first_row, prev_ref[...], pltpu.roll(z, 1, 0))
        prev_ref[...] = z[lc - 1:lc, :]
        return z + (prev - z) * mu_ref[...]

    xr_s[...] = shifted(zr_ref, pr_ref, mur_ref)
    xk_s[...] = shifted(zk_ref, pk_ref, muk_ref)
    xv_s[...] = shifted(zv_ref, pv_ref, muv_ref)
    xm = shifted(zm_ref, pm_ref, mum_ref)
    lora_in = xm[:, :LANES]
    wl = w0_ref[...] + _dot(jnp.tanh(lora_in).astype(BF16), w2_ref[...])
    w_log = -(jnp.maximum(-wl, 0.0) + jnp.log(1.0 + jnp.exp(-jnp.abs(wl)))) - 0.5
    lw = -jnp.exp(w_log)
    lw_s[...] = lw
    as_s[...] = jax.nn.sigmoid(a0_ref[...] + _dot(lora_in.astype(BF16), a2_ref[...]))
    g_s[...] = _dot(jax.nn.sigmoid(xm[:, LANES:]).astype(BF16), g2_ref[...])
    ti = lax.broadcasted_iota(jnp.int32, (lc, lc), 0)
    si = lax.broadcasted_iota(jnp.int32, (lc, lc), 1)
    cum_s[...] = _dot_exact_lhs((si <= ti).astype(BF16), lw, 3)

    row = lax.broadcasted_iota(jnp.int32, (2 * lc, 2 * lc), 0)
    colm = lax.broadcasted_iota(jnp.int32, (2 * lc, 2 * lc), 1)
    same_head = (row >= lc) == (colm >= lc)
    t_in = row & (lc - 1)
    s_in = colm & (lc - 1)
    strict = same_head & (s_in < t_in)
    incl = same_head & (s_in <= t_in)
    eye = row == colm
    ones_bd = same_head.astype(BF16)
    lane = lax.broadcasted_iota(jnp.int32, (lc, LANES), 1)
    first = lane < RW_HEAD

    def seg(x):
        return _dot_exact_rhs(x, ones_bd, 2)

    def stack_heads(x):
        return jnp.concatenate([jnp.where(first, x, 0.0), jnp.where(first, 0.0, x)], 0)

    slabs = range(n_slabs)
    sls = [slice(hp * LANES, (hp + 1) * LANES) for hp in slabs]
    kk_raw = [xk_s[:, sl] * kk_ref[:, sl] for sl in sls]
    kk_ss = [seg(kk * kk) for kk in kk_raw]
    lhs, rhs, bk_t, v_bs, k2s, pl_cols = [], [], [], [], [], []
    for hp, sl in enumerate(sls):
        a_sig = as_s[:, sl]
        cum = cum_s[:, sl]
        kk = kk_raw[hp] / jnp.maximum(jnp.sqrt(kk_ss[hp]), 1e-12)
        k2 = xk_s[:, sl] * (1.0 + (a_sig - 1.0) * ka_ref[:, sl])
        b_p = kk * a_sig
        cl = cum[lc - 1:lc, :]
        e_neg = jnp.exp(-cum)
        e_end = jnp.exp(cl - cum)
        at = -kk * jnp.exp(cum - lw_s[:, sl])
        rt = xr_s[:, sl] * jnp.exp(cum)
        lhs.append(jnp.concatenate([stack_heads(at), stack_heads(rt)], 0).astype(BF16))
        bt = (b_p * e_neg).astype(BF16)
        kt = (k2 * e_neg).astype(BF16)
        rhs.append(jnp.concatenate([bt, bt, kt, kt], 0))
        bk_t.append(jnp.concatenate([b_p * e_end, k2 * e_end], 0).T.astype(BF16))
        v_bs.append(xv_s[:, sl].astype(BF16))
        k2s.append(k2)
        pl_cols.append(jnp.sum(jnp.where(eye, jnp.broadcast_to(jnp.exp(cl), (2 * lc, LANES)), 0.0), 1,
                               keepdims=True))
    qms = [_dot_nt(lhs[hp], rhs[hp]) for hp in slabs]
    states = [st_ref[hp] for hp in slabs]
    arss = [_dot(lhs[hp], states[hp].astype(BF16)) for hp in slabs]
    a_abs = [jnp.where(strict, qm[:2 * lc, :2 * lc], 0.0).astype(BF16) for qm in qms]
    xs = [jnp.where(eye, 1.0, 0.0) + a.astype(F32) for a in a_abs]
    ps = [_dot(a, a).astype(BF16) for a in a_abs]
    n_double = int(math.log2(lc)) - 1
    for it in range(n_double - 1):
        res = [_dot(ps[hp], jnp.concatenate([xs[hp].astype(BF16), ps[hp]], 1)) for hp in slabs]
        xs = [xs[hp] + res[hp][:, :2 * lc] for hp in slabs]
        ps = [res[hp][:, 2 * lc:].astype(BF16) for hp in slabs]
    xs = [xs[hp] + _dot(ps[hp], xs[hp].astype(BF16)) for hp in slabs]
    vss = [jnp.concatenate([v_b, v_b], 0) for v_b in v_bs]
    gmats = [_dot(jnp.where(strict, qms[hp][:2 * lc, 2 * lc:], 0.0).astype(BF16), vss[hp]) + arss[hp][:2 * lc]
             for hp in slabs]
    uss = [_dot(xs[hp].astype(BF16), gmats[hp].astype(BF16)) for hp in slabs]
    incl2 = jnp.concatenate([incl, incl], 1)
    yss = [_dot(jnp.where(incl2, qms[hp][2 * lc:, :], 0.0).astype(BF16),
                jnp.concatenate([uss[hp].astype(BF16), vss[hp]], 0)) + arss[hp][2 * lc:] for hp in slabs]
    us_p = [jnp.where(first, us[:lc], us[lc:]) for us in uss]
    ys_p = [jnp.where(first, ys[:lc], ys[lc:]) for ys in yss]
    upds = [_dot(bk_t[hp], jnp.concatenate([us_p[hp].astype(BF16), v_bs[hp]], 0)) for hp in slabs]
    for hp in slabs:
        st_ref[hp] = pl_cols[hp] * states[hp] + jnp.where(same_head, upds[hp], 0.0)
    means = [seg(y) * (1.0 / RW_HEAD) for y in ys_p]
    dys = [ys_p[hp] - means[hp] for hp in slabs]
    variances = [seg(dy * dy) * (1.0 / RW_HEAD) for dy in dys]
    bonus = [seg(xr_s[:, sl] * k2s[hp] * rk_ref[:, sl]) for hp, sl in enumerate(sls)]
    for hp, sl in enumerate(sls):
        yn = dys[hp] * lax.rsqrt(variances[hp] + RW_GN_EPS) * lng_ref[:, sl] + lnb_ref[:, sl]
        o_ref[:, sl] = ((yn + bonus[hp] * xv_s[:, sl]) * g_s[:, sl]).astype(BF16)


def _rwkv(z, batch, seq, mu, w0, w2, a0, a2, g2, k_k, k_a, r_k, lnx_g, lnx_b):
    t = z.shape[0]
    lc = RW_CHUNK
    nc = seq // lc
    width = w0.shape[0]
    n_slabs = width // LANES
    col0 = DIL_IN // width
    misc_blk = (DIL_IN + 3 * width) // RW_MISC
    n_lora = RW_LORA_W + RW_LORA_A + RW_LORA_G
    mu_m = jnp.pad(mu[3 * width:], (0, RW_MISC - n_lora)).reshape(1, RW_MISC)
    w2p = jnp.pad(w2, ((0, LANES - RW_LORA_W), (0, 0))).astype(BF16)
    a2p = jnp.pad(a2, ((RW_LORA_W, LANES - RW_LORA_W - RW_LORA_A), (0, 0))).astype(BF16)
    g2p = jnp.pad(g2, ((0, RW_MISC - LANES - RW_LORA_G), (0, 0))).astype(BF16)
    vec = lambda a: a.reshape(1, width).astype(F32)
    const = lambda b, c: (0, 0)
    wide = lambda: pl.BlockSpec((1, width), const)
    return pl.pallas_call(
        _rwkv_body,
        grid=(batch, nc),
        in_specs=[
            pl.BlockSpec((lc, width), lambda b, c: (b * nc + c, col0)),
            pl.BlockSpec((lc, width), lambda b, c: (b * nc + c, col0 + 1)),
            pl.BlockSpec((lc, width), lambda b, c: (b * nc + c, col0 + 2)),
            pl.BlockSpec((lc, RW_MISC), lambda b, c: (b * nc + c, misc_blk)),
            wide(), wide(), wide(), pl.BlockSpec((1, RW_MISC), const),
            wide(), pl.BlockSpec((LANES, width), const),
            wide(), pl.BlockSpec((LANES, width), const),
            pl.BlockSpec((RW_MISC - LANES, width), const),
            wide(), wide(), wide(), wide(), wide(),
        ],
        out_specs=pl.BlockSpec((lc, width), lambda b, c: (b * nc + c, 0)),
        out_shape=jax.ShapeDtypeStruct((t, width), BF16),
        scratch_shapes=[pltpu.VMEM((n_slabs, LANES, LANES), F32),
                        pltpu.VMEM((1, width), F32), pltpu.VMEM((1, width), F32),
                        pltpu.VMEM((1, width), F32), pltpu.VMEM((1, RW_MISC), F32)]
        + [pltpu.VMEM((lc, width), F32)] * 7,
        compiler_params=_cp("parallel", "arbitrary"),
        name="rwkv7",
    )(z, z, z, z, vec(mu[:width]), vec(mu[width:2 * width]), vec(mu[2 * width:3 * width]), mu_m,
      vec(w0), w2p, vec(a0), a2p, g2p, vec(k_k), vec(k_a), vec(r_k), vec(lnx_g), vec(lnx_b))


def kernel(x, c, ada_w, ada_b, ln_mix_g, ln_mix_b, ln_ffn_g, ln_ffn_b, router_w, router_b, moe_w_gate_up, moe_w_down, rel_bias, ev_w_in, mla_q_norm, mla_w_uq, mla_kv_norm, mla_w_ukv, s5_lambda_re, s5_lambda_im, s5_b_re, s5_b_im, s5_c_re, s5_c_im, s5_d, s5_log_dt, s5_w_glu, ev_w_out, od_w_in, rw_mu, rw_w0, rw_w2, rw_a0, rw_a2, rw_g2, rw_k_k, rw_k_a, rw_r_k, rw_lnx_g, rw_lnx_b, od_w_out):
    batch, seq, d = x.shape
    assert seq % 2048 == 0 and d % LANES == 0
    x2 = x.reshape(batch * seq, d)
    mods = _ada(c, ada_w, ada_b)
    wgu_all = moe_w_gate_up.astype(BF16)
    wdn_all = moe_w_down.astype(BF16)
    for layer in range(DEPTH):
        mod = mods[layer]
        if layer % 2 == 0:
            e = layer // 2
            q, k, v, u = _front0(x2, mod, seq, ev_w_in[e], mla_q_norm[e], mla_w_uq[e], mla_kv_norm[e],
                                 mla_w_ukv[e])
            att = _flash(q, k, v, batch, seq)
            n_steps = max(1, math.ceil(math.log2(seq // S5_CHUNK)))
            mats = _s5_matrices(s5_lambda_re[e], s5_lambda_im[e], s5_b_re[e], s5_b_im[e], s5_c_re[e],
                                s5_c_im[e], s5_log_dt[e], n_steps)
            ssm = _s5_glu(_s5_scan(u, batch, seq, mats), u, s5_d[e], s5_w_glu[e])
            x2, h, logits = _outproj(att, ssm, ev_w_out[e], x2, mod, ln_mix_g[layer], ln_mix_b[layer],
                                     router_w, seq)
        else:
            o = layer // 2
            z = _front1(x2, mod, od_w_in[o], seq)
            outs, lses = [], []
            for gi, (win, dil) in enumerate(DIL_PATTERNS):
                assert win // dil == DIL_SPAN
                og, lg = _dilated_group(z, rel_bias, gi, dil, batch, seq)
                outs.append(og)
                lses.append(lg)
            att = _dil_merge(outs, lses)
            tm_out = _rwkv(z, batch, seq, rw_mu[o], rw_w0[o], rw_w2[o], rw_a0[o], rw_a2[o], rw_g2[o],
                           rw_k_k[o], rw_k_a[o], rw_r_k[o], rw_lnx_g[o], rw_lnx_b[o])
            x2, h, logits = _outproj(att, tm_out, od_w_out[o], x2, mod, ln_mix_g[layer], ln_mix_b[layer],
                                     router_w, seq)
        cls, glo, ghi = _route(logits, router_b)
        y = _moe(h, cls, glo, ghi, wgu_all, wdn_all, layer)
        x2 = _resid_ln(x2, y, mod, ln_ffn_g[layer], ln_ffn_b[layer], seq)
    return x2.reshape(batch, seq, d)
```

```python
import functools
import math

import jax
import jax.numpy as jnp
from jax import lax
from jax.experimental import pallas as pl
from jax.experimental.pallas import tpu as pltpu

F32 = jnp.float32
BF16 = jnp.bfloat16

DEPTH = 2
DN_ALPHA = (2.0 * DEPTH) ** 0.25
LN_EPS = 1e-5
RMS_EPS = 1e-6
NEG_INF = -1e30

MLA_HEADS = 8
MLA_NOPE = 128
MLA_ROPE = 64
MLA_V = 128
MLA_QK = MLA_NOPE + MLA_ROPE
MLA_Q_RANK = 512
MLA_KV_RANK = 256
ROPE_THETA = 10000.0
MLA_HEAD_PAD = 256

S5_GROUP = 16
S5_STATE = 64
S5_CHUNK = 16

DIL_PATTERNS = ((128, 1), (512, 4), (2048, 16))
DIL_SPAN = 128
DIL_HEADS = 8
DIL_HEAD_DIM = 64
DIL_WIDTH = DIL_HEADS * DIL_HEAD_DIM
DIL_IN = len(DIL_PATTERNS) * 3 * DIL_WIDTH
T5_BUCKETS = 32
T5_MAX_DIST = 2048

RW_HEAD = 64
RW_LORA_W = 64
RW_LORA_A = 64
RW_LORA_G = 224
RW_GN_EPS = 64e-5
RW_CHUNK = 64
RW_MISC = 512

N_EXPERTS = 16
N_EXPERT_GROUPS = 4
EXPERTS_PER_GROUP = 4
D_EXPERT = 512
N_PAIRS = 6
N_CLASSES = N_EXPERT_GROUPS * N_PAIRS
MOE_TILE = 256

LANES = 128
VMEM_LIMIT = 56 * 1024 * 1024


def _cp(*sem):
    return pltpu.CompilerParams(dimension_semantics=sem, vmem_limit_bytes=VMEM_LIMIT)


def _dot(a, b):
    return jnp.dot(a, b, preferred_element_type=F32)


def _dot_nt(a, b):
    return lax.dot_general(a, b, (((1,), (1,)), ((), ())), preferred_element_type=F32)


def _split_bf16(x, parts):
    out = []
    for _ in range(parts):
        hi = x.astype(BF16)
        out.append(hi)
        x = x - hi.astype(F32)
    return out


def _dot_exact_rhs(x, w_bf16, parts=3):
    acc = None
    for p in _split_bf16(x, parts):
        t = _dot(p, w_bf16)
        acc = t if acc is None else acc + t
    return acc


def _dot_exact_lhs(w_bf16, x, parts=3):
    acc = None
    for p in _split_bf16(x, parts):
        t = _dot(w_bf16, p)
        acc = t if acc is None else acc + t
    return acc


def _layernorm(x, g, b):
    mu = jnp.mean(x, -1, keepdims=True)
    d = x - mu
    var = jnp.mean(d * d, -1, keepdims=True)
    return d * lax.rsqrt(var + LN_EPS) * g + b


def _cast_body(x_ref, o_ref):
    o_ref[...] = x_ref[...].astype(BF16)


def _cast_bf16(w, block_rows):
    w2 = w.reshape(-1, w.shape[-1])
    rows, cols = w2.shape
    spec = pl.BlockSpec((block_rows, cols), lambda i: (i, 0))
    out = pl.pallas_call(
        _cast_body, grid=(rows // block_rows,), in_specs=[spec], out_specs=spec,
        out_shape=jax.ShapeDtypeStruct((rows, cols), BF16), compiler_params=_cp("parallel"), name="cast_bf16",
    )(w2)
    return out.reshape(w.shape)


def _ada_body(c_ref, w_ref, b_ref, o_ref):
    c = c_ref[...]
    cond = c * jax.nn.sigmoid(c)
    o_ref[0] = _dot_exact_rhs(cond, w_ref[0].astype(BF16), 2) + b_ref[0]


def _ada(c, ada_w, ada_b):
    depth, d, n = ada_w.shape
    b = c.shape[0]
    rows = -(-b // 8) * 8
    cp = jnp.pad(c, ((0, rows - b), (0, 0)))
    tn = 1024
    out = pl.pallas_call(
        _ada_body,
        grid=(depth, n // tn),
        in_specs=[
            pl.BlockSpec((rows, d), lambda l, j: (0, 0)),
            pl.BlockSpec((1, d, tn), lambda l, j: (l, 0, j)),
            pl.BlockSpec((1, 1, tn), lambda l, j: (l, 0, j)),
        ],
        out_specs=pl.BlockSpec((1, rows, tn), lambda l, j: (l, 0, j)),
        out_shape=jax.ShapeDtypeStruct((depth, rows, n), F32),
        compiler_params=_cp("parallel", "parallel"),
        name="ada",
    )(cp, ada_w, ada_b.reshape(depth, 1, n))
    return out[:, :b].reshape(depth, b, 6, d)


def _front0_body(x_ref, mod_ref, cs_ref, win_ref, qn_ref, kvn_ref, wq1_ref, wq2_ref, wkn_ref,
                 wv_ref, q_ref, k_ref, v_ref, u_ref):
    m = mod_ref[0]
    h = (x_ref[...] * (1.0 + m[1:2]) + m[0:1]).astype(BF16)
    z = _dot(h, win_ref[...])
    cs = cs_ref[...]
    c2 = cs[:, :LANES]
    s2 = cs[:, LANES:]
    q_c = z[:, :MLA_Q_RANK]
    kv_c = z[:, MLA_Q_RANK:MLA_Q_RANK + MLA_KV_RANK]
    o = MLA_Q_RANK + MLA_KV_RANK
    krope = (z[:, o:o + LANES] * c2 + z[:, o + LANES:o + 2 * LANES] * s2).astype(BF16)
    u_ref[...] = z[:, o + 2 * LANES:]
    hq = (q_c * lax.rsqrt(jnp.mean(q_c * q_c, -1, keepdims=True) + RMS_EPS) * qn_ref[...]).astype(BF16)
    hkv = (kv_c * lax.rsqrt(jnp.mean(kv_c * kv_c, -1, keepdims=True) + RMS_EPS) * kvn_ref[...]).astype(BF16)
    full = _dot(hq, wq1_ref[...])
    sw = _dot(hq, wq2_ref[...])
    kn = _dot(hkv, wkn_ref[...])
    v_ref[...] = _dot(hkv, wv_ref[...]).astype(BF16)
    for j in range(MLA_HEADS):
        a = j * MLA_HEAD_PAD
        q_ref[:, a:a + LANES] = full[:, a:a + LANES].astype(BF16)
        q_ref[:, a + LANES:a + 2 * LANES] = (
            full[:, a + LANES:a + 2 * LANES] * c2 + sw[:, j * LANES:(j + 1) * LANES] * s2).astype(BF16)
        k_ref[:, a:a + LANES] = kn[:, j * LANES:(j + 1) * LANES].astype(BF16)
        k_ref[:, a + LANES:a + 2 * LANES] = krope


def _rope_swapped(w):
    half = w.shape[-1] // 2
    return jnp.concatenate([-w[..., half:], w[..., :half]], -1)


def _front0(x2, mod, seq, ev_w_in, q_norm, w_uq, kv_norm, w_ukv):
    t, d = x2.shape
    tm = 256
    per_b = seq // tm
    zpad = jnp.zeros((d, LANES - MLA_ROPE), F32)
    o = MLA_Q_RANK + MLA_KV_RANK
    w_kr = ev_w_in[:, o:o + MLA_ROPE]
    win = jnp.concatenate([ev_w_in[:, :o], w_kr, zpad, _rope_swapped(w_kr), zpad,
                           ev_w_in[:, o + MLA_ROPE:]], 1).astype(BF16)
    scale = MLA_QK ** -0.5
    wq = w_uq.reshape(MLA_Q_RANK, MLA_HEADS, MLA_QK) * scale
    zq = jnp.zeros((MLA_Q_RANK, MLA_HEADS, MLA_HEAD_PAD - MLA_QK), F32)
    wq1 = jnp.concatenate([wq, zq], -1).reshape(MLA_Q_RANK, -1).astype(BF16)
    wq2 = jnp.concatenate([_rope_swapped(wq[..., MLA_NOPE:]), zq], -1).reshape(MLA_Q_RANK, -1).astype(BF16)
    wkv = w_ukv.reshape(MLA_KV_RANK, MLA_HEADS, MLA_NOPE + MLA_V)
    wkn = wkv[..., :MLA_NOPE].reshape(MLA_KV_RANK, -1).astype(BF16)
    wv = wkv[..., MLA_NOPE:].reshape(MLA_KV_RANK, -1).astype(BF16)
    inv = ROPE_THETA ** (-jnp.arange(0, MLA_ROPE, 2, dtype=F32) / MLA_ROPE)
    ang = jnp.arange(seq, dtype=F32)[:, None] * inv[None]
    zt = jnp.zeros((seq, LANES - MLA_ROPE), F32)
    cs = jnp.concatenate([jnp.cos(ang), jnp.cos(ang), zt, jnp.sin(ang), jnp.sin(ang), zt], 1)
    const = lambda i: (0, 0)
    hq_w = MLA_HEADS * MLA_HEAD_PAD
    hv_w = MLA_HEADS * MLA_V
    s5w = ev_w_in.shape[1] - o - MLA_ROPE
    return pl.pallas_call(
        _front0_body,
        grid=(t // tm,),
        in_specs=[
            pl.BlockSpec((tm, d), lambda i: (i, 0)),
            pl.BlockSpec((1, 6, d), lambda i: (i // per_b, 0, 0)),
            pl.BlockSpec((tm, 2 * LANES), lambda i: (i % per_b, 0)),
            pl.BlockSpec(win.shape, const),
            pl.BlockSpec((1, MLA_Q_RANK), const),
            pl.BlockSpec((1, MLA_KV_RANK), const),
            pl.BlockSpec(wq1.shape, const),
            pl.BlockSpec(wq2.shape, const),
            pl.BlockSpec(wkn.shape, const),
            pl.BlockSpec(wv.shape, const),
        ],
        out_specs=[
            pl.BlockSpec((tm, hq_w), lambda i: (i, 0)),
            pl.BlockSpec((tm, hq_w), lambda i: (i, 0)),
            pl.BlockSpec((tm, hv_w), lambda i: (i, 0)),
            pl.BlockSpec((tm, s5w), lambda i: (i, 0)),
        ],
        out_shape=[
            jax.ShapeDtypeStruct((t, hq_w), BF16),
            jax.ShapeDtypeStruct((t, hq_w), BF16),
            jax.ShapeDtypeStruct((t, hv_w), BF16),
            jax.ShapeDtypeStruct((t, s5w), F32),
        ],
        compiler_params=_cp("parallel"),
        name="front0",
    )(x2, mod, cs, win, q_norm.reshape(1, -1), kv_norm.reshape(1, -1), wq1, wq2, wkn, wv)


def _flash_body(q_ref, k_ref, v_ref, o_ref, *, tile):
    i = pl.program_id(2)
    q = q_ref[...]

    def step(j, carry, diagonal):
        m, l, acc = carry
        start = pl.multiple_of(j * tile, tile)
        k = k_ref[pl.ds(start, tile), :]
        v = v_ref[pl.ds(start, tile), :]
        s = _dot_nt(q, k)
        if diagonal:
            row = lax.broadcasted_iota(jnp.int32, s.shape, 0)
            col = lax.broadcasted_iota(jnp.int32, s.shape, 1)
            s = jnp.where(col <= row, s, NEG_INF)
        m_new = jnp.maximum(m, jnp.max(s, -1, keepdims=True))
        alpha = jnp.exp(m - m_new)
        p = jnp.exp(s - m_new)
        l = l * alpha + jnp.sum(p, -1, keepdims=True)
        acc = acc * alpha + _dot(p.astype(BF16), v)
        return m_new, l, acc

    init = (jnp.full((tile, 1), NEG_INF, F32), jnp.zeros((tile, 1), F32),
            jnp.zeros((tile, MLA_V), F32))
    carry = lax.fori_loop(0, i, lambda j, c: step(j, c, False), init)
    _, l, acc = step(i, carry, True)
    o_ref[...] = (acc / l).astype(BF16)


def _flash(q, k, v, batch, seq):
    t = q.shape[0]
    tile = 512
    nq = seq // tile
    return pl.pallas_call(
        functools.partial(_flash_body, tile=tile),
        grid=(batch, MLA_HEADS, nq),
        in_specs=[
            pl.BlockSpec((tile, MLA_HEAD_PAD), lambda b, h, i: (b * nq + i, h)),
            pl.BlockSpec((seq, MLA_HEAD_PAD), lambda b, h, i: (b, h)),
            pl.BlockSpec((seq, MLA_V), lambda b, h, i: (b, h)),
        ],
        out_specs=pl.BlockSpec((tile, MLA_V), lambda b, h, i: (b * nq + i, h)),
        out_shape=jax.ShapeDtypeStruct((t, MLA_HEADS * MLA_V), BF16),
        compiler_params=_cp("parallel", "parallel", "arbitrary"),
        name="mla_attention",
    )(q, k, v)


def _s5_body(u_ref, t_ref, w_ref, v_ref, ac_ref, bc_ref, y_ref, *, n_chunks, n_steps):
    u = u_ref[0]
    z = _dot(u, w_ref[0])
    cidx = lax.rem(lax.broadcasted_iota(jnp.int32, z.shape, 0), n_chunks)
    half = S5_STATE
    for step in range(n_steps):
        d = 1 << step
        zs = jnp.where(cidx >= d, pltpu.roll(z, d, 0), 0.0)
        z = z + zs * ac_ref[0, step:step + 1, :] + pltpu.roll(zs, half, 1) * bc_ref[0, step:step + 1, :]
    xprev = jnp.where(cidx >= 1, pltpu.roll(z, 1, 0), 0.0)
    y_ref[0] = _dot(u, t_ref[0]) + _dot(xprev.astype(BF16), v_ref[0])


def _s5_matrices(lam_re, lam_im, b_re, b_im, c_re, c_im, log_dt, n_steps):
    lc = S5_CHUNK
    lam = lax.complex(lam_re.astype(F32), lam_im.astype(F32))
    dt = jnp.exp(log_dt.astype(F32))[:, None]
    lam_bar = jnp.exp(lam * dt)
    b_bar = ((lam_bar - 1.0) / lam)[..., None] * lax.complex(b_re.astype(F32), b_im.astype(F32))
    c = lax.complex(c_re.astype(F32), c_im.astype(F32))
    j = jnp.arange(lc + 1, dtype=F32)[:, None, None]
    pows = jnp.exp(j * (lam * dt)[None])
    kern = jnp.einsum("gop,jgp,gpi->jgoi", c, pows[:lc], b_bar).real
    tt = jnp.arange(lc)
    lag = tt[None, :] - tt[:, None]
    tm = jnp.where((lag >= 0)[:, :, None, None, None], kern[jnp.clip(lag, 0, lc - 1)], 0.0)
    tm = tm.transpose(2, 0, 4, 1, 3).reshape(-1, lc * S5_GROUP, lc * S5_GROUP)
    wm = pows[lc - 1 - tt][..., None] * b_bar[None]
    wm = wm.transpose(1, 0, 3, 2).reshape(-1, lc * S5_GROUP, S5_STATE)
    wm = jnp.concatenate([wm.real, wm.imag], -1)
    cv = c[None] * pows[1:lc + 1][:, :, None, :]
    cv = cv.transpose(1, 3, 0, 2).reshape(-1, S5_STATE, lc * S5_GROUP)
    vm = jnp.concatenate([cv.real, -cv.imag], 1)
    stride = (lc * (2.0 ** jnp.arange(n_steps, dtype=F32)))[None, :, None]
    ap = jnp.exp(stride * (lam * dt)[:, None, :])
    ac = jnp.concatenate([ap.real, ap.real], -1)
    bc = jnp.concatenate([-ap.imag, ap.imag], -1)
    return tm.astype(BF16), wm.astype(BF16), vm.astype(BF16), ac, bc


def _s5_scan(u, batch, seq, mats):
    tm, wm, vm, ac, bc = mats
    g = tm.shape[0]
    lc = S5_CHUNK
    nc = seq // lc
    n_steps = ac.shape[1]
    rows = batch * nc
    cols = lc * S5_GROUP
    ut = u.reshape(batch, nc, lc, g, S5_GROUP).transpose(3, 0, 1, 2, 4).reshape(g, rows, cols).astype(BF16)
    per_g = lambda i: (i, 0, 0)
    y = pl.pallas_call(
        functools.partial(_s5_body, n_chunks=nc, n_steps=n_steps),
        grid=(g,),
        in_specs=[
            pl.BlockSpec((1, rows, cols), per_g),
            pl.BlockSpec((1, cols, cols), per_g),
            pl.BlockSpec((1, cols, 2 * S5_STATE), per_g),
            pl.BlockSpec((1, 2 * S5_STATE, cols), per_g),
            pl.BlockSpec((1, n_steps, 2 * S5_STATE), per_g),
            pl.BlockSpec((1, n_steps, 2 * S5_STATE), per_g),
        ],
        out_specs=pl.BlockSpec((1, rows, cols), per_g),
        out_shape=jax.ShapeDtypeStruct((g, rows, cols), F32),
        compiler_params=_cp("parallel"),
        name="s5_scan",
    )(ut, tm, wm, vm, ac, bc)
    return y.reshape(g, batch, nc, lc, S5_GROUP).transpose(1, 2, 3, 0, 4).reshape(batch * seq, g * S5_GROUP)


def _s5_glu_body(y_ref, u_ref, d_ref, w_ref, o_ref):
    y = y_ref[...] + d_ref[...] * u_ref[...]
    y = 0.5 * y * (1.0 + jnp.tanh(math.sqrt(2.0 / math.pi) * (y + 0.044715 * (y * y * y))))
    gate = jax.nn.sigmoid(_dot(y.astype(BF16), w_ref[...]))
    o_ref[...] = (y * gate).astype(BF16)


def _s5_glu(y, u, d_skip, w_glu):
    t, n = y.shape
    tm = 512
    row = lambda i: (i, 0)
    const = lambda i: (0, 0)
    return pl.pallas_call(
        _s5_glu_body,
        grid=(t // tm,),
        in_specs=[pl.BlockSpec((tm, n), row), pl.BlockSpec((tm, n), row),
                  pl.BlockSpec((1, n), const), pl.BlockSpec((n, n), const)],
        out_specs=pl.BlockSpec((tm, n), row),
        out_shape=jax.ShapeDtypeStruct((t, n), BF16),
        compiler_params=_cp("parallel"),
        name="s5_glu",
    )(y, u, d_skip.reshape(1, n), w_glu.astype(BF16))


def _outproj_body(a1_ref, a2_ref, w1_ref, w2_ref, x_ref, mod_ref, g_ref, b_ref, rwh_ref, rwl_ref,
                  xo_ref, h_ref, lg_ref):
    y = _dot(a1_ref[...], w1_ref[...]) + _dot(a2_ref[...], w2_ref[...])
    m = mod_ref[0]
    xn = _layernorm(DN_ALPHA * x_ref[...] + (1.0 + m[2:3]) * y, g_ref[...], b_ref[...])
    xo_ref[...] = xn
    h = xn * (1.0 + m[4:5]) + m[3:4]
    h_ref[...] = h.astype(BF16)
    h_hi, h_lo = _split_bf16(h, 2)
    lg_ref[...] = _dot(h_hi, rwh_ref[...]) + (_dot(h_lo, rwh_ref[...]) + _dot(h_hi, rwl_ref[...]))


def _outproj(a1, a2, w_out, x2, mod, ln_g, ln_b, router_w, seq):
    t, d = x2.shape
    k1 = a1.shape[1]
    k2 = a2.shape[1]
    tm = 256
    per_b = seq // tm
    w1 = w_out[:k1].astype(BF16)
    w2 = w_out[k1:].astype(BF16)
    rwh, rwl = _split_bf16(router_w, 2)
    ne = router_w.shape[1]
    row = lambda i: (i, 0)
    const = lambda i: (0, 0)
    return pl.pallas_call(
        _outproj_body,
        grid=(t // tm,),
        in_specs=[
            pl.BlockSpec((tm, k1), row), pl.BlockSpec((tm, k2), row),
            pl.BlockSpec((k1, d), const), pl.BlockSpec((k2, d), const),
            pl.BlockSpec((tm, d), row),
            pl.BlockSpec((1, 6, d), lambda i: (i // per_b, 0, 0)),
            pl.BlockSpec((1, d), const), pl.BlockSpec((1, d), const),
            pl.BlockSpec((d, ne), const), pl.BlockSpec((d, ne), const),
        ],
        out_specs=[pl.BlockSpec((tm, d), row), pl.BlockSpec((tm, d), row), pl.BlockSpec((tm, ne), row)],
        out_shape=[jax.ShapeDtypeStruct((t, d), F32), jax.ShapeDtypeStruct((t, d), BF16),
                   jax.ShapeDtypeStruct((t, ne), F32)],
        compiler_params=_cp("parallel"),
        name="outproj_ln",
    )(a1, a2, w1, w2, x2, mod, ln_g.reshape(1, d), ln_b.reshape(1, d), rwh, rwl)


def _route_body(lg_ref, rb_ref, cls_ref, glo_ref, ghi_ref):
    s = jax.nn.sigmoid(lg_ref[...])
    sb = s + rb_ref[...]
    rows = [sb[e:e + 1, :] for e in range(N_EXPERTS)]
    raw = [s[e:e + 1, :] for e in range(N_EXPERTS)]
    gscore = []
    for g in range(N_EXPERT_GROUPS):
        v = rows[g * EXPERTS_PER_GROUP:(g + 1) * EXPERTS_PER_GROUP]
        best = None
        for a in range(EXPERTS_PER_GROUP):
            for b in range(a + 1, EXPERTS_PER_GROUP):
                pair = v[a] + v[b]
                best = pair if best is None else jnp.maximum(best, pair)
        gscore.append(best)
    bg = jnp.zeros_like(gscore[0], dtype=jnp.int32)
    bs = gscore[0]
    for g in range(1, N_EXPERT_GROUPS):
        upd = gscore[g] > bs
        bg = jnp.where(upd, g, bg)
        bs = jnp.where(upd, gscore[g], bs)

    def pick(vals, k):
        out = vals[k]
        for g in range(1, N_EXPERT_GROUPS):
            out = jnp.where(bg == g, vals[g * EXPERTS_PER_GROUP + k], out)
        return out

    v = [pick(rows, k) for k in range(EXPERTS_PER_GROUP)]
    r = [pick(raw, k) for k in range(EXPERTS_PER_GROUP)]
    i1 = jnp.zeros_like(bg)
    b1 = v[0]
    for k in range(1, EXPERTS_PER_GROUP):
        upd = v[k] > b1
        i1 = jnp.where(upd, k, i1)
        b1 = jnp.where(upd, v[k], b1)
    i2 = jnp.zeros_like(bg)
    b2 = jnp.full_like(b1, -jnp.inf)
    for k in range(EXPERTS_PER_GROUP):
        upd = (i1 != k) & (v[k] > b2)
        i2 = jnp.where(upd, k, i2)
        b2 = jnp.where(upd, v[k], b2)
    s1 = r[0]
    s2 = r[0]
    for k in range(1, EXPERTS_PER_GROUP):
        s1 = jnp.where(i1 == k, r[k], s1)
        s2 = jnp.where(i2 == k, r[k], s2)
    den = s1 + s2
    g1 = s1 / den
    g2 = s2 / den
    lo = jnp.minimum(i1, i2)
    hi = jnp.maximum(i1, i2)
    base = jnp.where(lo == 0, 0, jnp.where(lo == 1, 3, 5))
    cls_ref[...] = bg * N_PAIRS + base + hi - lo - 1
    first_lo = i1 < i2
    glo_ref[...] = jnp.where(first_lo, g1, g2)
    ghi_ref[...] = jnp.where(first_lo, g2, g1)


def _route(logits, router_b):
    t, ne = logits.shape
    tt = min(2048, t)
    col = lambda i: (0, i)
    cls, glo, ghi = pl.pallas_call(
        _route_body,
        grid=(t // tt,),
        in_specs=[pl.BlockSpec((ne, tt), col), pl.BlockSpec((ne, 1), lambda i: (0, 0))],
        out_specs=[pl.BlockSpec((1, tt), col)] * 3,
        out_shape=[jax.ShapeDtypeStruct((1, t), jnp.int32), jax.ShapeDtypeStruct((1, t), F32),
                   jax.ShapeDtypeStruct((1, t), F32)],
        compiler_params=_cp("parallel"),
        name="moe_route",
    )(logits.T, router_b.reshape(ne, 1).astype(F32))
    return cls[0], glo[0], ghi[0]


def _moe_body(e1_ref, e2_ref, valid_ref, h_ref, g_ref, wg1_ref, wg2_ref, wd1_ref, wd2_ref, o_ref):
    i = pl.program_id(0)

    @pl.when(valid_ref[i] == 1)
    def _():
        h = h_ref[...]
        g = g_ref[...]

        def expert(wg_ref, wd_ref):
            gu = _dot(h, wg_ref[0, 0])
            gt = gu[:, :D_EXPERT]
            act = (gt * jax.nn.sigmoid(gt) * gu[:, D_EXPERT:]).astype(BF16)
            return _dot(act, wd_ref[0, 0])

        y = g[:, 0:1] * expert(wg1_ref, wd1_ref) + g[:, 1:2] * expert(wg2_ref, wd2_ref)
        o_ref[...] = y.astype(BF16)

    @pl.when(valid_ref[i] == 0)
    def _():
        o_ref[...] = jnp.zeros_like(o_ref)


_PAIR_LO = (0, 0, 0, 1, 1, 2)
_PAIR_HI = (1, 2, 3, 2, 3, 3)


def _moe(h, cls, glo, ghi, wgu, wdn, layer):
    t, d = h.shape
    tm = MOE_TILE
    n_tiles = t // tm + N_CLASSES
    rows = n_tiles * tm
    onehot = (cls[:, None] == jnp.arange(N_CLASSES)[None, :]).astype(jnp.int32)
    csum = jnp.cumsum(onehot, 0)
    rank = jnp.take_along_axis(csum, cls[:, None], 1)[:, 0] - 1
    counts = csum[-1]
    padded = -(-counts // tm) * tm
    ends = jnp.cumsum(padded)
    dest = (ends - padded)[cls] + rank
    src = jnp.zeros((rows,), jnp.int32).at[dest].set(jnp.arange(t, dtype=jnp.int32))
    tile_start = jnp.arange(n_tiles, dtype=jnp.int32) * tm
    valid = (tile_start < ends[-1]).astype(jnp.int32)
    last_cls = jnp.max(jnp.where(counts > 0, jnp.arange(N_CLASSES), 0))
    tile_cls = jnp.minimum(jnp.searchsorted(ends, tile_start, side="right"), last_cls).astype(jnp.int32)
    grp = tile_cls // N_PAIRS
    pair = tile_cls % N_PAIRS
    e1 = (grp * EXPERTS_PER_GROUP + jnp.asarray(_PAIR_LO, jnp.int32)[pair]).astype(jnp.int32)
    e2 = (grp * EXPERTS_PER_GROUP + jnp.asarray(_PAIR_HI, jnp.int32)[pair]).astype(jnp.int32)
    hs = h.at[src].get(mode="promise_in_bounds")
    gs = jnp.stack([glo, ghi], 1).at[src].get(mode="promise_in_bounds")
    ys = pl.pallas_call(
        _moe_body,
        grid_spec=pltpu.PrefetchScalarGridSpec(
            num_scalar_prefetch=3,
            grid=(n_tiles,),
            in_specs=[
                pl.BlockSpec((tm, d), lambda i, e1, e2, va: (i, 0)),
                pl.BlockSpec((tm, 2), lambda i, e1, e2, va: (i, 0)),
                pl.BlockSpec((1, 1, d, 2 * D_EXPERT), lambda i, e1, e2, va: (layer, e1[i], 0, 0)),
                pl.BlockSpec((1, 1, d, 2 * D_EXPERT), lambda i, e1, e2, va: (layer, e2[i], 0, 0)),
                pl.BlockSpec((1, 1, D_EXPERT, d), lambda i, e1, e2, va: (layer, e1[i], 0, 0)),
                pl.BlockSpec((1, 1, D_EXPERT, d), lambda i, e1, e2, va: (layer, e2[i], 0, 0)),
            ],
            out_specs=pl.BlockSpec((tm, d), lambda i, e1, e2, va: (i, 0)),
        ),
        out_shape=jax.ShapeDtypeStruct((rows, d), BF16),
        compiler_params=_cp("arbitrary"),
        name="moe_experts",
    )(e1, e2, valid, hs, gs, wgu, wgu, wdn, wdn)
    return ys.at[dest].get(mode="promise_in_bounds")


def _resid_ln_body(x_ref, y_ref, mod_ref, g_ref, b_ref, o_ref):
    m = mod_ref[0]
    o_ref[...] = _layernorm(DN_ALPHA * x_ref[...] + (1.0 + m[5:6]) * y_ref[...].astype(F32),
                            g_ref[...], b_ref[...])


def _resid_ln(x2, y, mod, ln_g, ln_b, seq):
    t, d = x2.shape
    tm = 512
    per_b = seq // tm
    row = lambda i: (i, 0)
    const = lambda i: (0, 0)
    return pl.pallas_call(
        _resid_ln_body,
        grid=(t // tm,),
        in_specs=[pl.BlockSpec((tm, d), row), pl.BlockSpec((tm, d), row),
                  pl.BlockSpec((1, 6, d), lambda i: (i // per_b, 0, 0)),
                  pl.BlockSpec((1, d), const), pl.BlockSpec((1, d), const)],
        out_specs=pl.BlockSpec((tm, d), row),
        out_shape=jax.ShapeDtypeStruct((t, d), F32),
        compiler_params=_cp("parallel"),
        name="resid_ln",
    )(x2, y, mod, ln_g.reshape(1, d), ln_b.reshape(1, d))


def _front1_body(x_ref, mod_ref, w_ref, zc0_ref, zc1_ref, zc2_ref, zr_ref, h_scr, res_scr, *, dils, tm):
    j = pl.program_id(1)
    n_att = 3 * len(dils)

    @pl.when(j == 0)
    def _():
        m = mod_ref[0]
        h_scr[...] = (x_ref[...] * (1.0 + m[1:2]) + m[0:1]).astype(BF16)

    res = _dot(h_scr[...], w_ref[...])

    @pl.when(j >= n_att)
    def _():
        zr_ref[...] = res.astype(BF16)

    n_slab = res.shape[1] // LANES

    @pl.when(j < n_att)
    def _():
        for c in range(n_slab):
            res_scr[c] = res[:, c * LANES:(c + 1) * LANES]

    for g, (o_ref, dil) in enumerate(zip((zc0_ref, zc1_ref, zc2_ref), dils)):
        @pl.when(j // 3 == g)
        def _(o_ref=o_ref, dil=dil):
            rows = tm // dil
            for r in range(dil):
                for c in range(n_slab):
                    o_ref[0, r, :, c * LANES:(c + 1) * LANES] = (
                        res_scr[c, pl.ds(r, rows, stride=dil), :].astype(BF16))


def _front1(x2, mod, od_w_in, batch, seq):
    t, d = x2.shape
    tm = min(1024, seq)
    tn = DIL_WIDTH
    per_b = seq // tm
    n_in = od_w_in.shape[1]
    n = -(-n_in // tn) * tn
    w = jnp.pad(od_w_in, ((0, 0), (0, n - n_in))).astype(BF16)
    dils = tuple(dil for _, dil in DIL_PATTERNS)
    n_att = 3 * len(dils)
    n_r = n // tn - n_att

    def att_spec(g, dil):
        return pl.BlockSpec((1, dil, tm // dil, tn),
                            lambda i, j: (i // per_b, 0, i % per_b, jnp.clip(j - 3 * g, 0, 2)))

    return pl.pallas_call(
        functools.partial(_front1_body, dils=dils, tm=tm),
        grid=(t // tm, n // tn),
        in_specs=[pl.BlockSpec((tm, d), lambda i, j: (i, 0)),
                  pl.BlockSpec((1, 6, d), lambda i, j: (i // per_b, 0, 0)),
                  pl.BlockSpec((d, tn), lambda i, j: (0, j))],
        out_specs=[att_spec(g, dil) for g, dil in enumerate(dils)]
        + [pl.BlockSpec((tm, tn), lambda i, j: (i, jnp.clip(j - n_att, 0, n_r - 1)))],
        out_shape=[jax.ShapeDtypeStruct((batch, dil, seq // dil, 3 * tn), BF16) for dil in dils]
        + [jax.ShapeDtypeStruct((t, n_r * tn), BF16)],
        scratch_shapes=[pltpu.VMEM((tm, d), BF16), pltpu.VMEM((tn // LANES, tm, LANES), F32)],
        compiler_params=_cp("parallel", "arbitrary"),
        name="front1",
    )(x2, mod, w)


def _dil_body(q_ref, kp_ref, kc_ref, vp_ref, vc_ref, bias_ref, o_ref, lse_ref):
    n = pl.program_id(2)
    span = DIL_SPAN
    q = q_ref[0, 0]
    k = jnp.concatenate([kp_ref[0, 0], kc_ref[0, 0]], 0)
    v = jnp.concatenate([vp_ref[0, 0], vc_ref[0, 0]], 0)
    qi = lax.broadcasted_iota(jnp.int32, (span, 2 * span), 0)
    ku = lax.broadcasted_iota(jnp.int32, (span, 2 * span), 1)
    valid = (ku >= qi) & (ku <= qi + span) & ((n >= 1) | (ku >= span))
    lane = lax.broadcasted_iota(jnp.int32, (span, LANES), 1)
    first = lane < DIL_HEAD_DIM
    lse_all = jnp.zeros((span, LANES), F32)
    scale = DIL_HEAD_DIM ** -0.5
    for hp in range(DIL_HEADS // 2):
        sl = slice(hp * LANES, (hp + 1) * LANES)
        q2, k2, v2 = q[:, sl], k[:, sl], v[:, sl]
        outs = []
        for hh in range(2):
            head = 2 * hp + hh
            qm = jnp.where(first if hh == 0 else jnp.logical_not(first), q2, jnp.zeros_like(q2))
            s = _dot_nt(qm, k2) * scale + bias_ref[head]
            s = jnp.where(valid, s, NEG_INF)
            m = jnp.max(s, -1, keepdims=True)
            p = jnp.exp(s - m)
            l = jnp.sum(p, -1, keepdims=True)
            outs.append(_dot(p.astype(BF16), v2) / l)
            lse_all = jnp.where(lane == head, m + jnp.log(l), lse_all)
        o_ref[0, 0, :, sl] = jnp.where(first, outs[0], outs[1]).astype(BF16)
    lse_ref[0, 0] = lse_all


def _t5_bucket(dist):
    exact = T5_BUCKETS // 2
    logd = jnp.log(jnp.maximum(dist, 1).astype(F32) / exact) / math.log(T5_MAX_DIST / exact)
    large = jnp.minimum(exact + (logd * (T5_BUCKETS - exact)).astype(jnp.int32), T5_BUCKETS - 1)
    return jnp.where(dist < exact, dist, large)


def _dilated_group(zc, rel_bias, gi, dil, batch, seq):
    span = DIL_SPAN
    l_stream = seq // dil
    nb = l_stream // span
    qi = jnp.arange(span)[:, None]
    ku = jnp.arange(2 * span)[None, :]
    delta = jnp.clip(span + qi - ku, 0, span) * dil
    onehot = (_t5_bucket(delta)[None] == jnp.arange(T5_BUCKETS)[:, None, None]).astype(F32)
    table = rel_bias.astype(F32)[:, gi * DIL_HEADS:(gi + 1) * DIL_HEADS]
    bias = jnp.einsum("kh,kqu->hqu", table, onehot, precision=lax.Precision.HIGHEST)

    def cur(j):
        return lambda b, r, n: (b, r, n, j)

    def prev(j):
        return lambda b, r, n: (b, r, jnp.maximum(n - 1, 0), j)

    blk = (1, 1, span, DIL_WIDTH)
    return pl.pallas_call(
        _dil_body,
        grid=(batch, dil, nb),
        in_specs=[pl.BlockSpec(blk, cur(0)), pl.BlockSpec(blk, prev(1)), pl.BlockSpec(blk, cur(1)),
                  pl.BlockSpec(blk, prev(2)), pl.BlockSpec(blk, cur(2)),
                  pl.BlockSpec((DIL_HEADS, span, 2 * span), lambda b, r, n: (0, 0, 0))],
        out_specs=[pl.BlockSpec(blk, cur(0)), pl.BlockSpec((1, 1, span, LANES), cur(0))],
        out_shape=[jax.ShapeDtypeStruct((batch, dil, l_stream, DIL_WIDTH), BF16),
                   jax.ShapeDtypeStruct((batch, dil, l_stream, LANES), F32)],
        compiler_params=_cp("parallel", "parallel", "arbitrary"),
        name=f"dilated_attention_{dil}",
    )(zc, zc, zc, zc, zc, bias)


def _dil_merge_body(o1_ref, o2_ref, o3_ref, l1_ref, l2_ref, l3_ref, e_ref, out_ref, o_scr, l1_scr, l2_scr,
                    l3_scr, *, dils, tm):
    def unstream(ref, scr, dil):
        rows = tm // dil
        n_slab = scr.shape[0]
        for r in range(dil):
            x = ref[0, r].astype(F32)
            for c in range(n_slab):
                scr[c, pl.ds(r, rows, stride=dil), :] = x[:, c * LANES:(c + 1) * LANES]
        return jnp.concatenate([scr[c] for c in range(n_slab)], 1)

    lses = [unstream(ref, scr, dil) for ref, scr, dil in
            zip((l1_ref, l2_ref, l3_ref), (l1_scr, l2_scr, l3_scr), dils)]
    top = jnp.maximum(jnp.maximum(lses[0], lses[1]), lses[2])
    num = None
    den = None
    for o_ref, lse, dil in zip((o1_ref, o2_ref, o3_ref), lses, dils):
        w = _dot_exact_rhs(jnp.exp(lse - top), e_ref[...], 2)
        o = unstream(o_ref, o_scr, dil)
        num = w * o if num is None else num + w * o
        den = w if den is None else den + w
    out_ref[...] = (num / den).astype(BF16)


def _dil_merge(outs, lses, batch, seq):
    tm = 512
    per_b = seq // tm
    dils = tuple(dil for _, dil in DIL_PATTERNS)
    head_of_lane = jnp.arange(DIL_WIDTH) // DIL_HEAD_DIM
    expand = (jnp.arange(LANES)[:, None] == head_of_lane[None, :]).astype(BF16)

    def spec(dil, width):
        return pl.BlockSpec((1, dil, tm // dil, width), lambda i: (i // per_b, 0, i % per_b, 0))

    return pl.pallas_call(
        functools.partial(_dil_merge_body, dils=dils, tm=tm),
        grid=(batch * per_b,),
        in_specs=[spec(dil, DIL_WIDTH) for dil in dils] + [spec(dil, LANES) for dil in dils]
        + [pl.BlockSpec((LANES, DIL_WIDTH), lambda i: (0, 0))],
        out_specs=pl.BlockSpec((tm, DIL_WIDTH), lambda i: (i, 0)),
        out_shape=jax.ShapeDtypeStruct((batch * seq, DIL_WIDTH), BF16),
        scratch_shapes=[pltpu.VMEM((DIL_WIDTH // LANES, tm, LANES), F32)] + [pltpu.VMEM((1, tm, LANES), F32)] * 3,
        compiler_params=_cp("parallel"),
        name="dilated_merge",
    )(*outs, *lses, expand)


def _rwkv_body(zr_ref, zk_ref, zv_ref, zm_ref, mur_ref, muk_ref, muv_ref, mum_ref, w0_ref, w2_ref,
               a0_ref, a2_ref, g2_ref, kk_ref, ka_ref, rk_ref, lng_ref, lnb_ref, o_ref,
               st_ref, pr_ref, pk_ref, pv_ref, pm_ref, xr_s, xk_s, xv_s, lw_s, as_s, g_s, cum_s):
    lc = RW_CHUNK
    n_slabs = st_ref.shape[0]

    @pl.when(pl.program_id(1) == 0)
    def _():
        st_ref[...] = jnp.zeros_like(st_ref)
        pr_ref[...] = jnp.zeros_like(pr_ref)
        pk_ref[...] = jnp.zeros_like(pk_ref)
        pv_ref[...] = jnp.zeros_like(pv_ref)
        pm_ref[...] = jnp.zeros_like(pm_ref)

    def shifted(z_ref, prev_ref, mu_ref):
        z = z_ref[...].astype(F32)
        first_row = lax.broadcasted_iota(jnp.int32, z.shape, 0) == 0
        prev = jnp.where(first_row, prev_ref[...], pltpu.roll(z, 1, 0))
        prev_ref[...] = z[lc - 1:lc, :]
        return z + (prev - z) * mu_ref[...]

    xr_s[...] = shifted(zr_ref, pr_ref, mur_ref)
    xk_s[...] = shifted(zk_ref, pk_ref, muk_ref)
    xv_s[...] = shifted(zv_ref, pv_ref, muv_ref)
    xm = shifted(zm_ref, pm_ref, mum_ref)
    lora_in = xm[:, :LANES]
    wl = w0_ref[...] + _dot(jnp.tanh(lora_in).astype(BF16), w2_ref[...])
    w_log = -(jnp.maximum(-wl, 0.0) + jnp.log(1.0 + jnp.exp(-jnp.abs(wl)))) - 0.5
    lw = -jnp.exp(w_log)
    lw_s[...] = lw
    as_s[...] = jax.nn.sigmoid(a0_ref[...] + _dot(lora_in.astype(BF16), a2_ref[...]))
    g_s[...] = _dot(jax.nn.sigmoid(xm[:, LANES:]).astype(BF16), g2_ref[...])
    ti = lax.broadcasted_iota(jnp.int32, (lc, lc), 0)
    si = lax.broadcasted_iota(jnp.int32, (lc, lc), 1)
    cum_s[...] = _dot_exact_lhs((si <= ti).astype(BF16), lw, 3)

    row = lax.broadcasted_iota(jnp.int32, (2 * lc, 2 * lc), 0)
    colm = lax.broadcasted_iota(jnp.int32, (2 * lc, 2 * lc), 1)
    same_head = (row >= lc) == (colm >= lc)
    t_in = row & (lc - 1)
    s_in = colm & (lc - 1)
    strict = same_head & (s_in < t_in)
    incl = same_head & (s_in <= t_in)
    eye = row == colm
    ones_bd = same_head.astype(BF16)
    lane = lax.broadcasted_iota(jnp.int32, (lc, LANES), 1)
    first = lane < RW_HEAD

    def seg(x):
        return _dot_exact_rhs(x, ones_bd, 2)

    def stack_heads(x):
        return jnp.concatenate([jnp.where(first, x, 0.0), jnp.where(first, 0.0, x)], 0)

    slabs = range(n_slabs)
    sls = [slice(hp * LANES, (hp + 1) * LANES) for hp in slabs]
    kk_raw = [xk_s[:, sl] * kk_ref[:, sl] for sl in sls]
    kk_ss = [seg(kk * kk) for kk in kk_raw]
    lhs, rhs, bk_t, v_bs, k2s, pl_cols = [], [], [], [], [], []
    for hp, sl in enumerate(sls):
        a_sig = as_s[:, sl]
        cum = cum_s[:, sl]
        kk = kk_raw[hp] / jnp.maximum(jnp.sqrt(kk_ss[hp]), 1e-12)
        k2 = xk_s[:, sl] * (1.0 + (a_sig - 1.0) * ka_ref[:, sl])
        b_p = kk * a_sig
        cl = cum[lc - 1:lc, :]
        e_neg = jnp.exp(-cum)
        e_end = jnp.exp(cl - cum)
        at = -kk * jnp.exp(cum - lw_s[:, sl])
        rt = xr_s[:, sl] * jnp.exp(cum)
        lhs.append(jnp.concatenate([stack_heads(at), stack_heads(rt)], 0).astype(BF16))
        bt = (b_p * e_neg).astype(BF16)
        kt = (k2 * e_neg).astype(BF16)
        rhs.append(jnp.concatenate([bt, bt, kt, kt], 0))
        bk_t.append(jnp.concatenate([b_p * e_end, k2 * e_end], 0).T.astype(BF16))
        v_bs.append(xv_s[:, sl].astype(BF16))
        k2s.append(k2)
        pl_cols.append(jnp.sum(jnp.where(eye, jnp.broadcast_to(jnp.exp(cl), (2 * lc, LANES)), 0.0), 1,
                               keepdims=True))
    qms = [_dot_nt(lhs[hp], rhs[hp]) for hp in slabs]
    states = [st_ref[hp] for hp in slabs]
    arss = [_dot(lhs[hp], states[hp].astype(BF16)) for hp in slabs]
    a_abs = [jnp.where(strict, qm[:2 * lc, :2 * lc], 0.0).astype(BF16) for qm in qms]
    xs = [jnp.where(eye, 1.0, 0.0) + a.astype(F32) for a in a_abs]
    ps = [_dot(a, a).astype(BF16) for a in a_abs]
    n_double = int(math.log2(lc)) - 1
    for it in range(n_double - 1):
        res = [_dot(ps[hp], jnp.concatenate([xs[hp].astype(BF16), ps[hp]], 1)) for hp in slabs]
        xs = [xs[hp] + res[hp][:, :2 * lc] for hp in slabs]
        ps = [res[hp][:, 2 * lc:].astype(BF16) for hp in slabs]
    xs = [xs[hp] + _dot(ps[hp], xs[hp].astype(BF16)) for hp in slabs]
    vss = [jnp.concatenate([v_b, v_b], 0) for v_b in v_bs]
    gmats = [_dot(jnp.where(strict, qms[hp][:2 * lc, 2 * lc:], 0.0).astype(BF16), vss[hp]) + arss[hp][:2 * lc]
             for hp in slabs]
    uss = [_dot(xs[hp].astype(BF16), gmats[hp].astype(BF16)) for hp in slabs]
    incl2 = jnp.concatenate([incl, incl], 1)
    yss = [_dot(jnp.where(incl2, qms[hp][2 * lc:, :], 0.0).astype(BF16),
                jnp.concatenate([uss[hp].astype(BF16), vss[hp]], 0)) + arss[hp][2 * lc:] for hp in slabs]
    us_p = [jnp.where(first, us[:lc], us[lc:]) for us in uss]
    ys_p = [jnp.where(first, ys[:lc], ys[lc:]) for ys in yss]
    upds = [_dot(bk_t[hp], jnp.concatenate([us_p[hp].astype(BF16), v_bs[hp]], 0)) for hp in slabs]
    for hp in slabs:
        st_ref[hp] = pl_cols[hp] * states[hp] + jnp.where(same_head, upds[hp], 0.0)
    means = [seg(y) * (1.0 / RW_HEAD) for y in ys_p]
    dys = [ys_p[hp] - means[hp] for hp in slabs]
    variances = [seg(dy * dy) * (1.0 / RW_HEAD) for dy in dys]
    bonus = [seg(xr_s[:, sl] * k2s[hp] * rk_ref[:, sl]) for hp, sl in enumerate(sls)]
    for hp, sl in enumerate(sls):
        yn = dys[hp] * lax.rsqrt(variances[hp] + RW_GN_EPS) * lng_ref[:, sl] + lnb_ref[:, sl]
        o_ref[:, sl] = ((yn + bonus[hp] * xv_s[:, sl]) * g_s[:, sl]).astype(BF16)


def _rwkv(z, batch, seq, mu, w0, w2, a0, a2, g2, k_k, k_a, r_k, lnx_g, lnx_b):
    t = z.shape[0]
    lc = RW_CHUNK
    nc = seq // lc
    width = w0.shape[0]
    n_slabs = width // LANES
    col0 = 0
    misc_blk = 3 * width // RW_MISC
    n_lora = RW_LORA_W + RW_LORA_A + RW_LORA_G
    mu_m = jnp.pad(mu[3 * width:], (0, RW_MISC - n_lora)).reshape(1, RW_MISC)
    w2p = jnp.pad(w2, ((0, LANES - RW_LORA_W), (0, 0))).astype(BF16)
    a2p = jnp.pad(a2, ((RW_LORA_W, LANES - RW_LORA_W - RW_LORA_A), (0, 0))).astype(BF16)
    g2p = jnp.pad(g2, ((0, RW_MISC - LANES - RW_LORA_G), (0, 0))).astype(BF16)
    vec = lambda a: a.reshape(1, width).astype(F32)
    const = lambda b, c: (0, 0)
    wide = lambda: pl.BlockSpec((1, width), const)
    return pl.pallas_call(
        _rwkv_body,
        grid=(batch, nc),
        in_specs=[
            pl.BlockSpec((lc, width), lambda b, c: (b * nc + c, col0)),
            pl.BlockSpec((lc, width), lambda b, c: (b * nc + c, col0 + 1)),
            pl.BlockSpec((lc, width), lambda b, c: (b * nc + c, col0 + 2)),
            pl.BlockSpec((lc, RW_MISC), lambda b, c: (b * nc + c, misc_blk)),
            wide(), wide(), wide(), pl.BlockSpec((1, RW_MISC), const),
            wide(), pl.BlockSpec((LANES, width), const),
            wide(), pl.BlockSpec((LANES, width), const),
            pl.BlockSpec((RW_MISC - LANES, width), const),
            wide(), wide(), wide(), wide(), wide(),
        ],
        out_specs=pl.BlockSpec((lc, width), lambda b, c: (b * nc + c, 0)),
        out_shape=jax.ShapeDtypeStruct((t, width), BF16),
        scratch_shapes=[pltpu.VMEM((n_slabs, LANES, LANES), F32),
                        pltpu.VMEM((1, width), F32), pltpu.VMEM((1, width), F32),
                        pltpu.VMEM((1, width), F32), pltpu.VMEM((1, RW_MISC), F32)]
        + [pltpu.VMEM((lc, width), F32)] * 7,
        compiler_params=_cp("parallel", "arbitrary"),
        name="rwkv7",
    )(z, z, z, z, vec(mu[:width]), vec(mu[width:2 * width]), vec(mu[2 * width:3 * width]), mu_m,
      vec(w0), w2p, vec(a0), a2p, g2p, vec(k_k), vec(k_a), vec(r_k), vec(lnx_g), vec(lnx_b))


def kernel(x, c, ada_w, ada_b, ln_mix_g, ln_mix_b, ln_ffn_g, ln_ffn_b, router_w, router_b, moe_w_gate_up, moe_w_down, rel_bias, ev_w_in, mla_q_norm, mla_w_uq, mla_kv_norm, mla_w_ukv, s5_lambda_re, s5_lambda_im, s5_b_re, s5_b_im, s5_c_re, s5_c_im, s5_d, s5_log_dt, s5_w_glu, ev_w_out, od_w_in, rw_mu, rw_w0, rw_w2, rw_a0, rw_a2, rw_g2, rw_k_k, rw_k_a, rw_r_k, rw_lnx_g, rw_lnx_b, od_w_out):
    batch, seq, d = x.shape
    assert seq % 2048 == 0 and d % LANES == 0
    x2 = x.reshape(batch * seq, d)
    mods = _ada(c, ada_w, ada_b)
    wgu_all = _cast_bf16(moe_w_gate_up, moe_w_gate_up.shape[-2])
    wdn_all = _cast_bf16(moe_w_down, 2 * moe_w_down.shape[-2])
    for layer in range(DEPTH):
        mod = mods[layer]
        if layer % 2 == 0:
            e = layer // 2
            q, k, v, u = _front0(x2, mod, seq, ev_w_in[e], mla_q_norm[e], mla_w_uq[e], mla_kv_norm[e],
                                 mla_w_ukv[e])
            att = _flash(q, k, v, batch, seq)
            n_steps = max(1, math.ceil(math.log2(seq // S5_CHUNK)))
            mats = _s5_matrices(s5_lambda_re[e], s5_lambda_im[e], s5_b_re[e], s5_b_im[e], s5_c_re[e],
                                s5_c_im[e], s5_log_dt[e], n_steps)
            ssm = _s5_glu(_s5_scan(u, batch, seq, mats), u, s5_d[e], s5_w_glu[e])
            x2, h, logits = _outproj(att, ssm, ev_w_out[e], x2, mod, ln_mix_g[layer], ln_mix_b[layer],
                                     router_w, seq)
        else:
            o = layer // 2
            *zcs, z = _front1(x2, mod, od_w_in[o], batch, seq)
            outs, lses = [], []
            for gi, (win, dil) in enumerate(DIL_PATTERNS):
                assert win // dil == DIL_SPAN
                og, lg = _dilated_group(zcs[gi], rel_bias, gi, dil, batch, seq)
                outs.append(og)
                lses.append(lg)
            att = _dil_merge(outs, lses, batch, seq)
            tm_out = _rwkv(z, batch, seq, rw_mu[o], rw_w0[o], rw_w2[o], rw_a0[o], rw_a2[o], rw_g2[o],
                           rw_k_k[o], rw_k_a[o], rw_r_k[o], rw_lnx_g[o], rw_lnx_b[o])
            x2, h, logits = _outproj(att, tm_out, od_w_out[o], x2, mod, ln_mix_g[layer], ln_mix_b[layer],
                                     router_w, seq)
        cls, glo, ghi = _route(logits, router_b)
        y = _moe(h, cls, glo, ghi, wgu_all, wdn_all, layer)
        x2 = _resid_ln(x2, y, mod, ln_ffn_g[layer], ln_ffn_b[layer], seq)
    return x2.reshape(batch, seq, d)
```

```python
import functools
import math

import jax
import jax.numpy as jnp
from jax import lax
from jax.experimental import pallas as pl
from jax.experimental.pallas import tpu as pltpu

F32 = jnp.float32
BF16 = jnp.bfloat16

DEPTH = 2
DN_ALPHA = (2.0 * DEPTH) ** 0.25
LN_EPS = 1e-5
RMS_EPS = 1e-6
NEG_INF = -1e30

MLA_HEADS = 8
MLA_NOPE = 128
MLA_ROPE = 64
MLA_V = 128
MLA_QK = MLA_NOPE + MLA_ROPE
MLA_Q_RANK = 512
MLA_KV_RANK = 256
ROPE_THETA = 10000.0
MLA_HEAD_PAD = 256

S5_GROUP = 16
S5_STATE = 64
S5_CHUNK = 16

DIL_PATTERNS = ((128, 1), (512, 4), (2048, 16))
DIL_SPAN = 128
DIL_HEADS = 8
DIL_HEAD_DIM = 64
DIL_WIDTH = DIL_HEADS * DIL_HEAD_DIM
DIL_IN = len(DIL_PATTERNS) * 3 * DIL_WIDTH
T5_BUCKETS = 32
T5_MAX_DIST = 2048

RW_HEAD = 64
RW_LORA_W = 64
RW_LORA_A = 64
RW_LORA_G = 224
RW_GN_EPS = 64e-5
RW_CHUNK = 64
RW_MISC = 512

N_EXPERTS = 16
N_EXPERT_GROUPS = 4
EXPERTS_PER_GROUP = 4
D_EXPERT = 512
N_PAIRS = 6
N_CLASSES = N_EXPERT_GROUPS * N_PAIRS
MOE_TILE = 256

LANES = 128
VMEM_LIMIT = 56 * 1024 * 1024


def _cp(*sem):
    return pltpu.CompilerParams(dimension_semantics=sem, vmem_limit_bytes=VMEM_LIMIT)


def _dot(a, b):
    return jnp.dot(a, b, preferred_element_type=F32)


def _dot_nt(a, b):
    return lax.dot_general(a, b, (((1,), (1,)), ((), ())), preferred_element_type=F32)


def _split_bf16(x, parts):
    out = []
    for _ in range(parts):
        hi = x.astype(BF16)
        out.append(hi)
        x = x - hi.astype(F32)
    return out


def _dot_exact_rhs(x, w_bf16, parts=3):
    acc = None
    for p in _split_bf16(x, parts):
        t = _dot(p, w_bf16)
        acc = t if acc is None else acc + t
    return acc


def _dot_exact_lhs(w_bf16, x, parts=3):
    acc = None
    for p in _split_bf16(x, parts):
        t = _dot(w_bf16, p)
        acc = t if acc is None else acc + t
    return acc


def _layernorm(x, g, b):
    mu = jnp.mean(x, -1, keepdims=True)
    d = x - mu
    var = jnp.mean(d * d, -1, keepdims=True)
    return d * lax.rsqrt(var + LN_EPS) * g + b


def _cast_body(x_ref, o_ref):
    o_ref[...] = x_ref[...].astype(BF16)


def _cast_bf16(w, block_rows):
    w2 = w.reshape(-1, w.shape[-1])
    rows, cols = w2.shape
    spec = pl.BlockSpec((block_rows, cols), lambda i: (i, 0))
    out = pl.pallas_call(
        _cast_body, grid=(rows // block_rows,), in_specs=[spec], out_specs=spec,
        out_shape=jax.ShapeDtypeStruct((rows, cols), BF16), compiler_params=_cp("parallel"), name="cast_bf16",
    )(w2)
    return out.reshape(w.shape)


def _ada_body(c_ref, w_ref, b_ref, o_ref):
    c = c_ref[...]
    cond = c * jax.nn.sigmoid(c)
    o_ref[0] = _dot_exact_rhs(cond, w_ref[0].astype(BF16), 2) + b_ref[0]


def _ada(c, ada_w, ada_b):
    depth, d, n = ada_w.shape
    b = c.shape[0]
    rows = -(-b // 8) * 8
    cp = jnp.pad(c, ((0, rows - b), (0, 0)))
    tn = 1024
    out = pl.pallas_call(
        _ada_body,
        grid=(depth, n // tn),
        in_specs=[
            pl.BlockSpec((rows, d), lambda l, j: (0, 0)),
            pl.BlockSpec((1, d, tn), lambda l, j: (l, 0, j)),
            pl.BlockSpec((1, 1, tn), lambda l, j: (l, 0, j)),
        ],
        out_specs=pl.BlockSpec((1, rows, tn), lambda l, j: (l, 0, j)),
        out_shape=jax.ShapeDtypeStruct((depth, rows, n), F32),
        compiler_params=_cp("parallel", "parallel"),
        name="ada",
    )(cp, ada_w, ada_b.reshape(depth, 1, n))
    return out[:, :b].reshape(depth, b, 6, d)


def _front0_body(x_ref, mod_ref, cs_ref, win_ref, qn_ref, kvn_ref, wq1_ref, wq2_ref, wkn_ref,
                 wv_ref, q_ref, k_ref, v_ref, u_ref):
    m = mod_ref[0]
    h = (x_ref[...] * (1.0 + m[1:2]) + m[0:1]).astype(BF16)
    z = _dot(h, win_ref[...])
    cs = cs_ref[...]
    c2 = cs[:, :LANES]
    s2 = cs[:, LANES:]
    q_c = z[:, :MLA_Q_RANK]
    kv_c = z[:, MLA_Q_RANK:MLA_Q_RANK + MLA_KV_RANK]
    o = MLA_Q_RANK + MLA_KV_RANK
    krope = (z[:, o:o + LANES] * c2 + z[:, o + LANES:o + 2 * LANES] * s2).astype(BF16)
    u_ref[...] = z[:, o + 2 * LANES:]
    hq = (q_c * lax.rsqrt(jnp.mean(q_c * q_c, -1, keepdims=True) + RMS_EPS) * qn_ref[...]).astype(BF16)
    hkv = (kv_c * lax.rsqrt(jnp.mean(kv_c * kv_c, -1, keepdims=True) + RMS_EPS) * kvn_ref[...]).astype(BF16)
    full = _dot(hq, wq1_ref[...])
    sw = _dot(hq, wq2_ref[...])
    kn = _dot(hkv, wkn_ref[...])
    v_ref[...] = _dot(hkv, wv_ref[...]).astype(BF16)
    for j in range(MLA_HEADS):
        a = j * MLA_HEAD_PAD
        q_ref[:, a:a + LANES] = full[:, a:a + LANES].astype(BF16)
        q_ref[:, a + LANES:a + 2 * LANES] = (
            full[:, a + LANES:a + 2 * LANES] * c2 + sw[:, j * LANES:(j + 1) * LANES] * s2).astype(BF16)
        k_ref[:, a:a + LANES] = kn[:, j * LANES:(j + 1) * LANES].astype(BF16)
        k_ref[:, a + LANES:a + 2 * LANES] = krope


def _rope_swapped(w):
    half = w.shape[-1] // 2
    return jnp.concatenate([-w[..., half:], w[..., :half]], -1)


def _front0(x2, mod, seq, ev_w_in, q_norm, w_uq, kv_norm, w_ukv):
    t, d = x2.shape
    tm = 256
    per_b = seq // tm
    zpad = jnp.zeros((d, LANES - MLA_ROPE), F32)
    o = MLA_Q_RANK + MLA_KV_RANK
    w_kr = ev_w_in[:, o:o + MLA_ROPE]
    win = jnp.concatenate([ev_w_in[:, :o], w_kr, zpad, _rope_swapped(w_kr), zpad,
                           ev_w_in[:, o + MLA_ROPE:]], 1).astype(BF16)
    scale = MLA_QK ** -0.5
    wq = w_uq.reshape(MLA_Q_RANK, MLA_HEADS, MLA_QK) * scale
    zq = jnp.zeros((MLA_Q_RANK, MLA_HEADS, MLA_HEAD_PAD - MLA_QK), F32)
    wq1 = jnp.concatenate([wq, zq], -1).reshape(MLA_Q_RANK, -1).astype(BF16)
    wq2 = jnp.concatenate([_rope_swapped(wq[..., MLA_NOPE:]), zq], -1).reshape(MLA_Q_RANK, -1).astype(BF16)
    wkv = w_ukv.reshape(MLA_KV_RANK, MLA_HEADS, MLA_NOPE + MLA_V)
    wkn = wkv[..., :MLA_NOPE].reshape(MLA_KV_RANK, -1).astype(BF16)
    wv = wkv[..., MLA_NOPE:].reshape(MLA_KV_RANK, -1).astype(BF16)
    inv = ROPE_THETA ** (-jnp.arange(0, MLA_ROPE, 2, dtype=F32) / MLA_ROPE)
    ang = jnp.arange(seq, dtype=F32)[:, None] * inv[None]
    zt = jnp.zeros((seq, LANES - MLA_ROPE), F32)
    cs = jnp.concatenate([jnp.cos(ang), jnp.cos(ang), zt, jnp.sin(ang), jnp.sin(ang), zt], 1)
    const = lambda i: (0, 0)
    hq_w = MLA_HEADS * MLA_HEAD_PAD
    hv_w = MLA_HEADS * MLA_V
    s5w = ev_w_in.shape[1] - o - MLA_ROPE
    return pl.pallas_call(
        _front0_body,
        grid=(t // tm,),
        in_specs=[
            pl.BlockSpec((tm, d), lambda i: (i, 0)),
            pl.BlockSpec((1, 6, d), lambda i: (i // per_b, 0, 0)),
            pl.BlockSpec((tm, 2 * LANES), lambda i: (i % per_b, 0)),
            pl.BlockSpec(win.shape, const),
            pl.BlockSpec((1, MLA_Q_RANK), const),
            pl.BlockSpec((1, MLA_KV_RANK), const),
            pl.BlockSpec(wq1.shape, const),
            pl.BlockSpec(wq2.shape, const),
            pl.BlockSpec(wkn.shape, const),
            pl.BlockSpec(wv.shape, const),
        ],
        out_specs=[
            pl.BlockSpec((tm, hq_w), lambda i: (i, 0)),
            pl.BlockSpec((tm, hq_w), lambda i: (i, 0)),
            pl.BlockSpec((tm, hv_w), lambda i: (i, 0)),
            pl.BlockSpec((tm, s5w), lambda i: (i, 0)),
        ],
        out_shape=[
            jax.ShapeDtypeStruct((t, hq_w), BF16),
            jax.ShapeDtypeStruct((t, hq_w), BF16),
            jax.ShapeDtypeStruct((t, hv_w), BF16),
            jax.ShapeDtypeStruct((t, s5w), F32),
        ],
        compiler_params=_cp("parallel"),
        name="front0",
    )(x2, mod, cs, win, q_norm.reshape(1, -1), kv_norm.reshape(1, -1), wq1, wq2, wkn, wv)


def _flash_body(q_ref, k_ref, v_ref, o_ref, *, tile, heads):
    i = pl.program_id(2)
    qs = [q_ref[:, h * MLA_HEAD_PAD:(h + 1) * MLA_HEAD_PAD] for h in range(heads)]

    def one_head(h, start, carry, diagonal):
        m, l, acc = carry
        k = k_ref[pl.ds(start, tile), h * MLA_HEAD_PAD:(h + 1) * MLA_HEAD_PAD]
        v = v_ref[pl.ds(start, tile), h * MLA_V:(h + 1) * MLA_V]
        s = _dot_nt(qs[h], k)
        if diagonal:
            row = lax.broadcasted_iota(jnp.int32, s.shape, 0)
            col = lax.broadcasted_iota(jnp.int32, s.shape, 1)
            s = jnp.where(col <= row, s, NEG_INF)
        m_new = jnp.maximum(m, jnp.max(s, -1, keepdims=True))
        alpha = jnp.exp(m - m_new)
        p = jnp.exp(s - m_new)
        l = l * alpha + jnp.sum(p, -1, keepdims=True)
        acc = acc * alpha + _dot(p.astype(BF16), v)
        return m_new, l, acc

    def step(j, carry, diagonal):
        start = pl.multiple_of(j * tile, tile)
        return tuple(one_head(h, start, carry[h], diagonal) for h in range(heads))

    init = tuple((jnp.full((tile, 1), NEG_INF, F32), jnp.zeros((tile, 1), F32),
                  jnp.zeros((tile, MLA_V), F32)) for _ in range(heads))
    carry = lax.fori_loop(0, i, lambda j, c: step(j, c, False), init)
    carry = step(i, carry, True)
    for h, (_, l, acc) in enumerate(carry):
        o_ref[:, h * MLA_V:(h + 1) * MLA_V] = (acc / l).astype(BF16)


def _flash(q, k, v, batch, seq):
    t = q.shape[0]
    tile = 512
    heads = 2
    nq = seq // tile
    return pl.pallas_call(
        functools.partial(_flash_body, tile=tile, heads=heads),
        grid=(batch, MLA_HEADS // heads, nq),
        in_specs=[
            pl.BlockSpec((tile, heads * MLA_HEAD_PAD), lambda b, h, i: (b * nq + i, h)),
            pl.BlockSpec((seq, heads * MLA_HEAD_PAD), lambda b, h, i: (b, h)),
            pl.BlockSpec((seq, heads * MLA_V), lambda b, h, i: (b, h)),
        ],
        out_specs=pl.BlockSpec((tile, heads * MLA_V), lambda b, h, i: (b * nq + i, h)),
        out_shape=jax.ShapeDtypeStruct((t, MLA_HEADS * MLA_V), BF16),
        compiler_params=_cp("parallel", "parallel", "arbitrary"),
        name="mla_attention",
    )(q, k, v)


S5_GROUPS_PER_BLOCK = LANES // S5_GROUP


def _s5_body(u_ref, w_ref, v_ref, bd_ref, ac_ref, bc_ref, y_ref, ucat, yx, *, n_chunks, n_steps):
    lc = S5_CHUNK
    for t in range(lc):
        ucat[:, t * LANES:(t + 1) * LANES] = u_ref[pl.ds(t, n_chunks, stride=lc), :].astype(BF16)
    z = _dot(ucat[...], w_ref[0])
    half = z.shape[1] // 2
    cidx = lax.broadcasted_iota(jnp.int32, z.shape, 0)
    for step in range(n_steps):
        d = 1 << step
        zs = jnp.where(cidx >= d, pltpu.roll(z, d, 0), 0.0)
        z = z + zs * ac_ref[0, step:step + 1, :] + pltpu.roll(zs, half, 1) * bc_ref[0, step:step + 1, :]
    xprev = jnp.where(cidx >= 1, pltpu.roll(z, 1, 0), 0.0)
    yx[...] = _dot(xprev.astype(BF16), v_ref[0])
    for t in range(lc):
        k = (t + 1) * LANES
        y = _dot(ucat[:, :k], bd_ref[0, (lc - 1 - t) * LANES:, :]) + yx[:, t * LANES:(t + 1) * LANES]
        y_ref[pl.ds(t, n_chunks, stride=lc), :] = y


def _s5_matrices(lam_re, lam_im, b_re, b_im, c_re, c_im, log_dt, n_steps):
    lc = S5_CHUNK
    gb = S5_GROUPS_PER_BLOCK
    lam = lax.complex(lam_re.astype(F32), lam_im.astype(F32))
    n_blk = lam.shape[0] // gb
    dt = jnp.exp(log_dt.astype(F32))[:, None]
    lam_bar = jnp.exp(lam * dt)
    b_bar = ((lam_bar - 1.0) / lam)[..., None] * lax.complex(b_re.astype(F32), b_im.astype(F32))
    c = lax.complex(c_re.astype(F32), c_im.astype(F32))
    j = jnp.arange(lc + 1, dtype=F32)[:, None, None]
    pows = jnp.exp(j * (lam * dt)[None])
    eye = jnp.eye(gb, dtype=F32)
    kern = jnp.einsum("gop,jgp,gpi->jgio", c, pows[:lc], b_bar).real
    kern = kern.reshape(lc, n_blk, gb, S5_GROUP, S5_GROUP)
    bd = jnp.einsum("jbgio,gh->bjgiho", kern, eye).reshape(n_blk, lc, LANES, LANES)
    bdrev = bd[:, ::-1].reshape(n_blk, lc * LANES, LANES)
    tt = jnp.arange(lc)
    wm = pows[lc - 1 - tt][..., None] * b_bar[None]
    wm = jnp.stack([wm.real, wm.imag], 2)
    wm = wm.reshape(lc, n_blk, gb, 2, S5_STATE, S5_GROUP)
    wcat = jnp.einsum("tbgrpi,gh->btgirhp", wm, eye).reshape(n_blk, lc * LANES, 2 * gb * S5_STATE)
    cv = c[None] * pows[1:lc + 1][:, :, None, :]
    cv = jnp.stack([cv.real, -cv.imag], 3)
    cv = cv.reshape(lc, n_blk, gb, S5_GROUP, 2, S5_STATE)
    vcat = jnp.einsum("tbgorp,gh->brgptho", cv, eye).reshape(n_blk, 2 * gb * S5_STATE, lc * LANES)
    stride = (lc * (2.0 ** jnp.arange(n_steps, dtype=F32)))[None, :, None]
    ap = jnp.exp(stride * (lam * dt)[:, None, :])

    def per_block(a):
        return a.reshape(n_blk, gb, n_steps, S5_STATE).transpose(0, 2, 1, 3).reshape(n_blk, n_steps, -1)

    ar, ai = per_block(ap.real), per_block(ap.imag)
    ac = jnp.concatenate([ar, ar], -1)
    bc = jnp.concatenate([-ai, ai], -1)
    return wcat.astype(BF16), vcat.astype(BF16), bdrev.astype(BF16), ac, bc


def _s5_scan(u, batch, seq, mats):
    wcat, vcat, bdrev, ac, bc = mats
    n_blk = wcat.shape[0]
    lc = S5_CHUNK
    nc = seq // lc
    n_steps = ac.shape[1]
    wide = wcat.shape[2]
    per_blk = lambda i, b: (i, 0, 0)
    return pl.pallas_call(
        functools.partial(_s5_body, n_chunks=nc, n_steps=n_steps),
        grid=(n_blk, batch),
        in_specs=[
            pl.BlockSpec((seq, LANES), lambda i, b: (b, i)),
            pl.BlockSpec((1, lc * LANES, wide), per_blk),
            pl.BlockSpec((1, wide, lc * LANES), per_blk),
            pl.BlockSpec((1, lc * LANES, LANES), per_blk),
            pl.BlockSpec((1, n_steps, wide), per_blk),
            pl.BlockSpec((1, n_steps, wide), per_blk),
        ],
        out_specs=pl.BlockSpec((seq, LANES), lambda i, b: (b, i)),
        out_shape=jax.ShapeDtypeStruct(u.shape, F32),
        scratch_shapes=[pltpu.VMEM((nc, lc * LANES), BF16), pltpu.VMEM((nc, lc * LANES), F32)],
        compiler_params=_cp("parallel", "arbitrary"),
        name="s5_scan",
    )(u, wcat, vcat, bdrev, ac, bc)


def _s5_glu_body(y_ref, u_ref, d_ref, w_ref, o_ref):
    y = y_ref[...] + d_ref[...] * u_ref[...]
    y = 0.5 * y * (1.0 + jnp.tanh(math.sqrt(2.0 / math.pi) * (y + 0.044715 * (y * y * y))))
    gate = jax.nn.sigmoid(_dot(y.astype(BF16), w_ref[...]))
    o_ref[...] = (y * gate).astype(BF16)


def _s5_glu(y, u, d_skip, w_glu):
    t, n = y.shape
    tm = 512
    row = lambda i: (i, 0)
    const = lambda i: (0, 0)
    return pl.pallas_call(
        _s5_glu_body,
        grid=(t // tm,),
        in_specs=[pl.BlockSpec((tm, n), row), pl.BlockSpec((tm, n), row),
                  pl.BlockSpec((1, n), const), pl.BlockSpec((n, n), const)],
        out_specs=pl.BlockSpec((tm, n), row),
        out_shape=jax.ShapeDtypeStruct((t, n), BF16),
        compiler_params=_cp("parallel"),
        name="s5_glu",
    )(y, u, d_skip.reshape(1, n), w_glu.astype(BF16))


def _outproj_body(a1_ref, a2_ref, w1_ref, w2_ref, x_ref, mod_ref, g_ref, b_ref, rwh_ref, rwl_ref,
                  xo_ref, h_ref, lg_ref):
    y = _dot(a1_ref[...], w1_ref[...]) + _dot(a2_ref[...], w2_ref[...])
    m = mod_ref[0]
    xn = _layernorm(DN_ALPHA * x_ref[...] + (1.0 + m[2:3]) * y, g_ref[...], b_ref[...])
    xo_ref[...] = xn
    h = xn * (1.0 + m[4:5]) + m[3:4]
    h_ref[...] = h.astype(BF16)
    h_hi, h_lo = _split_bf16(h, 2)
    lg_ref[...] = _dot(h_hi, rwh_ref[...]) + (_dot(h_lo, rwh_ref[...]) + _dot(h_hi, rwl_ref[...]))


def _outproj(a1, a2, w_out, x2, mod, ln_g, ln_b, router_w, seq):
    t, d = x2.shape
    k1 = a1.shape[1]
    k2 = a2.shape[1]
    tm = 256
    per_b = seq // tm
    w1 = w_out[:k1].astype(BF16)
    w2 = w_out[k1:].astype(BF16)
    rwh, rwl = _split_bf16(router_w, 2)
    ne = router_w.shape[1]
    row = lambda i: (i, 0)
    const = lambda i: (0, 0)
    return pl.pallas_call(
        _outproj_body,
        grid=(t // tm,),
        in_specs=[
            pl.BlockSpec((tm, k1), row), pl.BlockSpec((tm, k2), row),
            pl.BlockSpec((k1, d), const), pl.BlockSpec((k2, d), const),
            pl.BlockSpec((tm, d), row),
            pl.BlockSpec((1, 6, d), lambda i: (i // per_b, 0, 0)),
            pl.BlockSpec((1, d), const), pl.BlockSpec((1, d), const),
            pl.BlockSpec((d, ne), const), pl.BlockSpec((d, ne), const),
        ],
        out_specs=[pl.BlockSpec((tm, d), row), pl.BlockSpec((tm, d), row), pl.BlockSpec((tm, ne), row)],
        out_shape=[jax.ShapeDtypeStruct((t, d), F32), jax.ShapeDtypeStruct((t, d), BF16),
                   jax.ShapeDtypeStruct((t, ne), F32)],
        compiler_params=_cp("parallel"),
        name="outproj_ln",
    )(a1, a2, w1, w2, x2, mod, ln_g.reshape(1, d), ln_b.reshape(1, d), rwh, rwl)


def _route_body(lg_ref, rb_ref, cls_ref, glo_ref, ghi_ref):
    s = jax.nn.sigmoid(lg_ref[...])
    sb = s + rb_ref[...]
    rows = [sb[e:e + 1, :] for e in range(N_EXPERTS)]
    raw = [s[e:e + 1, :] for e in range(N_EXPERTS)]
    gscore = []
    for g in range(N_EXPERT_GROUPS):
        v = rows[g * EXPERTS_PER_GROUP:(g + 1) * EXPERTS_PER_GROUP]
        best = None
        for a in range(EXPERTS_PER_GROUP):
            for b in range(a + 1, EXPERTS_PER_GROUP):
                pair = v[a] + v[b]
                best = pair if best is None else jnp.maximum(best, pair)
        gscore.append(best)
    bg = jnp.zeros_like(gscore[0], dtype=jnp.int32)
    bs = gscore[0]
    for g in range(1, N_EXPERT_GROUPS):
        upd = gscore[g] > bs
        bg = jnp.where(upd, g, bg)
        bs = jnp.where(upd, gscore[g], bs)

    def pick(vals, k):
        out = vals[k]
        for g in range(1, N_EXPERT_GROUPS):
            out = jnp.where(bg == g, vals[g * EXPERTS_PER_GROUP + k], out)
        return out

    v = [pick(rows, k) for k in range(EXPERTS_PER_GROUP)]
    r = [pick(raw, k) for k in range(EXPERTS_PER_GROUP)]
    i1 = jnp.zeros_like(bg)
    b1 = v[0]
    for k in range(1, EXPERTS_PER_GROUP):
        upd = v[k] > b1
        i1 = jnp.where(upd, k, i1)
        b1 = jnp.where(upd, v[k], b1)
    i2 = jnp.zeros_like(bg)
    b2 = jnp.full_like(b1, -jnp.inf)
    for k in range(EXPERTS_PER_GROUP):
        upd = (i1 != k) & (v[k] > b2)
        i2 = jnp.where(upd, k, i2)
        b2 = jnp.where(upd, v[k], b2)
    s1 = r[0]
    s2 = r[0]
    for k in range(1, EXPERTS_PER_GROUP):
        s1 = jnp.where(i1 == k, r[k], s1)
        s2 = jnp.where(i2 == k, r[k], s2)
    den = s1 + s2
    g1 = s1 / den
    g2 = s2 / den
    lo = jnp.minimum(i1, i2)
    hi = jnp.maximum(i1, i2)
    base = jnp.where(lo == 0, 0, jnp.where(lo == 1, 3, 5))
    cls_ref[...] = bg * N_PAIRS + base + hi - lo - 1
    first_lo = i1 < i2
    glo_ref[...] = jnp.where(first_lo, g1, g2)
    ghi_ref[...] = jnp.where(first_lo, g2, g1)


def _route(logits, router_b):
    t, ne = logits.shape
    tt = min(2048, t)
    col = lambda i: (0, i)
    cls, glo, ghi = pl.pallas_call(
        _route_body,
        grid=(t // tt,),
        in_specs=[pl.BlockSpec((ne, tt), col), pl.BlockSpec((ne, 1), lambda i: (0, 0))],
        out_specs=[pl.BlockSpec((1, tt), col)] * 3,
        out_shape=[jax.ShapeDtypeStruct((1, t), jnp.int32), jax.ShapeDtypeStruct((1, t), F32),
                   jax.ShapeDtypeStruct((1, t), F32)],
        compiler_params=_cp("parallel"),
        name="moe_route",
    )(logits.T, router_b.reshape(ne, 1).astype(F32))
    return cls[0], glo[0], ghi[0]


def _moe_body(e1_ref, e2_ref, valid_ref, h_ref, g_ref, wg1_ref, wg2_ref, wd1_ref, wd2_ref, o_ref):
    i = pl.program_id(0)

    @pl.when(valid_ref[i] == 1)
    def _():
        h = h_ref[...]
        g = g_ref[...]

        def expert(wg_ref, wd_ref):
            gu = _dot(h, wg_ref[0, 0])
            gt = gu[:, :D_EXPERT]
            act = (gt * jax.nn.sigmoid(gt) * gu[:, D_EXPERT:]).astype(BF16)
            return _dot(act, wd_ref[0, 0])

        y = g[:, 0:1] * expert(wg1_ref, wd1_ref) + g[:, 1:2] * expert(wg2_ref, wd2_ref)
        o_ref[...] = y.astype(BF16)

    @pl.when(valid_ref[i] == 0)
    def _():
        o_ref[...] = jnp.zeros_like(o_ref)


_PAIR_LO = (0, 0, 0, 1, 1, 2)
_PAIR_HI = (1, 2, 3, 2, 3, 3)


def _moe(h, cls, glo, ghi, wgu, wdn, layer):
    t, d = h.shape
    tm = MOE_TILE
    n_tiles = t // tm + N_CLASSES
    rows = n_tiles * tm
    onehot = (cls[:, None] == jnp.arange(N_CLASSES)[None, :]).astype(jnp.int32)
    csum = jnp.cumsum(onehot, 0)
    rank = jnp.take_along_axis(csum, cls[:, None], 1)[:, 0] - 1
    counts = csum[-1]
    padded = -(-counts // tm) * tm
    ends = jnp.cumsum(padded)
    dest = (ends - padded)[cls] + rank
    src = jnp.zeros((rows,), jnp.int32).at[dest].set(jnp.arange(t, dtype=jnp.int32))
    tile_start = jnp.arange(n_tiles, dtype=jnp.int32) * tm
    valid = (tile_start < ends[-1]).astype(jnp.int32)
    last_cls = jnp.max(jnp.where(counts > 0, jnp.arange(N_CLASSES), 0))
    tile_cls = jnp.minimum(jnp.searchsorted(ends, tile_start, side="right"), last_cls).astype(jnp.int32)
    grp = tile_cls // N_PAIRS
    pair = tile_cls % N_PAIRS
    e1 = (grp * EXPERTS_PER_GROUP + jnp.asarray(_PAIR_LO, jnp.int32)[pair]).astype(jnp.int32)
    e2 = (grp * EXPERTS_PER_GROUP + jnp.asarray(_PAIR_HI, jnp.int32)[pair]).astype(jnp.int32)
    hs = h.at[src].get(mode="promise_in_bounds")
    gs = jnp.stack([glo, ghi], 1).at[src].get(mode="promise_in_bounds")
    ys = pl.pallas_call(
        _moe_body,
        grid_spec=pltpu.PrefetchScalarGridSpec(
            num_scalar_prefetch=3,
            grid=(n_tiles,),
            in_specs=[
                pl.BlockSpec((tm, d), lambda i, e1, e2, va: (i, 0)),
                pl.BlockSpec((tm, 2), lambda i, e1, e2, va: (i, 0)),
                pl.BlockSpec((1, 1, d, 2 * D_EXPERT), lambda i, e1, e2, va: (layer, e1[i], 0, 0)),
                pl.BlockSpec((1, 1, d, 2 * D_EXPERT), lambda i, e1, e2, va: (layer, e2[i], 0, 0)),
                pl.BlockSpec((1, 1, D_EXPERT, d), lambda i, e1, e2, va: (layer, e1[i], 0, 0)),
                pl.BlockSpec((1, 1, D_EXPERT, d), lambda i, e1, e2, va: (layer, e2[i], 0, 0)),
            ],
            out_specs=pl.BlockSpec((tm, d), lambda i, e1, e2, va: (i, 0)),
        ),
        out_shape=jax.ShapeDtypeStruct((rows, d), BF16),
        compiler_params=_cp("arbitrary"),
        name="moe_experts",
    )(e1, e2, valid, hs, gs, wgu, wgu, wdn, wdn)
    return ys.at[dest].get(mode="promise_in_bounds")


def _resid_ln_body(x_ref, y_ref, mod_ref, g_ref, b_ref, o_ref):
    m = mod_ref[0]
    o_ref[...] = _layernorm(DN_ALPHA * x_ref[...] + (1.0 + m[5:6]) * y_ref[...].astype(F32),
                            g_ref[...], b_ref[...])


def _resid_ln(x2, y, mod, ln_g, ln_b, seq):
    t, d = x2.shape
    tm = 512
    per_b = seq // tm
    row = lambda i: (i, 0)
    const = lambda i: (0, 0)
    return pl.pallas_call(
        _resid_ln_body,
        grid=(t // tm,),
        in_specs=[pl.BlockSpec((tm, d), row), pl.BlockSpec((tm, d), row),
                  pl.BlockSpec((1, 6, d), lambda i: (i // per_b, 0, 0)),
                  pl.BlockSpec((1, d), const), pl.BlockSpec((1, d), const)],
        out_specs=pl.BlockSpec((tm, d), row),
        out_shape=jax.ShapeDtypeStruct((t, d), F32),
        compiler_params=_cp("parallel"),
        name="resid_ln",
    )(x2, y, mod, ln_g.reshape(1, d), ln_b.reshape(1, d))


def _front1_body(x_ref, mod_ref, w_ref, zc0_ref, zc1_ref, zc2_ref, zr_ref, h_scr, res_scr, *, dils, tm):
    j = pl.program_id(1)
    n_att = 3 * len(dils)

    @pl.when(j == 0)
    def _():
        m = mod_ref[0]
        h_scr[...] = (x_ref[...] * (1.0 + m[1:2]) + m[0:1]).astype(BF16)

    res = _dot(h_scr[...], w_ref[...])

    @pl.when(j >= n_att)
    def _():
        zr_ref[...] = res.astype(BF16)

    n_slab = res.shape[1] // LANES

    @pl.when(j < n_att)
    def _():
        for c in range(n_slab):
            res_scr[c] = res[:, c * LANES:(c + 1) * LANES]

    for g, (o_ref, dil) in enumerate(zip((zc0_ref, zc1_ref, zc2_ref), dils)):
        @pl.when(j // 3 == g)
        def _(o_ref=o_ref, dil=dil):
            rows = tm // dil
            for r in range(dil):
                for c in range(n_slab):
                    o_ref[0, r, :, c * LANES:(c + 1) * LANES] = (
                        res_scr[c, pl.ds(r, rows, stride=dil), :].astype(BF16))


def _front1(x2, mod, od_w_in, batch, seq):
    t, d = x2.shape
    tm = min(1024, seq)
    tn = DIL_WIDTH
    per_b = seq // tm
    n_in = od_w_in.shape[1]
    n = -(-n_in // tn) * tn
    w = jnp.pad(od_w_in, ((0, 0), (0, n - n_in))).astype(BF16)
    dils = tuple(dil for _, dil in DIL_PATTERNS)
    n_att = 3 * len(dils)
    n_r = n // tn - n_att

    def att_spec(g, dil):
        return pl.BlockSpec((1, dil, tm // dil, tn),
                            lambda i, j: (i // per_b, 0, i % per_b, jnp.clip(j - 3 * g, 0, 2)))

    return pl.pallas_call(
        functools.partial(_front1_body, dils=dils, tm=tm),
        grid=(t // tm, n // tn),
        in_specs=[pl.BlockSpec((tm, d), lambda i, j: (i, 0)),
                  pl.BlockSpec((1, 6, d), lambda i, j: (i // per_b, 0, 0)),
                  pl.BlockSpec((d, tn), lambda i, j: (0, j))],
        out_specs=[att_spec(g, dil) for g, dil in enumerate(dils)]
        + [pl.BlockSpec((tm, tn), lambda i, j: (i, jnp.clip(j - n_att, 0, n_r - 1)))],
        out_shape=[jax.ShapeDtypeStruct((batch, dil, seq // dil, 3 * tn), BF16) for dil in dils]
        + [jax.ShapeDtypeStruct((t, n_r * tn), BF16)],
        scratch_shapes=[pltpu.VMEM((tm, d), BF16), pltpu.VMEM((tn // LANES, tm, LANES), F32)],
        compiler_params=_cp("parallel", "arbitrary"),
        name="front1",
    )(x2, mod, w)


def _dil_body(q_ref, kp_ref, kc_ref, vp_ref, vc_ref, bias_ref, o_ref, lse_ref):
    n = pl.program_id(2)
    span = DIL_SPAN
    q = q_ref[0, 0]
    k = jnp.concatenate([kp_ref[0, 0], kc_ref[0, 0]], 0)
    v = jnp.concatenate([vp_ref[0, 0], vc_ref[0, 0]], 0)
    qi = lax.broadcasted_iota(jnp.int32, (span, 2 * span), 0)
    ku = lax.broadcasted_iota(jnp.int32, (span, 2 * span), 1)
    valid = (ku >= qi) & (ku <= qi + span) & ((n >= 1) | (ku >= span))
    lane = lax.broadcasted_iota(jnp.int32, (span, LANES), 1)
    first = lane < DIL_HEAD_DIM
    lse_all = jnp.zeros((span, LANES), F32)
    scale = DIL_HEAD_DIM ** -0.5
    for hp in range(DIL_HEADS // 2):
        sl = slice(hp * LANES, (hp + 1) * LANES)
        q2, k2, v2 = q[:, sl], k[:, sl], v[:, sl]
        outs = []
        for hh in range(2):
            head = 2 * hp + hh
            qm = jnp.where(first if hh == 0 else jnp.logical_not(first), q2, jnp.zeros_like(q2))
            s = _dot_nt(qm, k2) * scale + bias_ref[head]
            s = jnp.where(valid, s, NEG_INF)
            m = jnp.max(s, -1, keepdims=True)
            p = jnp.exp(s - m)
            l = jnp.sum(p, -1, keepdims=True)
            outs.append(_dot(p.astype(BF16), v2) / l)
            lse_all = jnp.where(lane == head, m + jnp.log(l), lse_all)
        o_ref[0, 0, :, sl] = jnp.where(first, outs[0], outs[1]).astype(BF16)
    lse_ref[0, 0] = lse_all


def _t5_bucket(dist):
    exact = T5_BUCKETS // 2
    logd = jnp.log(jnp.maximum(dist, 1).astype(F32) / exact) / math.log(T5_MAX_DIST / exact)
    large = jnp.minimum(exact + (logd * (T5_BUCKETS - exact)).astype(jnp.int32), T5_BUCKETS - 1)
    return jnp.where(dist < exact, dist, large)


def _dilated_group(zc, rel_bias, gi, dil, batch, seq):
    span = DIL_SPAN
    l_stream = seq // dil
    nb = l_stream // span
    qi = jnp.arange(span)[:, None]
    ku = jnp.arange(2 * span)[None, :]
    delta = jnp.clip(span + qi - ku, 0, span) * dil
    onehot = (_t5_bucket(delta)[None] == jnp.arange(T5_BUCKETS)[:, None, None]).astype(F32)
    table = rel_bias.astype(F32)[:, gi * DIL_HEADS:(gi + 1) * DIL_HEADS]
    bias = jnp.einsum("kh,kqu->hqu", table, onehot, precision=lax.Precision.HIGHEST)

    def cur(j):
        return lambda b, r, n: (b, r, n, j)

    def prev(j):
        return lambda b, r, n: (b, r, jnp.maximum(n - 1, 0), j)

    blk = (1, 1, span, DIL_WIDTH)
    return pl.pallas_call(
        _dil_body,
        grid=(batch, dil, nb),
        in_specs=[pl.BlockSpec(blk, cur(0)), pl.BlockSpec(blk, prev(1)), pl.BlockSpec(blk, cur(1)),
                  pl.BlockSpec(blk, prev(2)), pl.BlockSpec(blk, cur(2)),
                  pl.BlockSpec((DIL_HEADS, span, 2 * span), lambda b, r, n: (0, 0, 0))],
        out_specs=[pl.BlockSpec(blk, cur(0)), pl.BlockSpec((1, 1, span, LANES), cur(0))],
        out_shape=[jax.ShapeDtypeStruct((batch, dil, l_stream, DIL_WIDTH), BF16),
                   jax.ShapeDtypeStruct((batch, dil, l_stream, LANES), F32)],
        compiler_params=_cp("parallel", "parallel", "arbitrary"),
        name=f"dilated_attention_{dil}",
    )(zc, zc, zc, zc, zc, bias)


def _dil_merge_body(o1_ref, o2_ref, o3_ref, l1_ref, l2_ref, l3_ref, e_ref, out_ref, o_scr, l1_scr, l2_scr,
                    l3_scr, *, dils, tm):
    def unstream(ref, scr, dil):
        rows = tm // dil
        n_slab = scr.shape[0]
        for r in range(dil):
            x = ref[0, r].astype(F32)
            for c in range(n_slab):
                scr[c, pl.ds(r, rows, stride=dil), :] = x[:, c * LANES:(c + 1) * LANES]
        return jnp.concatenate([scr[c] for c in range(n_slab)], 1)

    lses = [unstream(ref, scr, dil) for ref, scr, dil in
            zip((l1_ref, l2_ref, l3_ref), (l1_scr, l2_scr, l3_scr), dils)]
    top = jnp.maximum(jnp.maximum(lses[0], lses[1]), lses[2])
    num = None
    den = None
    for o_ref, lse, dil in zip((o1_ref, o2_ref, o3_ref), lses, dils):
        w = _dot_exact_rhs(jnp.exp(lse - top), e_ref[...], 2)
        o = unstream(o_ref, o_scr, dil)
        num = w * o if num is None else num + w * o
        den = w if den is None else den + w
    out_ref[...] = (num / den).astype(BF16)


def _dil_merge(outs, lses, batch, seq):
    tm = 512
    per_b = seq // tm
    dils = tuple(dil for _, dil in DIL_PATTERNS)
    head_of_lane = jnp.arange(DIL_WIDTH) // DIL_HEAD_DIM
    expand = (jnp.arange(LANES)[:, None] == head_of_lane[None, :]).astype(BF16)

    def spec(dil, width):
        return pl.BlockSpec((1, dil, tm // dil, width), lambda i: (i // per_b, 0, i % per_b, 0))

    return pl.pallas_call(
        functools.partial(_dil_merge_body, dils=dils, tm=tm),
        grid=(batch * per_b,),
        in_specs=[spec(dil, DIL_WIDTH) for dil in dils] + [spec(dil, LANES) for dil in dils]
        + [pl.BlockSpec((LANES, DIL_WIDTH), lambda i: (0, 0))],
        out_specs=pl.BlockSpec((tm, DIL_WIDTH), lambda i: (i, 0)),
        out_shape=jax.ShapeDtypeStruct((batch * seq, DIL_WIDTH), BF16),
        scratch_shapes=[pltpu.VMEM((DIL_WIDTH // LANES, tm, LANES), F32)] + [pltpu.VMEM((1, tm, LANES), F32)] * 3,
        compiler_params=_cp("parallel"),
        name="dilated_merge",
    )(*outs, *lses, expand)


def _rwkv_body(zr_ref, zk_ref, zv_ref, zm_ref, mur_ref, muk_ref, muv_ref, mum_ref, w0_ref, w2_ref,
               a0_ref, a2_ref, g2_ref, kk_ref, ka_ref, rk_ref, lng_ref, lnb_ref, o_ref,
               st_ref, pr_ref, pk_ref, pv_ref, pm_ref, xr_s, xk_s, xv_s, lw_s, as_s, g_s, cum_s):
    lc = RW_CHUNK
    n_slabs = st_ref.shape[0]

    @pl.when(pl.program_id(1) == 0)
    def _():
        st_ref[...] = jnp.zeros_like(st_ref)
        pr_ref[...] = jnp.zeros_like(pr_ref)
        pk_ref[...] = jnp.zeros_like(pk_ref)
        pv_ref[...] = jnp.zeros_like(pv_ref)
        pm_ref[...] = jnp.zeros_like(pm_ref)

    def shifted(z_ref, prev_ref, mu_ref):
        z = z_ref[...].astype(F32)
        first_row = lax.broadcasted_iota(jnp.int32, z.shape, 0) == 0
        prev = jnp.where(first_row, prev_ref[...], pltpu.roll(z, 1, 0))
        prev_ref[...] = z[lc - 1:lc, :]
        return z + (prev - z) * mu_ref[...]

    xr_s[...] = shifted(zr_ref, pr_ref, mur_ref)
    xk_s[...] = shifted(zk_ref, pk_ref, muk_ref)
    xv_s[...] = shifted(zv_ref, pv_ref, muv_ref)
    xm = shifted(zm_ref, pm_ref, mum_ref)
    lora_in = xm[:, :LANES]
    wl = w0_ref[...] + _dot(jnp.tanh(lora_in).astype(BF16), w2_ref[...])
    w_log = -(jnp.maximum(-wl, 0.0) + jnp.log(1.0 + jnp.exp(-jnp.abs(wl)))) - 0.5
    lw = -jnp.exp(w_log)
    lw_s[...] = lw
    as_s[...] = jax.nn.sigmoid(a0_ref[...] + _dot(lora_in.astype(BF16), a2_ref[...]))
    g_s[...] = _dot(jax.nn.sigmoid(xm[:, LANES:]).astype(BF16), g2_ref[...])
    ti = lax.broadcasted_iota(jnp.int32, (lc, lc), 0)
    si = lax.broadcasted_iota(jnp.int32, (lc, lc), 1)
    cum_s[...] = _dot_exact_lhs((si <= ti).astype(BF16), lw, 3)

    row = lax.broadcasted_iota(jnp.int32, (2 * lc, 2 * lc), 0)
    colm = lax.broadcasted_iota(jnp.int32, (2 * lc, 2 * lc), 1)
    same_head = (row >= lc) == (colm >= lc)
    t_in = row & (lc - 1)
    s_in = colm & (lc - 1)
    strict = same_head & (s_in < t_in)
    incl = same_head & (s_in <= t_in)
    eye = row == colm
    ones_bd = same_head.astype(BF16)
    lane = lax.broadcasted_iota(jnp.int32, (lc, LANES), 1)
    first = lane < RW_HEAD

    def seg(x):
        return _dot_exact_rhs(x, ones_bd, 2)

    def stack_heads(x):
        return jnp.concatenate([jnp.where(first, x, 0.0), jnp.where(first, 0.0, x)], 0)

    slabs = range(n_slabs)
    sls = [slice(hp * LANES, (hp + 1) * LANES) for hp in slabs]
    kk_raw = [xk_s[:, sl] * kk_ref[:, sl] for sl in sls]
    kk_ss = [seg(kk * kk) for kk in kk_raw]
    lhs, rhs, bk_t, v_bs, k2s, pl_cols = [], [], [], [], [], []
    for hp, sl in enumerate(sls):
        a_sig = as_s[:, sl]
        cum = cum_s[:, sl]
        kk = kk_raw[hp] / jnp.maximum(jnp.sqrt(kk_ss[hp]), 1e-12)
        k2 = xk_s[:, sl] * (1.0 + (a_sig - 1.0) * ka_ref[:, sl])
        b_p = kk * a_sig
        cl = cum[lc - 1:lc, :]
        e_neg = jnp.exp(-cum)
        e_end = jnp.exp(cl - cum)
        at = -kk * jnp.exp(cum - lw_s[:, sl])
        rt = xr_s[:, sl] * jnp.exp(cum)
        lhs.append(jnp.concatenate([stack_heads(at), stack_heads(rt)], 0).astype(BF16))
        bt = (b_p * e_neg).astype(BF16)
        kt = (k2 * e_neg).astype(BF16)
        rhs.append(jnp.concatenate([bt, bt, kt, kt], 0))
        bk_t.append(jnp.concatenate([b_p * e_end, k2 * e_end], 0).T.astype(BF16))
        v_bs.append(xv_s[:, sl].astype(BF16))
        k2s.append(k2)
        pl_cols.append(jnp.sum(jnp.where(eye, jnp.broadcast_to(jnp.exp(cl), (2 * lc, LANES)), 0.0), 1,
                               keepdims=True))
    qms = [_dot_nt(lhs[hp], rhs[hp]) for hp in slabs]
    states = [st_ref[hp] for hp in slabs]
    arss = [_dot(lhs[hp], states[hp].astype(BF16)) for hp in slabs]
    a_abs = [jnp.where(strict, qm[:2 * lc, :2 * lc], 0.0).astype(BF16) for qm in qms]
    xs = [jnp.where(eye, 1.0, 0.0) + a.astype(F32) for a in a_abs]
    ps = [_dot(a, a).astype(BF16) for a in a_abs]
    n_double = int(math.log2(lc)) - 1
    for it in range(n_double - 1):
        res = [_dot(ps[hp], jnp.concatenate([xs[hp].astype(BF16), ps[hp]], 1)) for hp in slabs]
        xs = [xs[hp] + res[hp][:, :2 * lc] for hp in slabs]
        ps = [res[hp][:, 2 * lc:].astype(BF16) for hp in slabs]
    xs = [xs[hp] + _dot(ps[hp], xs[hp].astype(BF16)) for hp in slabs]
    vss = [jnp.concatenate([v_b, v_b], 0) for v_b in v_bs]
    gmats = [_dot(jnp.where(strict, qms[hp][:2 * lc, 2 * lc:], 0.0).astype(BF16), vss[hp]) + arss[hp][:2 * lc]
             for hp in slabs]
    uss = [_dot(xs[hp].astype(BF16), gmats[hp].astype(BF16)) for hp in slabs]
    incl2 = jnp.concatenate([incl, incl], 1)
    yss = [_dot(jnp.where(incl2, qms[hp][2 * lc:, :], 0.0).astype(BF16),
                jnp.concatenate([uss[hp].astype(BF16), vss[hp]], 0)) + arss[hp][2 * lc:] for hp in slabs]
    us_p = [jnp.where(first, us[:lc], us[lc:]) for us in uss]
    ys_p = [jnp.where(first, ys[:lc], ys[lc:]) for ys in yss]
    upds = [_dot(bk_t[hp], jnp.concatenate([us_p[hp].astype(BF16), v_bs[hp]], 0)) for hp in slabs]
    for hp in slabs:
        st_ref[hp] = pl_cols[hp] * states[hp] + jnp.where(same_head, upds[hp], 0.0)
    means = [seg(y) * (1.0 / RW_HEAD) for y in ys_p]
    dys = [ys_p[hp] - means[hp] for hp in slabs]
    variances = [seg(dy * dy) * (1.0 / RW_HEAD) for dy in dys]
    bonus = [seg(xr_s[:, sl] * k2s[hp] * rk_ref[:, sl]) for hp, sl in enumerate(sls)]
    for hp, sl in enumerate(sls):
        yn = dys[hp] * lax.rsqrt(variances[hp] + RW_GN_EPS) * lng_ref[:, sl] + lnb_ref[:, sl]
        o_ref[:, sl] = ((yn + bonus[hp] * xv_s[:, sl]) * g_s[:, sl]).astype(BF16)


def _rwkv(z, batch, seq, mu, w0, w2, a0, a2, g2, k_k, k_a, r_k, lnx_g, lnx_b):
    t = z.shape[0]
    lc = RW_CHUNK
    nc = seq // lc
    width = w0.shape[0]
    n_slabs = width // LANES
    col0 = 0
    misc_blk = 3 * width // RW_MISC
    n_lora = RW_LORA_W + RW_LORA_A + RW_LORA_G
    mu_m = jnp.pad(mu[3 * width:], (0, RW_MISC - n_lora)).reshape(1, RW_MISC)
    w2p = jnp.pad(w2, ((0, LANES - RW_LORA_W), (0, 0))).astype(BF16)
    a2p = jnp.pad(a2, ((RW_LORA_W, LANES - RW_LORA_W - RW_LORA_A), (0, 0))).astype(BF16)
    g2p = jnp.pad(g2, ((0, RW_MISC - LANES - RW_LORA_G), (0, 0))).astype(BF16)
    vec = lambda a: a.reshape(1, width).astype(F32)
    const = lambda b, c: (0, 0)
    wide = lambda: pl.BlockSpec((1, width), const)
    return pl.pallas_call(
        _rwkv_body,
        grid=(batch, nc),
        in_specs=[
            pl.BlockSpec((lc, width), lambda b, c: (b * nc + c, col0)),
            pl.BlockSpec((lc, width), lambda b, c: (b * nc + c, col0 + 1)),
            pl.BlockSpec((lc, width), lambda b, c: (b * nc + c, col0 + 2)),
            pl.BlockSpec((lc, RW_MISC), lambda b, c: (b * nc + c, misc_blk)),
            wide(), wide(), wide(), pl.BlockSpec((1, RW_MISC), const),
            wide(), pl.BlockSpec((LANES, width), const),
            wide(), pl.BlockSpec((LANES, width), const),
            pl.BlockSpec((RW_MISC - LANES, width), const),
            wide(), wide(), wide(), wide(), wide(),
        ],
        out_specs=pl.BlockSpec((lc, width), lambda b, c: (b * nc + c, 0)),
        out_shape=jax.ShapeDtypeStruct((t, width), BF16),
        scratch_shapes=[pltpu.VMEM((n_slabs, LANES, LANES), F32),
                        pltpu.VMEM((1, width), F32), pltpu.VMEM((1, width), F32),
                        pltpu.VMEM((1, width), F32), pltpu.VMEM((1, RW_MISC), F32)]
        + [pltpu.VMEM((lc, width), F32)] * 7,
        compiler_params=_cp("parallel", "arbitrary"),
        name="rwkv7",
    )(z, z, z, z, vec(mu[:width]), vec(mu[width:2 * width]), vec(mu[2 * width:3 * width]), mu_m,
      vec(w0), w2p, vec(a0), a2p, g2p, vec(k_k), vec(k_a), vec(r_k), vec(lnx_g), vec(lnx_b))


def kernel(x, c, ada_w, ada_b, ln_mix_g, ln_mix_b, ln_ffn_g, ln_ffn_b, router_w, router_b, moe_w_gate_up, moe_w_down, rel_bias, ev_w_in, mla_q_norm, mla_w_uq, mla_kv_norm, mla_w_ukv, s5_lambda_re, s5_lambda_im, s5_b_re, s5_b_im, s5_c_re, s5_c_im, s5_d, s5_log_dt, s5_w_glu, ev_w_out, od_w_in, rw_mu, rw_w0, rw_w2, rw_a0, rw_a2, rw_g2, rw_k_k, rw_k_a, rw_r_k, rw_lnx_g, rw_lnx_b, od_w_out):
    batch, seq, d = x.shape
    assert seq % 2048 == 0 and d % LANES == 0
    x2 = x.reshape(batch * seq, d)
    mods = _ada(c, ada_w, ada_b)
    wgu_all = _cast_bf16(moe_w_gate_up, moe_w_gate_up.shape[-2] // 4)
    wdn_all = _cast_bf16(moe_w_down, 2 * moe_w_down.shape[-2])
    for layer in range(DEPTH):
        mod = mods[layer]
        if layer % 2 == 0:
            e = layer // 2
            q, k, v, u = _front0(x2, mod, seq, ev_w_in[e], mla_q_norm[e], mla_w_uq[e], mla_kv_norm[e],
                                 mla_w_ukv[e])
            att = _flash(q, k, v, batch, seq)
            n_steps = max(1, math.ceil(math.log2(seq // S5_CHUNK)))
            mats = _s5_matrices(s5_lambda_re[e], s5_lambda_im[e], s5_b_re[e], s5_b_im[e], s5_c_re[e],
                                s5_c_im[e], s5_log_dt[e], n_steps)
            ssm = _s5_glu(_s5_scan(u, batch, seq, mats), u, s5_d[e], s5_w_glu[e])
            x2, h, logits = _outproj(att, ssm, ev_w_out[e], x2, mod, ln_mix_g[layer], ln_mix_b[layer],
                                     router_w, seq)
        else:
            o = layer // 2
            *zcs, z = _front1(x2, mod, od_w_in[o], batch, seq)
            outs, lses = [], []
            for gi, (win, dil) in enumerate(DIL_PATTERNS):
                assert win // dil == DIL_SPAN
                og, lg = _dilated_group(zcs[gi], rel_bias, gi, dil, batch, seq)
                outs.append(og)
                lses.append(lg)
            att = _dil_merge(outs, lses, batch, seq)
            tm_out = _rwkv(z, batch, seq, rw_mu[o], rw_w0[o], rw_w2[o], rw_a0[o], rw_a2[o], rw_g2[o],
                           rw_k_k[o], rw_k_a[o], rw_r_k[o], rw_lnx_g[o], rw_lnx_b[o])
            x2, h, logits = _outproj(att, tm_out, od_w_out[o], x2, mod, ln_mix_g[layer], ln_mix_b[layer],
                                     router_w, seq)
        cls, glo, ghi = _route(logits, router_b)
        y = _moe(h, cls, glo, ghi, wgu_all, wdn_all, layer)
        x2 = _resid_ln(x2, y, mod, ln_ffn_g[layer], ln_ffn_b[layer], seq)
    return x2.reshape(batch, seq, d)
```

```python
import functools
import math

import jax
import jax.numpy as jnp
from jax import lax
from jax.experimental import pallas as pl
from jax.experimental.pallas import tpu as pltpu

F32 = jnp.float32
BF16 = jnp.bfloat16

DEPTH = 2
DN_ALPHA = (2.0 * DEPTH) ** 0.25
LN_EPS = 1e-5
RMS_EPS = 1e-6
NEG_INF = -1e30

MLA_HEADS = 8
MLA_NOPE = 128
MLA_ROPE = 64
MLA_V = 128
MLA_QK = MLA_NOPE + MLA_ROPE
MLA_Q_RANK = 512
MLA_KV_RANK = 256
ROPE_THETA = 10000.0
MLA_HEAD_PAD = 256

S5_GROUP = 16
S5_STATE = 64
S5_CHUNK = 16

DIL_PATTERNS = ((128, 1), (512, 4), (2048, 16))
DIL_SPAN = 128
DIL_HEADS = 8
DIL_HEAD_DIM = 64
DIL_WIDTH = DIL_HEADS * DIL_HEAD_DIM
DIL_IN = len(DIL_PATTERNS) * 3 * DIL_WIDTH
T5_BUCKETS = 32
T5_MAX_DIST = 2048

RW_HEAD = 64
RW_LORA_W = 64
RW_LORA_A = 64
RW_LORA_G = 224
RW_GN_EPS = 64e-5
RW_CHUNK = 64
RW_MISC = 512

N_EXPERTS = 16
N_EXPERT_GROUPS = 4
EXPERTS_PER_GROUP = 4
D_EXPERT = 512
N_PAIRS = 6
N_CLASSES = N_EXPERT_GROUPS * N_PAIRS
MOE_TILE = 256

LANES = 128
VMEM_LIMIT = 56 * 1024 * 1024


def _cp(*sem):
    return pltpu.CompilerParams(dimension_semantics=sem, vmem_limit_bytes=VMEM_LIMIT)


def _dot(a, b):
    return jnp.dot(a, b, preferred_element_type=F32)


def _dot_nt(a, b):
    return lax.dot_general(a, b, (((1,), (1,)), ((), ())), preferred_element_type=F32)


def _split_bf16(x, parts):
    out = []
    for _ in range(parts):
        hi = x.astype(BF16)
        out.append(hi)
        x = x - hi.astype(F32)
    return out


def _dot_exact_rhs(x, w_bf16, parts=3):
    acc = None
    for p in _split_bf16(x, parts):
        t = _dot(p, w_bf16)
        acc = t if acc is None else acc + t
    return acc


def _dot_exact_lhs(w_bf16, x, parts=3):
    acc = None
    for p in _split_bf16(x, parts):
        t = _dot(w_bf16, p)
        acc = t if acc is None else acc + t
    return acc


def _layernorm(x, g, b):
    mu = jnp.mean(x, -1, keepdims=True)
    d = x - mu
    var = jnp.mean(d * d, -1, keepdims=True)
    return d * lax.rsqrt(var + LN_EPS) * g + b


def _cast_body(*refs):
    *x_refs, o_ref = refs
    sub = x_refs[0].shape[0]
    for k, x_ref in enumerate(x_refs):
        o_ref[k * sub:(k + 1) * sub, :] = x_ref[...].astype(BF16)


def _cast_bf16(w, block_rows, streams=4):
    w2 = w.reshape(-1, w.shape[-1])
    rows, cols = w2.shape
    sub = block_rows // streams
    in_specs = [pl.BlockSpec((sub, cols), lambda i, k=k: (i * streams + k, 0)) for k in range(streams)]
    out = pl.pallas_call(
        _cast_body, grid=(rows // block_rows,), in_specs=in_specs,
        out_specs=pl.BlockSpec((block_rows, cols), lambda i: (i, 0)),
        out_shape=jax.ShapeDtypeStruct((rows, cols), BF16), compiler_params=_cp("parallel"), name="cast_bf16",
    )(*([w2] * streams))
    return out.reshape(w.shape)


def _ada_body(c_ref, w_ref, b_ref, o_ref):
    c = c_ref[...]
    cond = c * jax.nn.sigmoid(c)
    o_ref[0] = _dot_exact_rhs(cond, w_ref[0].astype(BF16), 2) + b_ref[0]


def _ada(c, ada_w, ada_b):
    depth, d, n = ada_w.shape
    b = c.shape[0]
    rows = -(-b // 8) * 8
    cp = jnp.pad(c, ((0, rows - b), (0, 0)))
    tn = 1024
    out = pl.pallas_call(
        _ada_body,
        grid=(depth, n // tn),
        in_specs=[
            pl.BlockSpec((rows, d), lambda l, j: (0, 0)),
            pl.BlockSpec((1, d, tn), lambda l, j: (l, 0, j)),
            pl.BlockSpec((1, 1, tn), lambda l, j: (l, 0, j)),
        ],
        out_specs=pl.BlockSpec((1, rows, tn), lambda l, j: (l, 0, j)),
        out_shape=jax.ShapeDtypeStruct((depth, rows, n), F32),
        compiler_params=_cp("parallel", "parallel"),
        name="ada",
    )(cp, ada_w, ada_b.reshape(depth, 1, n))
    return out[:, :b].reshape(depth, b, 6, d)


def _front0_body(x_ref, mod_ref, cs_ref, win_ref, qn_ref, kvn_ref, wq1_ref, wq2_ref, wkn_ref,
                 wv_ref, q_ref, k_ref, v_ref, u_ref):
    m = mod_ref[0]
    h = (x_ref[...] * (1.0 + m[1:2]) + m[0:1]).astype(BF16)
    z = _dot(h, win_ref[...])
    cs = cs_ref[...]
    c2 = cs[:, :LANES]
    s2 = cs[:, LANES:]
    q_c = z[:, :MLA_Q_RANK]
    kv_c = z[:, MLA_Q_RANK:MLA_Q_RANK + MLA_KV_RANK]
    o = MLA_Q_RANK + MLA_KV_RANK
    krope = (z[:, o:o + LANES] * c2 + z[:, o + LANES:o + 2 * LANES] * s2).astype(BF16)
    u_ref[...] = z[:, o + 2 * LANES:]
    hq = (q_c * lax.rsqrt(jnp.mean(q_c * q_c, -1, keepdims=True) + RMS_EPS) * qn_ref[...]).astype(BF16)
    hkv = (kv_c * lax.rsqrt(jnp.mean(kv_c * kv_c, -1, keepdims=True) + RMS_EPS) * kvn_ref[...]).astype(BF16)
    full = _dot(hq, wq1_ref[...])
    sw = _dot(hq, wq2_ref[...])
    kn = _dot(hkv, wkn_ref[...])
    v_ref[...] = _dot(hkv, wv_ref[...]).astype(BF16)
    for j in range(MLA_HEADS):
        a = j * MLA_HEAD_PAD
        q_ref[:, a:a + LANES] = full[:, a:a + LANES].astype(BF16)
        q_ref[:, a + LANES:a + 2 * LANES] = (
            full[:, a + LANES:a + 2 * LANES] * c2 + sw[:, j * LANES:(j + 1) * LANES] * s2).astype(BF16)
        k_ref[:, a:a + LANES] = kn[:, j * LANES:(j + 1) * LANES].astype(BF16)
        k_ref[:, a + LANES:a + 2 * LANES] = krope


def _rope_swapped(w):
    half = w.shape[-1] // 2
    return jnp.concatenate([-w[..., half:], w[..., :half]], -1)


def _front0(x2, mod, seq, ev_w_in, q_norm, w_uq, kv_norm, w_ukv):
    t, d = x2.shape
    tm = 256
    per_b = seq // tm
    zpad = jnp.zeros((d, LANES - MLA_ROPE), F32)
    o = MLA_Q_RANK + MLA_KV_RANK
    w_kr = ev_w_in[:, o:o + MLA_ROPE]
    win = jnp.concatenate([ev_w_in[:, :o], w_kr, zpad, _rope_swapped(w_kr), zpad,
                           ev_w_in[:, o + MLA_ROPE:]], 1).astype(BF16)
    scale = MLA_QK ** -0.5
    wq = w_uq.reshape(MLA_Q_RANK, MLA_HEADS, MLA_QK) * scale
    zq = jnp.zeros((MLA_Q_RANK, MLA_HEADS, MLA_HEAD_PAD - MLA_QK), F32)
    wq1 = jnp.concatenate([wq, zq], -1).reshape(MLA_Q_RANK, -1).astype(BF16)
    wq2 = jnp.concatenate([_rope_swapped(wq[..., MLA_NOPE:]), zq], -1).reshape(MLA_Q_RANK, -1).astype(BF16)
    wkv = w_ukv.reshape(MLA_KV_RANK, MLA_HEADS, MLA_NOPE + MLA_V)
    wkn = wkv[..., :MLA_NOPE].reshape(MLA_KV_RANK, -1).astype(BF16)
    wv = wkv[..., MLA_NOPE:].reshape(MLA_KV_RANK, -1).astype(BF16)
    inv = ROPE_THETA ** (-jnp.arange(0, MLA_ROPE, 2, dtype=F32) / MLA_ROPE)
    ang = jnp.arange(seq, dtype=F32)[:, None] * inv[None]
    zt = jnp.zeros((seq, LANES - MLA_ROPE), F32)
    cs = jnp.concatenate([jnp.cos(ang), jnp.cos(ang), zt, jnp.sin(ang), jnp.sin(ang), zt], 1)
    const = lambda i: (0, 0)
    hq_w = MLA_HEADS * MLA_HEAD_PAD
    hv_w = MLA_HEADS * MLA_V
    s5w = ev_w_in.shape[1] - o - MLA_ROPE
    return pl.pallas_call(
        _front0_body,
        grid=(t // tm,),
        in_specs=[
            pl.BlockSpec((tm, d), lambda i: (i, 0)),
            pl.BlockSpec((1, 6, d), lambda i: (i // per_b, 0, 0)),
            pl.BlockSpec((tm, 2 * LANES), lambda i: (i % per_b, 0)),
            pl.BlockSpec(win.shape, const),
            pl.BlockSpec((1, MLA_Q_RANK), const),
            pl.BlockSpec((1, MLA_KV_RANK), const),
            pl.BlockSpec(wq1.shape, const),
            pl.BlockSpec(wq2.shape, const),
            pl.BlockSpec(wkn.shape, const),
            pl.BlockSpec(wv.shape, const),
        ],
        out_specs=[
            pl.BlockSpec((tm, hq_w), lambda i: (i, 0)),
            pl.BlockSpec((tm, hq_w), lambda i: (i, 0)),
            pl.BlockSpec((tm, hv_w), lambda i: (i, 0)),
            pl.BlockSpec((tm, s5w), lambda i: (i, 0)),
        ],
        out_shape=[
            jax.ShapeDtypeStruct((t, hq_w), BF16),
            jax.ShapeDtypeStruct((t, hq_w), BF16),
            jax.ShapeDtypeStruct((t, hv_w), BF16),
            jax.ShapeDtypeStruct((t, s5w), F32),
        ],
        compiler_params=_cp("parallel"),
        name="front0",
    )(x2, mod, cs, win, q_norm.reshape(1, -1), kv_norm.reshape(1, -1), wq1, wq2, wkn, wv)


def _flash_body(q_ref, k_ref, v_ref, o_ref, *, tile, heads):
    i = pl.program_id(2)
    qs = [q_ref[:, h * MLA_HEAD_PAD:(h + 1) * MLA_HEAD_PAD] for h in range(heads)]

    def one_head(h, start, carry, diagonal):
        m, l, acc = carry
        k = k_ref[pl.ds(start, tile), h * MLA_HEAD_PAD:(h + 1) * MLA_HEAD_PAD]
        v = v_ref[pl.ds(start, tile), h * MLA_V:(h + 1) * MLA_V]
        s = _dot_nt(qs[h], k)
        if diagonal:
            row = lax.broadcasted_iota(jnp.int32, s.shape, 0)
            col = lax.broadcasted_iota(jnp.int32, s.shape, 1)
            s = jnp.where(col <= row, s, NEG_INF)
        m_new = jnp.maximum(m, jnp.max(s, -1, keepdims=True))
        alpha = jnp.exp(m - m_new)
        p = jnp.exp(s - m_new)
        l = l * alpha + jnp.sum(p, -1, keepdims=True)
        acc = acc * alpha + _dot(p.astype(BF16), v)
        return m_new, l, acc

    def step(j, carry, diagonal):
        start = pl.multiple_of(j * tile, tile)
        return tuple(one_head(h, start, carry[h], diagonal) for h in range(heads))

    init = tuple((jnp.full((tile, 1), NEG_INF, F32), jnp.zeros((tile, 1), F32),
                  jnp.zeros((tile, MLA_V), F32)) for _ in range(heads))
    carry = lax.fori_loop(0, i, lambda j, c: step(j, c, False), init)
    carry = step(i, carry, True)
    for h, (_, l, acc) in enumerate(carry):
        o_ref[:, h * MLA_V:(h + 1) * MLA_V] = (acc / l).astype(BF16)


def _flash(q, k, v, batch, seq):
    t = q.shape[0]
    tile = 512
    heads = 2
    nq = seq // tile
    return pl.pallas_call(
        functools.partial(_flash_body, tile=tile, heads=heads),
        grid=(batch, MLA_HEADS // heads, nq),
        in_specs=[
            pl.BlockSpec((tile, heads * MLA_HEAD_PAD), lambda b, h, i: (b * nq + i, h)),
            pl.BlockSpec((seq, heads * MLA_HEAD_PAD), lambda b, h, i: (b, h)),
            pl.BlockSpec((seq, heads * MLA_V), lambda b, h, i: (b, h)),
        ],
        out_specs=pl.BlockSpec((tile, heads * MLA_V), lambda b, h, i: (b * nq + i, h)),
        out_shape=jax.ShapeDtypeStruct((t, MLA_HEADS * MLA_V), BF16),
        compiler_params=_cp("parallel", "parallel", "arbitrary"),
        name="mla_attention",
    )(q, k, v)


S5_GROUPS_PER_BLOCK = LANES // S5_GROUP


def _s5_body(u_ref, w_ref, v_ref, bd_ref, ac_ref, bc_ref, y_ref, ucat, yx, *, n_chunks, n_steps):
    lc = S5_CHUNK
    for t in range(lc):
        ucat[:, t * LANES:(t + 1) * LANES] = u_ref[pl.ds(t, n_chunks, stride=lc), :].astype(BF16)
    z = _dot(ucat[...], w_ref[0])
    half = z.shape[1] // 2
    cidx = lax.broadcasted_iota(jnp.int32, z.shape, 0)
    for step in range(n_steps):
        d = 1 << step
        zs = jnp.where(cidx >= d, pltpu.roll(z, d, 0), 0.0)
        z = z + zs * ac_ref[0, step:step + 1, :] + pltpu.roll(zs, half, 1) * bc_ref[0, step:step + 1, :]
    xprev = jnp.where(cidx >= 1, pltpu.roll(z, 1, 0), 0.0)
    yx[...] = _dot(xprev.astype(BF16), v_ref[0])
    for t in range(lc):
        k = (t + 1) * LANES
        y = _dot(ucat[:, :k], bd_ref[0, (lc - 1 - t) * LANES:, :]) + yx[:, t * LANES:(t + 1) * LANES]
        y_ref[pl.ds(t, n_chunks, stride=lc), :] = y


def _s5_matrices(lam_re, lam_im, b_re, b_im, c_re, c_im, log_dt, n_steps):
    lc = S5_CHUNK
    gb = S5_GROUPS_PER_BLOCK
    lam = lax.complex(lam_re.astype(F32), lam_im.astype(F32))
    n_blk = lam.shape[0] // gb
    dt = jnp.exp(log_dt.astype(F32))[:, None]
    lam_bar = jnp.exp(lam * dt)
    b_bar = ((lam_bar - 1.0) / lam)[..., None] * lax.complex(b_re.astype(F32), b_im.astype(F32))
    c = lax.complex(c_re.astype(F32), c_im.astype(F32))
    j = jnp.arange(lc + 1, dtype=F32)[:, None, None]
    pows = jnp.exp(j * (lam * dt)[None])
    eye = jnp.eye(gb, dtype=F32)
    kern = jnp.einsum("gop,jgp,gpi->jgio", c, pows[:lc], b_bar).real
    kern = kern.reshape(lc, n_blk, gb, S5_GROUP, S5_GROUP)
    bd = jnp.einsum("jbgio,gh->bjgiho", kern, eye).reshape(n_blk, lc, LANES, LANES)
    bdrev = bd[:, ::-1].reshape(n_blk, lc * LANES, LANES)
    tt = jnp.arange(lc)
    wm = pows[lc - 1 - tt][..., None] * b_bar[None]
    wm = jnp.stack([wm.real, wm.imag], 2)
    wm = wm.reshape(lc, n_blk, gb, 2, S5_STATE, S5_GROUP)
    wcat = jnp.einsum("tbgrpi,gh->btgirhp", wm, eye).reshape(n_blk, lc * LANES, 2 * gb * S5_STATE)
    cv = c[None] * pows[1:lc + 1][:, :, None, :]
    cv = jnp.stack([cv.real, -cv.imag], 3)
    cv = cv.reshape(lc, n_blk, gb, S5_GROUP, 2, S5_STATE)
    vcat = jnp.einsum("tbgorp,gh->brgptho", cv, eye).reshape(n_blk, 2 * gb * S5_STATE, lc * LANES)
    stride = (lc * (2.0 ** jnp.arange(n_steps, dtype=F32)))[None, :, None]
    ap = jnp.exp(stride * (lam * dt)[:, None, :])

    def per_block(a):
        return a.reshape(n_blk, gb, n_steps, S5_STATE).transpose(0, 2, 1, 3).reshape(n_blk, n_steps, -1)

    ar, ai = per_block(ap.real), per_block(ap.imag)
    ac = jnp.concatenate([ar, ar], -1)
    bc = jnp.concatenate([-ai, ai], -1)
    return wcat.astype(BF16), vcat.astype(BF16), bdrev.astype(BF16), ac, bc


def _s5_scan(u, batch, seq, mats):
    wcat, vcat, bdrev, ac, bc = mats
    n_blk = wcat.shape[0]
    lc = S5_CHUNK
    nc = seq // lc
    n_steps = ac.shape[1]
    wide = wcat.shape[2]
    per_blk = lambda i, b: (i, 0, 0)
    return pl.pallas_call(
        functools.partial(_s5_body, n_chunks=nc, n_steps=n_steps),
        grid=(n_blk, batch),
        in_specs=[
            pl.BlockSpec((seq, LANES), lambda i, b: (b, i)),
            pl.BlockSpec((1, lc * LANES, wide), per_blk),
            pl.BlockSpec((1, wide, lc * LANES), per_blk),
            pl.BlockSpec((1, lc * LANES, LANES), per_blk),
            pl.BlockSpec((1, n_steps, wide), per_blk),
            pl.BlockSpec((1, n_steps, wide), per_blk),
        ],
        out_specs=pl.BlockSpec((seq, LANES), lambda i, b: (b, i)),
        out_shape=jax.ShapeDtypeStruct(u.shape, F32),
        scratch_shapes=[pltpu.VMEM((nc, lc * LANES), BF16), pltpu.VMEM((nc, lc * LANES), F32)],
        compiler_params=_cp("parallel", "arbitrary"),
        name="s5_scan",
    )(u, wcat, vcat, bdrev, ac, bc)


def _s5_glu_body(y_ref, u_ref, d_ref, w_ref, o_ref):
    y = y_ref[...] + d_ref[...] * u_ref[...]
    y = 0.5 * y * (1.0 + jnp.tanh(math.sqrt(2.0 / math.pi) * (y + 0.044715 * (y * y * y))))
    gate = jax.nn.sigmoid(_dot(y.astype(BF16), w_ref[...]))
    o_ref[...] = (y * gate).astype(BF16)


def _s5_glu(y, u, d_skip, w_glu):
    t, n = y.shape
    tm = 512
    row = lambda i: (i, 0)
    const = lambda i: (0, 0)
    return pl.pallas_call(
        _s5_glu_body,
        grid=(t // tm,),
        in_specs=[pl.BlockSpec((tm, n), row), pl.BlockSpec((tm, n), row),
                  pl.BlockSpec((1, n), const), pl.BlockSpec((n, n), const)],
        out_specs=pl.BlockSpec((tm, n), row),
        out_shape=jax.ShapeDtypeStruct((t, n), BF16),
        compiler_params=_cp("parallel"),
        name="s5_glu",
    )(y, u, d_skip.reshape(1, n), w_glu.astype(BF16))


def _outproj_body(a1_ref, a2_ref, w1_ref, w2_ref, x_ref, mod_ref, g_ref, b_ref, rwh_ref, rwl_ref,
                  xo_ref, h_ref, lg_ref):
    y = _dot(a1_ref[...], w1_ref[...]) + _dot(a2_ref[...], w2_ref[...])
    m = mod_ref[0]
    xn = _layernorm(DN_ALPHA * x_ref[...] + (1.0 + m[2:3]) * y, g_ref[...], b_ref[...])
    xo_ref[...] = xn
    h = xn * (1.0 + m[4:5]) + m[3:4]
    h_ref[...] = h.astype(BF16)
    h_hi, h_lo = _split_bf16(h, 2)
    lg_ref[...] = _dot(h_hi, rwh_ref[...]) + (_dot(h_lo, rwh_ref[...]) + _dot(h_hi, rwl_ref[...]))


def _outproj(a1, a2, w_out, x2, mod, ln_g, ln_b, router_w, seq):
    t, d = x2.shape
    k1 = a1.shape[1]
    k2 = a2.shape[1]
    tm = 256
    per_b = seq // tm
    w1 = w_out[:k1].astype(BF16)
    w2 = w_out[k1:].astype(BF16)
    rwh, rwl = _split_bf16(router_w, 2)
    ne = router_w.shape[1]
    row = lambda i: (i, 0)
    const = lambda i: (0, 0)
    return pl.pallas_call(
        _outproj_body,
        grid=(t // tm,),
        in_specs=[
            pl.BlockSpec((tm, k1), row), pl.BlockSpec((tm, k2), row),
            pl.BlockSpec((k1, d), const), pl.BlockSpec((k2, d), const),
            pl.BlockSpec((tm, d), row),
            pl.BlockSpec((1, 6, d), lambda i: (i // per_b, 0, 0)),
            pl.BlockSpec((1, d), const), pl.BlockSpec((1, d), const),
            pl.BlockSpec((d, ne), const), pl.BlockSpec((d, ne), const),
        ],
        out_specs=[pl.BlockSpec((tm, d), row), pl.BlockSpec((tm, d), row), pl.BlockSpec((tm, ne), row)],
        out_shape=[jax.ShapeDtypeStruct((t, d), F32), jax.ShapeDtypeStruct((t, d), BF16),
                   jax.ShapeDtypeStruct((t, ne), F32)],
        compiler_params=_cp("parallel"),
        name="outproj_ln",
    )(a1, a2, w1, w2, x2, mod, ln_g.reshape(1, d), ln_b.reshape(1, d), rwh, rwl)


def _route_body(lg_ref, rb_ref, cls_ref, glo_ref, ghi_ref):
    s = jax.nn.sigmoid(lg_ref[...])
    sb = s + rb_ref[...]
    rows = [sb[e:e + 1, :] for e in range(N_EXPERTS)]
    raw = [s[e:e + 1, :] for e in range(N_EXPERTS)]
    gscore = []
    for g in range(N_EXPERT_GROUPS):
        v = rows[g * EXPERTS_PER_GROUP:(g + 1) * EXPERTS_PER_GROUP]
        best = None
        for a in range(EXPERTS_PER_GROUP):
            for b in range(a + 1, EXPERTS_PER_GROUP):
                pair = v[a] + v[b]
                best = pair if best is None else jnp.maximum(best, pair)
        gscore.append(best)
    bg = jnp.zeros_like(gscore[0], dtype=jnp.int32)
    bs = gscore[0]
    for g in range(1, N_EXPERT_GROUPS):
        upd = gscore[g] > bs
        bg = jnp.where(upd, g, bg)
        bs = jnp.where(upd, gscore[g], bs)

    def pick(vals, k):
        out = vals[k]
        for g in range(1, N_EXPERT_GROUPS):
            out = jnp.where(bg == g, vals[g * EXPERTS_PER_GROUP + k], out)
        return out

    v = [pick(rows, k) for k in range(EXPERTS_PER_GROUP)]
    r = [pick(raw, k) for k in range(EXPERTS_PER_GROUP)]
    i1 = jnp.zeros_like(bg)
    b1 = v[0]
    for k in range(1, EXPERTS_PER_GROUP):
        upd = v[k] > b1
        i1 = jnp.where(upd, k, i1)
        b1 = jnp.where(upd, v[k], b1)
    i2 = jnp.zeros_like(bg)
    b2 = jnp.full_like(b1, -jnp.inf)
    for k in range(EXPERTS_PER_GROUP):
        upd = (i1 != k) & (v[k] > b2)
        i2 = jnp.where(upd, k, i2)
        b2 = jnp.where(upd, v[k], b2)
    s1 = r[0]
    s2 = r[0]
    for k in range(1, EXPERTS_PER_GROUP):
        s1 = jnp.where(i1 == k, r[k], s1)
        s2 = jnp.where(i2 == k, r[k], s2)
    den = s1 + s2
    g1 = s1 / den
    g2 = s2 / den
    lo = jnp.minimum(i1, i2)
    hi = jnp.maximum(i1, i2)
    base = jnp.where(lo == 0, 0, jnp.where(lo == 1, 3, 5))
    cls_ref[...] = bg * N_PAIRS + base + hi - lo - 1
    first_lo = i1 < i2
    glo_ref[...] = jnp.where(first_lo, g1, g2)
    ghi_ref[...] = jnp.where(first_lo, g2, g1)


def _route(logits, router_b):
    t, ne = logits.shape
    tt = min(2048, t)
    col = lambda i: (0, i)
    cls, glo, ghi = pl.pallas_call(
        _route_body,
        grid=(t // tt,),
        in_specs=[pl.BlockSpec((ne, tt), col), pl.BlockSpec((ne, 1), lambda i: (0, 0))],
        out_specs=[pl.BlockSpec((1, tt), col)] * 3,
        out_shape=[jax.ShapeDtypeStruct((1, t), jnp.int32), jax.ShapeDtypeStruct((1, t), F32),
                   jax.ShapeDtypeStruct((1, t), F32)],
        compiler_params=_cp("parallel"),
        name="moe_route",
    )(logits.T, router_b.reshape(ne, 1).astype(F32))
    return cls[0], glo[0], ghi[0]


def _moe_body(e1_ref, e2_ref, valid_ref, h_ref, g_ref, wg1_ref, wg2_ref, wd1_ref, wd2_ref, o_ref):
    i = pl.program_id(0)

    @pl.when(valid_ref[i] == 1)
    def _():
        h = h_ref[...]
        g = g_ref[...]

        def expert(wg_ref, wd_ref):
            gu = _dot(h, wg_ref[0, 0])
            gt = gu[:, :D_EXPERT]
            act = (gt * jax.nn.sigmoid(gt) * gu[:, D_EXPERT:]).astype(BF16)
            return _dot(act, wd_ref[0, 0])

        y = g[:, 0:1] * expert(wg1_ref, wd1_ref) + g[:, 1:2] * expert(wg2_ref, wd2_ref)
        o_ref[...] = y.astype(BF16)

    @pl.when(valid_ref[i] == 0)
    def _():
        o_ref[...] = jnp.zeros_like(o_ref)


_PAIR_LO = (0, 0, 0, 1, 1, 2)
_PAIR_HI = (1, 2, 3, 2, 3, 3)


def _moe(h, cls, glo, ghi, wgu, wdn, layer):
    t, d = h.shape
    tm = MOE_TILE
    n_tiles = t // tm + N_CLASSES
    rows = n_tiles * tm
    onehot = (cls[:, None] == jnp.arange(N_CLASSES)[None, :]).astype(jnp.int32)
    csum = jnp.cumsum(onehot, 0)
    rank = jnp.take_along_axis(csum, cls[:, None], 1)[:, 0] - 1
    counts = csum[-1]
    padded = -(-counts // tm) * tm
    ends = jnp.cumsum(padded)
    dest = (ends - padded)[cls] + rank
    src = jnp.zeros((rows,), jnp.int32).at[dest].set(jnp.arange(t, dtype=jnp.int32))
    tile_start = jnp.arange(n_tiles, dtype=jnp.int32) * tm
    valid = (tile_start < ends[-1]).astype(jnp.int32)
    last_cls = jnp.max(jnp.where(counts > 0, jnp.arange(N_CLASSES), 0))
    tile_cls = jnp.minimum(jnp.searchsorted(ends, tile_start, side="right"), last_cls).astype(jnp.int32)
    grp = tile_cls // N_PAIRS
    pair = tile_cls % N_PAIRS
    e1 = (grp * EXPERTS_PER_GROUP + jnp.asarray(_PAIR_LO, jnp.int32)[pair]).astype(jnp.int32)
    e2 = (grp * EXPERTS_PER_GROUP + jnp.asarray(_PAIR_HI, jnp.int32)[pair]).astype(jnp.int32)
    hs = h.at[src].get(mode="promise_in_bounds")
    gs = jnp.stack([glo, ghi], 1).at[src].get(mode="promise_in_bounds")
    ys = pl.pallas_call(
        _moe_body,
        grid_spec=pltpu.PrefetchScalarGridSpec(
            num_scalar_prefetch=3,
            grid=(n_tiles,),
            in_specs=[
                pl.BlockSpec((tm, d), lambda i, e1, e2, va: (i, 0)),
                pl.BlockSpec((tm, 2), lambda i, e1, e2, va: (i, 0)),
                pl.BlockSpec((1, 1, d, 2 * D_EXPERT), lambda i, e1, e2, va: (layer, e1[i], 0, 0)),
                pl.BlockSpec((1, 1, d, 2 * D_EXPERT), lambda i, e1, e2, va: (layer, e2[i], 0, 0)),
                pl.BlockSpec((1, 1, D_EXPERT, d), lambda i, e1, e2, va: (layer, e1[i], 0, 0)),
                pl.BlockSpec((1, 1, D_EXPERT, d), lambda i, e1, e2, va: (layer, e2[i], 0, 0)),
            ],
            out_specs=pl.BlockSpec((tm, d), lambda i, e1, e2, va: (i, 0)),
        ),
        out_shape=jax.ShapeDtypeStruct((rows, d), BF16),
        compiler_params=_cp("arbitrary"),
        name="moe_experts",
    )(e1, e2, valid, hs, gs, wgu, wgu, wdn, wdn)
    return ys.at[dest].get(mode="promise_in_bounds")


def _resid_ln_body(x_ref, y_ref, mod_ref, g_ref, b_ref, o_ref):
    m = mod_ref[0]
    o_ref[...] = _layernorm(DN_ALPHA * x_ref[...] + (1.0 + m[5:6]) * y_ref[...].astype(F32),
                            g_ref[...], b_ref[...])


def _resid_ln(x2, y, mod, ln_g, ln_b, seq):
    t, d = x2.shape
    tm = 512
    per_b = seq // tm
    row = lambda i: (i, 0)
    const = lambda i: (0, 0)
    return pl.pallas_call(
        _resid_ln_body,
        grid=(t // tm,),
        in_specs=[pl.BlockSpec((tm, d), row), pl.BlockSpec((tm, d), row),
                  pl.BlockSpec((1, 6, d), lambda i: (i // per_b, 0, 0)),
                  pl.BlockSpec((1, d), const), pl.BlockSpec((1, d), const)],
        out_specs=pl.BlockSpec((tm, d), row),
        out_shape=jax.ShapeDtypeStruct((t, d), F32),
        compiler_params=_cp("parallel"),
        name="resid_ln",
    )(x2, y, mod, ln_g.reshape(1, d), ln_b.reshape(1, d))


def _front1_body(x_ref, mod_ref, wa_ref, wb_ref, zc0_ref, zc1_ref, zc2_ref, zr_ref, h_scr, res_scr, *, dils, tm):
    j = pl.program_id(1)
    n_att = 3 * len(dils)

    @pl.when(j == 0)
    def _():
        m = mod_ref[0]
        h_scr[...] = (x_ref[...] * (1.0 + m[1:2]) + m[0:1]).astype(BF16)

    kh = wa_ref.shape[0]
    res = _dot(h_scr[:, :kh], wa_ref[...]) + _dot(h_scr[:, kh:], wb_ref[...])

    @pl.when(j >= n_att)
    def _():
        zr_ref[...] = res.astype(BF16)

    n_slab = res.shape[1] // LANES

    @pl.when(j < n_att)
    def _():
        for c in range(n_slab):
            res_scr[c] = res[:, c * LANES:(c + 1) * LANES]

    for g, (o_ref, dil) in enumerate(zip((zc0_ref, zc1_ref, zc2_ref), dils)):
        @pl.when(j // 3 == g)
        def _(o_ref=o_ref, dil=dil):
            rows = tm // dil
            for r in range(dil):
                for c in range(n_slab):
                    o_ref[0, r, :, c * LANES:(c + 1) * LANES] = (
                        res_scr[c, pl.ds(r, rows, stride=dil), :].astype(BF16))


def _front1(x2, mod, od_w_in, batch, seq):
    t, d = x2.shape
    tm = min(1024, seq)
    tn = DIL_WIDTH
    per_b = seq // tm
    n_in = od_w_in.shape[1]
    n = -(-n_in // tn) * tn
    w = jnp.pad(od_w_in, ((0, 0), (0, n - n_in))).astype(BF16)
    dils = tuple(dil for _, dil in DIL_PATTERNS)
    n_att = 3 * len(dils)
    n_r = n // tn - n_att

    def att_spec(g, dil):
        return pl.BlockSpec((1, dil, tm // dil, tn),
                            lambda i, j: (i // per_b, 0, i % per_b, jnp.clip(j - 3 * g, 0, 2)))

    return pl.pallas_call(
        functools.partial(_front1_body, dils=dils, tm=tm),
        grid=(t // tm, n // tn),
        in_specs=[pl.BlockSpec((tm, d), lambda i, j: (i, 0)),
                  pl.BlockSpec((1, 6, d), lambda i, j: (i // per_b, 0, 0)),
                  pl.BlockSpec((d // 2, tn), lambda i, j: (0, j)),
                  pl.BlockSpec((d // 2, tn), lambda i, j: (1, j))],
        out_specs=[att_spec(g, dil) for g, dil in enumerate(dils)]
        + [pl.BlockSpec((tm, tn), lambda i, j: (i, jnp.clip(j - n_att, 0, n_r - 1)))],
        out_shape=[jax.ShapeDtypeStruct((batch, dil, seq // dil, 3 * tn), BF16) for dil in dils]
        + [jax.ShapeDtypeStruct((t, n_r * tn), BF16)],
        scratch_shapes=[pltpu.VMEM((tm, d), BF16), pltpu.VMEM((tn // LANES, tm, LANES), F32)],
        compiler_params=_cp("parallel", "arbitrary"),
        name="front1",
    )(x2, mod, w, w)


def _dil_body(q_ref, kp_ref, kc_ref, vp_ref, vc_ref, bias_ref, o_ref, lse_ref):
    n = pl.program_id(2)
    span = DIL_SPAN
    q = q_ref[0, 0]
    k = jnp.concatenate([kp_ref[0, 0], kc_ref[0, 0]], 0)
    v = jnp.concatenate([vp_ref[0, 0], vc_ref[0, 0]], 0)
    qi = lax.broadcasted_iota(jnp.int32, (span, 2 * span), 0)
    ku = lax.broadcasted_iota(jnp.int32, (span, 2 * span), 1)
    valid = (ku >= qi) & (ku <= qi + span) & ((n >= 1) | (ku >= span))
    lane = lax.broadcasted_iota(jnp.int32, (span, LANES), 1)
    first = lane < DIL_HEAD_DIM
    lse_all = jnp.zeros((span, LANES), F32)
    scale = DIL_HEAD_DIM ** -0.5
    for hp in range(DIL_HEADS // 2):
        sl = slice(hp * LANES, (hp + 1) * LANES)
        q2, k2, v2 = q[:, sl], k[:, sl], v[:, sl]
        outs = []
        for hh in range(2):
            head = 2 * hp + hh
            qm = jnp.where(first if hh == 0 else jnp.logical_not(first), q2, jnp.zeros_like(q2))
            s = _dot_nt(qm, k2) * scale + bias_ref[head]
            s = jnp.where(valid, s, NEG_INF)
            m = jnp.max(s, -1, keepdims=True)
            p = jnp.exp(s - m)
            l = jnp.sum(p, -1, keepdims=True)
            outs.append(_dot(p.astype(BF16), v2) / l)
            lse_all = jnp.where(lane == head, m + jnp.log(l), lse_all)
        o_ref[0, 0, :, sl] = jnp.where(first, outs[0], outs[1]).astype(BF16)
    lse_ref[0, 0] = lse_all


def _t5_bucket(dist):
    exact = T5_BUCKETS // 2
    logd = jnp.log(jnp.maximum(dist, 1).astype(F32) / exact) / math.log(T5_MAX_DIST / exact)
    large = jnp.minimum(exact + (logd * (T5_BUCKETS - exact)).astype(jnp.int32), T5_BUCKETS - 1)
    return jnp.where(dist < exact, dist, large)


def _dilated_group(zc, rel_bias, gi, dil, batch, seq):
    span = DIL_SPAN
    l_stream = seq // dil
    nb = l_stream // span
    qi = jnp.arange(span)[:, None]
    ku = jnp.arange(2 * span)[None, :]
    delta = jnp.clip(span + qi - ku, 0, span) * dil
    onehot = (_t5_bucket(delta)[None] == jnp.arange(T5_BUCKETS)[:, None, None]).astype(F32)
    table = rel_bias.astype(F32)[:, gi * DIL_HEADS:(gi + 1) * DIL_HEADS]
    bias = jnp.einsum("kh,kqu->hqu", table, onehot, precision=lax.Precision.HIGHEST)

    def cur(j):
        return lambda b, r, n: (b, r, n, j)

    def prev(j):
        return lambda b, r, n: (b, r, jnp.maximum(n - 1, 0), j)

    blk = (1, 1, span, DIL_WIDTH)
    return pl.pallas_call(
        _dil_body,
        grid=(batch, dil, nb),
        in_specs=[pl.BlockSpec(blk, cur(0)), pl.BlockSpec(blk, prev(1)), pl.BlockSpec(blk, cur(1)),
                  pl.BlockSpec(blk, prev(2)), pl.BlockSpec(blk, cur(2)),
                  pl.BlockSpec((DIL_HEADS, span, 2 * span), lambda b, r, n: (0, 0, 0))],
        out_specs=[pl.BlockSpec(blk, cur(0)), pl.BlockSpec((1, 1, span, LANES), cur(0))],
        out_shape=[jax.ShapeDtypeStruct((batch, dil, l_stream, DIL_WIDTH), BF16),
                   jax.ShapeDtypeStruct((batch, dil, l_stream, LANES), F32)],
        compiler_params=_cp("parallel", "parallel", "arbitrary"),
        name=f"dilated_attention_{dil}",
    )(zc, zc, zc, zc, zc, bias)


def _dil_merge_body(o1_ref, o2_ref, o3_ref, l1_ref, l2_ref, l3_ref, e_ref, out_ref, o_scr, l1_scr, l2_scr,
                    l3_scr, *, dils, tm):
    def unstream(ref, scr, dil):
        rows = tm // dil
        n_slab = scr.shape[0]
        for r in range(dil):
            x = ref[0, r].astype(F32)
            for c in range(n_slab):
                scr[c, pl.ds(r, rows, stride=dil), :] = x[:, c * LANES:(c + 1) * LANES]
        return jnp.concatenate([scr[c] for c in range(n_slab)], 1)

    lses = [unstream(ref, scr, dil) for ref, scr, dil in
            zip((l1_ref, l2_ref, l3_ref), (l1_scr, l2_scr, l3_scr), dils)]
    top = jnp.maximum(jnp.maximum(lses[0], lses[1]), lses[2])
    num = None
    den = None
    for o_ref, lse, dil in zip((o1_ref, o2_ref, o3_ref), lses, dils):
        w = _dot_exact_rhs(jnp.exp(lse - top), e_ref[...], 2)
        o = unstream(o_ref, o_scr, dil)
        num = w * o if num is None else num + w * o
        den = w if den is None else den + w
    out_ref[...] = (num / den).astype(BF16)


def _dil_merge(outs, lses, batch, seq):
    tm = 512
    per_b = seq // tm
    dils = tuple(dil for _, dil in DIL_PATTERNS)
    head_of_lane = jnp.arange(DIL_WIDTH) // DIL_HEAD_DIM
    expand = (jnp.arange(LANES)[:, None] == head_of_lane[None, :]).astype(BF16)

    def spec(dil, width):
        return pl.BlockSpec((1, dil, tm // dil, width), lambda i: (i // per_b, 0, i % per_b, 0))

    return pl.pallas_call(
        functools.partial(_dil_merge_body, dils=dils, tm=tm),
        grid=(batch * per_b,),
        in_specs=[spec(dil, DIL_WIDTH) for dil in dils] + [spec(dil, LANES) for dil in dils]
        + [pl.BlockSpec((LANES, DIL_WIDTH), lambda i: (0, 0))],
        out_specs=pl.BlockSpec((tm, DIL_WIDTH), lambda i: (i, 0)),
        out_shape=jax.ShapeDtypeStruct((batch * seq, DIL_WIDTH), BF16),
        scratch_shapes=[pltpu.VMEM((DIL_WIDTH // LANES, tm, LANES), F32)] + [pltpu.VMEM((1, tm, LANES), F32)] * 3,
        compiler_params=_cp("parallel"),
        name="dilated_merge",
    )(*outs, *lses, expand)


def _rwkv_body(zr_ref, zk_ref, zv_ref, zm_ref, mur_ref, muk_ref, muv_ref, mum_ref, w0_ref, w2_ref,
               a0_ref, a2_ref, g2_ref, kk_ref, ka_ref, rk_ref, lng_ref, lnb_ref, o_ref,
               st_ref, pr_ref, pk_ref, pv_ref, pm_ref, xr_s, xk_s, xv_s, lw_s, as_s, g_s, cum_s):
    lc = RW_CHUNK
    n_slabs = st_ref.shape[0]

    @pl.when(pl.program_id(1) == 0)
    def _():
        st_ref[...] = jnp.zeros_like(st_ref)
        pr_ref[...] = jnp.zeros_like(pr_ref)
        pk_ref[...] = jnp.zeros_like(pk_ref)
        pv_ref[...] = jnp.zeros_like(pv_ref)
        pm_ref[...] = jnp.zeros_like(pm_ref)

    def shifted(z_ref, prev_ref, mu_ref):
        z = z_ref[...].astype(F32)
        first_row = lax.broadcasted_iota(jnp.int32, z.shape, 0) == 0
        prev = jnp.where(first_row, prev_ref[...], pltpu.roll(z, 1, 0))
        prev_ref[...] = z[lc - 1:lc, :]
        return z + (prev - z) * mu_ref[...]

    xr_s[...] = shifted(zr_ref, pr_ref, mur_ref)
    xk_s[...] = shifted(zk_ref, pk_ref, muk_ref)
    xv_s[...] = shifted(zv_ref, pv_ref, muv_ref)
    xm = shifted(zm_ref, pm_ref, mum_ref)
    lora_in = xm[:, :LANES]
    wl = w0_ref[...] + _dot(jnp.tanh(lora_in).astype(BF16), w2_ref[...])
    w_log = -(jnp.maximum(-wl, 0.0) + jnp.log(1.0 + jnp.exp(-jnp.abs(wl)))) - 0.5
    lw = -jnp.exp(w_log)
    lw_s[...] = lw
    as_s[...] = jax.nn.sigmoid(a0_ref[...] + _dot(lora_in.astype(BF16), a2_ref[...]))
    g_s[...] = _dot(jax.nn.sigmoid(xm[:, LANES:]).astype(BF16), g2_ref[...])
    ti = lax.broadcasted_iota(jnp.int32, (lc, lc), 0)
    si = lax.broadcasted_iota(jnp.int32, (lc, lc), 1)
    cum_s[...] = _dot_exact_lhs((si <= ti).astype(BF16), lw, 3)

    row = lax.broadcasted_iota(jnp.int32, (2 * lc, 2 * lc), 0)
    colm = lax.broadcasted_iota(jnp.int32, (2 * lc, 2 * lc), 1)
    same_head = (row >= lc) == (colm >= lc)
    t_in = row & (lc - 1)
    s_in = colm & (lc - 1)
    strict = same_head & (s_in < t_in)
    incl = same_head & (s_in <= t_in)
    eye = row == colm
    ones_bd = same_head.astype(BF16)
    lane = lax.broadcasted_iota(jnp.int32, (lc, LANES), 1)
    first = lane < RW_HEAD

    def seg(x):
        return _dot_exact_rhs(x, ones_bd, 2)

    def stack_heads(x):
        return jnp.concatenate([jnp.where(first, x, 0.0), jnp.where(first, 0.0, x)], 0)

    slabs = range(n_slabs)
    sls = [slice(hp * LANES, (hp + 1) * LANES) for hp in slabs]
    kk_raw = [xk_s[:, sl] * kk_ref[:, sl] for sl in sls]
    kk_ss = [seg(kk * kk) for kk in kk_raw]
    lhs, rhs, bk_t, v_bs, k2s, pl_cols = [], [], [], [], [], []
    for hp, sl in enumerate(sls):
        a_sig = as_s[:, sl]
        cum = cum_s[:, sl]
        kk = kk_raw[hp] / jnp.maximum(jnp.sqrt(kk_ss[hp]), 1e-12)
        k2 = xk_s[:, sl] * (1.0 + (a_sig - 1.0) * ka_ref[:, sl])
        b_p = kk * a_sig
        cl = cum[lc - 1:lc, :]
        e_neg = jnp.exp(-cum)
        e_end = jnp.exp(cl - cum)
        at = -kk * jnp.exp(cum - lw_s[:, sl])
        rt = xr_s[:, sl] * jnp.exp(cum)
        lhs.append(jnp.concatenate([stack_heads(at), stack_heads(rt)], 0).astype(BF16))
        bt = (b_p * e_neg).astype(BF16)
        kt = (k2 * e_neg).astype(BF16)
        rhs.append(jnp.concatenate([bt, bt, kt, kt], 0))
        bk_t.append(jnp.concatenate([b_p * e_end, k2 * e_end], 0).T.astype(BF16))
        v_bs.append(xv_s[:, sl].astype(BF16))
        k2s.append(k2)
        pl_cols.append(jnp.sum(jnp.where(eye, jnp.broadcast_to(jnp.exp(cl), (2 * lc, LANES)), 0.0), 1,
                               keepdims=True))
    qms = [_dot_nt(lhs[hp], rhs[hp]) for hp in slabs]
    states = [st_ref[hp] for hp in slabs]
    arss = [_dot(lhs[hp], states[hp].astype(BF16)) for hp in slabs]
    a_abs = [jnp.where(strict, qm[:2 * lc, :2 * lc], 0.0).astype(BF16) for qm in qms]
    xs = [jnp.where(eye, 1.0, 0.0) + a.astype(F32) for a in a_abs]
    ps = [_dot(a, a).astype(BF16) for a in a_abs]
    n_double = int(math.log2(lc)) - 1
    for it in range(n_double - 1):
        res = [_dot(ps[hp], jnp.concatenate([xs[hp].astype(BF16), ps[hp]], 1)) for hp in slabs]
        xs = [xs[hp] + res[hp][:, :2 * lc] for hp in slabs]
        ps = [res[hp][:, 2 * lc:].astype(BF16) for hp in slabs]
    xs = [xs[hp] + _dot(ps[hp], xs[hp].astype(BF16)) for hp in slabs]
    vss = [jnp.concatenate([v_b, v_b], 0) for v_b in v_bs]
    gmats = [_dot(jnp.where(strict, qms[hp][:2 * lc, 2 * lc:], 0.0).astype(BF16), vss[hp]) + arss[hp][:2 * lc]
             for hp in slabs]
    uss = [_dot(xs[hp].astype(BF16), gmats[hp].astype(BF16)) for hp in slabs]
    incl2 = jnp.concatenate([incl, incl], 1)
    yss = [_dot(jnp.where(incl2, qms[hp][2 * lc:, :], 0.0).astype(BF16),
                jnp.concatenate([uss[hp].astype(BF16), vss[hp]], 0)) + arss[hp][2 * lc:] for hp in slabs]
    us_p = [jnp.where(first, us[:lc], us[lc:]) for us in uss]
    ys_p = [jnp.where(first, ys[:lc], ys[lc:]) for ys in yss]
    upds = [_dot(bk_t[hp], jnp.concatenate([us_p[hp].astype(BF16), v_bs[hp]], 0)) for hp in slabs]
    for hp in slabs:
        st_ref[hp] = pl_cols[hp] * states[hp] + jnp.where(same_head, upds[hp], 0.0)
    means = [seg(y) * (1.0 / RW_HEAD) for y in ys_p]
    dys = [ys_p[hp] - means[hp] for hp in slabs]
    variances = [seg(dy * dy) * (1.0 / RW_HEAD) for dy in dys]
    bonus = [seg(xr_s[:, sl] * k2s[hp] * rk_ref[:, sl]) for hp, sl in enumerate(sls)]
    for hp, sl in enumerate(sls):
        yn = dys[hp] * lax.rsqrt(variances[hp] + RW_GN_EPS) * lng_ref[:, sl] + lnb_ref[:, sl]
        o_ref[:, sl] = ((yn + bonus[hp] * xv_s[:, sl]) * g_s[:, sl]).astype(BF16)


def _rwkv(z, batch, seq, mu, w0, w2, a0, a2, g2, k_k, k_a, r_k, lnx_g, lnx_b):
    t = z.shape[0]
    lc = RW_CHUNK
    nc = seq // lc
    width = w0.shape[0]
    n_slabs = width // LANES
    col0 = 0
    misc_blk = 3 * width // RW_MISC
    n_lora = RW_LORA_W + RW_LORA_A + RW_LORA_G
    mu_m = jnp.pad(mu[3 * width:], (0, RW_MISC - n_lora)).reshape(1, RW_MISC)
    w2p = jnp.pad(w2, ((0, LANES - RW_LORA_W), (0, 0))).astype(BF16)
    a2p = jnp.pad(a2, ((RW_LORA_W, LANES - RW_LORA_W - RW_LORA_A), (0, 0))).astype(BF16)
    g2p = jnp.pad(g2, ((0, RW_MISC - LANES - RW_LORA_G), (0, 0))).astype(BF16)
    vec = lambda a: a.reshape(1, width).astype(F32)
    const = lambda b, c: (0, 0)
    wide = lambda: pl.BlockSpec((1, width), const)
    return pl.pallas_call(
        _rwkv_body,
        grid=(batch, nc),
        in_specs=[
            pl.BlockSpec((lc, width), lambda b, c: (b * nc + c, col0)),
            pl.BlockSpec((lc, width), lambda b, c: (b * nc + c, col0 + 1)),
            pl.BlockSpec((lc, width), lambda b, c: (b * nc + c, col0 + 2)),
            pl.BlockSpec((lc, RW_MISC), lambda b, c: (b * nc + c, misc_blk)),
            wide(), wide(), wide(), pl.BlockSpec((1, RW_MISC), const),
            wide(), pl.BlockSpec((LANES, width), const),
            wide(), pl.BlockSpec((LANES, width), const),
            pl.BlockSpec((RW_MISC - LANES, width), const),
            wide(), wide(), wide(), wide(), wide(),
        ],
        out_specs=pl.BlockSpec((lc, width), lambda b, c: (b * nc + c, 0)),
        out_shape=jax.ShapeDtypeStruct((t, width), BF16),
        scratch_shapes=[pltpu.VMEM((n_slabs, LANES, LANES), F32),
                        pltpu.VMEM((1, width), F32), pltpu.VMEM((1, width), F32),
                        pltpu.VMEM((1, width), F32), pltpu.VMEM((1, RW_MISC), F32)]
        + [pltpu.VMEM((lc, width), F32)] * 7,
        compiler_params=_cp("parallel", "arbitrary"),
        name="rwkv7",
    )(z, z, z, z, vec(mu[:width]), vec(mu[width:2 * width]), vec(mu[2 * width:3 * width]), mu_m,
      vec(w0), w2p, vec(a0), a2p, g2p, vec(k_k), vec(k_a), vec(r_k), vec(lnx_g), vec(lnx_b))


def kernel(x, c, ada_w, ada_b, ln_mix_g, ln_mix_b, ln_ffn_g, ln_ffn_b, router_w, router_b, moe_w_gate_up, moe_w_down, rel_bias, ev_w_in, mla_q_norm, mla_w_uq, mla_kv_norm, mla_w_ukv, s5_lambda_re, s5_lambda_im, s5_b_re, s5_b_im, s5_c_re, s5_c_im, s5_d, s5_log_dt, s5_w_glu, ev_w_out, od_w_in, rw_mu, rw_w0, rw_w2, rw_a0, rw_a2, rw_g2, rw_k_k, rw_k_a, rw_r_k, rw_lnx_g, rw_lnx_b, od_w_out):
    batch, seq, d = x.shape
    assert seq % 2048 == 0 and d % LANES == 0
    x2 = x.reshape(batch * seq, d)
    mods = _ada(c, ada_w, ada_b)
    wgu_all = _cast_bf16(moe_w_gate_up, moe_w_gate_up.shape[-2])
    wdn_all = _cast_bf16(moe_w_down, 2 * moe_w_down.shape[-2])
    for layer in range(DEPTH):
        mod = mods[layer]
        if layer % 2 == 0:
            e = layer // 2
            q, k, v, u = _front0(x2, mod, seq, ev_w_in[e], mla_q_norm[e], mla_w_uq[e], mla_kv_norm[e],
                                 mla_w_ukv[e])
            att = _flash(q, k, v, batch, seq)
            n_steps = max(1, math.ceil(math.log2(seq // S5_CHUNK)))
            mats = _s5_matrices(s5_lambda_re[e], s5_lambda_im[e], s5_b_re[e], s5_b_im[e], s5_c_re[e],
                                s5_c_im[e], s5_log_dt[e], n_steps)
            ssm = _s5_glu(_s5_scan(u, batch, seq, mats), u, s5_d[e], s5_w_glu[e])
            x2, h, logits = _outproj(att, ssm, ev_w_out[e], x2, mod, ln_mix_g[layer], ln_mix_b[layer],
                                     router_w, seq)
        else:
            o = layer // 2
            *zcs, z = _front1(x2, mod, od_w_in[o], batch, seq)
            outs, lses = [], []
            for gi, (win, dil) in enumerate(DIL_PATTERNS):
                assert win // dil == DIL_SPAN
                og, lg = _dilated_group(zcs[gi], rel_bias, gi, dil, batch, seq)
                outs.append(og)
                lses.append(lg)
            att = _dil_merge(outs, lses, batch, seq)
            tm_out = _rwkv(z, batch, seq, rw_mu[o], rw_w0[o], rw_w2[o], rw_a0[o], rw_a2[o], rw_g2[o],
                           rw_k_k[o], rw_k_a[o], rw_r_k[o], rw_lnx_g[o], rw_lnx_b[o])
            x2, h, logits = _outproj(att, tm_out, od_w_out[o], x2, mod, ln_mix_g[layer], ln_mix_b[layer],
                                     router_w, seq)
        cls, glo, ghi = _route(logits, router_b)
        y = _moe(h, cls, glo, ghi, wgu_all, wdn_all, layer)
        x2 = _resid_ln(x2, y, mod, ln_ffn_g[layer], ln_ffn_b[layer], seq)
    return x2.reshape(batch, seq, d)
```

```python
import functools
import math

import jax
import jax.numpy as jnp
from jax import lax
from jax.experimental import pallas as pl
from jax.experimental.pallas import tpu as pltpu

F32 = jnp.float32
BF16 = jnp.bfloat16

DEPTH = 2
DN_ALPHA = (2.0 * DEPTH) ** 0.25
LN_EPS = 1e-5
RMS_EPS = 1e-6
NEG_INF = -1e30

MLA_HEADS = 8
MLA_NOPE = 128
MLA_ROPE = 64
MLA_V = 128
MLA_QK = MLA_NOPE + MLA_ROPE
MLA_Q_RANK = 512
MLA_KV_RANK = 256
ROPE_THETA = 10000.0
MLA_HEAD_PAD = 256

S5_GROUP = 16
S5_STATE = 64
S5_CHUNK = 16

DIL_PATTERNS = ((128, 1), (512, 4), (2048, 16))
DIL_SPAN = 128
DIL_HEADS = 8
DIL_HEAD_DIM = 64
DIL_WIDTH = DIL_HEADS * DIL_HEAD_DIM
DIL_IN = len(DIL_PATTERNS) * 3 * DIL_WIDTH
T5_BUCKETS = 32
T5_MAX_DIST = 2048

RW_HEAD = 64
RW_LORA_W = 64
RW_LORA_A = 64
RW_LORA_G = 224
RW_GN_EPS = 64e-5
RW_CHUNK = 64
RW_MISC = 512

N_EXPERTS = 16
N_EXPERT_GROUPS = 4
EXPERTS_PER_GROUP = 4
D_EXPERT = 512
N_PAIRS = 6
N_CLASSES = N_EXPERT_GROUPS * N_PAIRS
MOE_TILE = 256

LANES = 128
VMEM_LIMIT = 56 * 1024 * 1024


def _cp(*sem):
    return pltpu.CompilerParams(dimension_semantics=sem, vmem_limit_bytes=VMEM_LIMIT)


def _dot(a, b):
    return jnp.dot(a, b, preferred_element_type=F32)


def _dot_nt(a, b):
    return lax.dot_general(a, b, (((1,), (1,)), ((), ())), preferred_element_type=F32)


def _split_bf16(x, parts):
    out = []
    for _ in range(parts):
        hi = x.astype(BF16)
        out.append(hi)
        x = x - hi.astype(F32)
    return out


def _dot_exact_rhs(x, w_bf16, parts=3):
    acc = None
    for p in _split_bf16(x, parts):
        t = _dot(p, w_bf16)
        acc = t if acc is None else acc + t
    return acc


def _dot_exact_lhs(w_bf16, x, parts=3):
    acc = None
    for p in _split_bf16(x, parts):
        t = _dot(w_bf16, p)
        acc = t if acc is None else acc + t
    return acc


def _layernorm(x, g, b):
    mu = jnp.mean(x, -1, keepdims=True)
    d = x - mu
    var = jnp.mean(d * d, -1, keepdims=True)
    return d * lax.rsqrt(var + LN_EPS) * g + b


def _cast_body(x_ref, o_ref):
    o_ref[...] = x_ref[...].astype(BF16)


def _cast_bf16(w, block_rows, col_blocks=2):
    w2 = w.reshape(-1, w.shape[-1])
    rows, cols = w2.shape
    spec = pl.BlockSpec((block_rows, cols // col_blocks), lambda i, j: (i, j))
    out = pl.pallas_call(
        _cast_body, grid=(rows // block_rows, col_blocks), in_specs=[spec], out_specs=spec,
        out_shape=jax.ShapeDtypeStruct((rows, cols), BF16), compiler_params=_cp("parallel", "parallel"),
        name="cast_bf16",
    )(w2)
    return out.reshape(w.shape)


def _pack_halves(x):
    n = x.shape[1] // 2
    xb = x.astype(BF16).astype(F32)
    lo = lax.shift_right_logical(pltpu.bitcast(xb[:, :n], jnp.uint32), jnp.uint32(16))
    hi = pltpu.bitcast(xb[:, n:], jnp.uint32) & jnp.uint32(0xFFFF0000)
    return lo | hi


def _unpack_halves(p):
    lo = pltpu.bitcast(lax.shift_left(p, jnp.uint32(16)), F32)
    hi = pltpu.bitcast(p & jnp.uint32(0xFFFF0000), F32)
    return lo, hi


def _gather_body(idx_ref, x_hbm, o_ref, sem, *, step):
    base = pl.program_id(0) * step

    def issue(r, carry):
        pltpu.make_async_copy(x_hbm.at[pl.ds(idx_ref[base + r], 1)], o_ref.at[pl.ds(r, 1)], sem.at[0]).start()
        return carry

    lax.fori_loop(0, step, issue, 0, unroll=8)
    pltpu.make_async_copy(x_hbm.at[pl.ds(0, step)], o_ref, sem.at[0]).wait()


def _gather_rows(x, idx):
    rows = idx.shape[0]
    d = x.shape[1]
    step = 2048 if rows % 2048 == 0 else MOE_TILE
    return pl.pallas_call(
        functools.partial(_gather_body, step=step),
        grid_spec=pltpu.PrefetchScalarGridSpec(
            num_scalar_prefetch=1,
            grid=(rows // step,),
            in_specs=[pl.BlockSpec(memory_space=pl.ANY)],
            out_specs=pl.BlockSpec((step, d), lambda i, idx: (i, 0)),
            scratch_shapes=[pltpu.SemaphoreType.DMA((1,))],
        ),
        out_shape=jax.ShapeDtypeStruct((rows, d), x.dtype),
        compiler_params=_cp("arbitrary"),
        name="gather_rows",
    )(idx, x)


def _ada_body(c_ref, w_ref, b_ref, o_ref):
    c = c_ref[...]
    cond = c * jax.nn.sigmoid(c)
    o_ref[0] = _dot_exact_rhs(cond, w_ref[0].astype(BF16), 2) + b_ref[0]


def _ada(c, ada_w, ada_b):
    depth, d, n = ada_w.shape
    b = c.shape[0]
    rows = -(-b // 8) * 8
    cp = jnp.pad(c, ((0, rows - b), (0, 0)))
    tn = 1024
    out = pl.pallas_call(
        _ada_body,
        grid=(depth, n // tn),
        in_specs=[
            pl.BlockSpec((rows, d), lambda l, j: (0, 0)),
            pl.BlockSpec((1, d, tn), lambda l, j: (l, 0, j)),
            pl.BlockSpec((1, 1, tn), lambda l, j: (l, 0, j)),
        ],
        out_specs=pl.BlockSpec((1, rows, tn), lambda l, j: (l, 0, j)),
        out_shape=jax.ShapeDtypeStruct((depth, rows, n), F32),
        compiler_params=_cp("parallel", "parallel"),
        name="ada",
    )(cp, ada_w, ada_b.reshape(depth, 1, n))
    return out[:, :b].reshape(depth, b, 6, d)


def _front0_body(x_ref, mod_ref, cs_ref, win_ref, qn_ref, kvn_ref, wq1_ref, wq2_ref, wkn_ref,
                 wv_ref, q_ref, k_ref, v_ref, u_ref):
    m = mod_ref[0]
    h = (x_ref[...] * (1.0 + m[1:2]) + m[0:1]).astype(BF16)
    z = _dot(h, win_ref[...])
    cs = cs_ref[...]
    c2 = cs[:, :LANES]
    s2 = cs[:, LANES:]
    q_c = z[:, :MLA_Q_RANK]
    kv_c = z[:, MLA_Q_RANK:MLA_Q_RANK + MLA_KV_RANK]
    o = MLA_Q_RANK + MLA_KV_RANK
    krope = (z[:, o:o + LANES] * c2 + z[:, o + LANES:o + 2 * LANES] * s2).astype(BF16)
    u_ref[...] = z[:, o + 2 * LANES:]
    hq = (q_c * lax.rsqrt(jnp.mean(q_c * q_c, -1, keepdims=True) + RMS_EPS) * qn_ref[...]).astype(BF16)
    hkv = (kv_c * lax.rsqrt(jnp.mean(kv_c * kv_c, -1, keepdims=True) + RMS_EPS) * kvn_ref[...]).astype(BF16)
    full = _dot(hq, wq1_ref[...])
    sw = _dot(hq, wq2_ref[...])
    kn = _dot(hkv, wkn_ref[...])
    v_ref[...] = _dot(hkv, wv_ref[...]).astype(BF16)
    for j in range(MLA_HEADS):
        a = j * MLA_HEAD_PAD
        q_ref[:, a:a + LANES] = full[:, a:a + LANES].astype(BF16)
        q_ref[:, a + LANES:a + 2 * LANES] = (
            full[:, a + LANES:a + 2 * LANES] * c2 + sw[:, j * LANES:(j + 1) * LANES] * s2).astype(BF16)
        k_ref[:, a:a + LANES] = kn[:, j * LANES:(j + 1) * LANES].astype(BF16)
        k_ref[:, a + LANES:a + 2 * LANES] = krope


def _rope_swapped(w):
    half = w.shape[-1] // 2
    return jnp.concatenate([-w[..., half:], w[..., :half]], -1)


def _front0(x2, mod, seq, ev_w_in, q_norm, w_uq, kv_norm, w_ukv):
    t, d = x2.shape
    tm = 256
    per_b = seq // tm
    zpad = jnp.zeros((d, LANES - MLA_ROPE), F32)
    o = MLA_Q_RANK + MLA_KV_RANK
    w_kr = ev_w_in[:, o:o + MLA_ROPE]
    win = jnp.concatenate([ev_w_in[:, :o], w_kr, zpad, _rope_swapped(w_kr), zpad,
                           ev_w_in[:, o + MLA_ROPE:]], 1).astype(BF16)
    scale = MLA_QK ** -0.5
    wq = w_uq.reshape(MLA_Q_RANK, MLA_HEADS, MLA_QK) * scale
    zq = jnp.zeros((MLA_Q_RANK, MLA_HEADS, MLA_HEAD_PAD - MLA_QK), F32)
    wq1 = jnp.concatenate([wq, zq], -1).reshape(MLA_Q_RANK, -1).astype(BF16)
    wq2 = jnp.concatenate([_rope_swapped(wq[..., MLA_NOPE:]), zq], -1).reshape(MLA_Q_RANK, -1).astype(BF16)
    wkv = w_ukv.reshape(MLA_KV_RANK, MLA_HEADS, MLA_NOPE + MLA_V)
    wkn = wkv[..., :MLA_NOPE].reshape(MLA_KV_RANK, -1).astype(BF16)
    wv = wkv[..., MLA_NOPE:].reshape(MLA_KV_RANK, -1).astype(BF16)
    inv = ROPE_THETA ** (-jnp.arange(0, MLA_ROPE, 2, dtype=F32) / MLA_ROPE)
    ang = jnp.arange(seq, dtype=F32)[:, None] * inv[None]
    zt = jnp.zeros((seq, LANES - MLA_ROPE), F32)
    cs = jnp.concatenate([jnp.cos(ang), jnp.cos(ang), zt, jnp.sin(ang), jnp.sin(ang), zt], 1)
    const = lambda i: (0, 0)
    hq_w = MLA_HEADS * MLA_HEAD_PAD
    hv_w = MLA_HEADS * MLA_V
    s5w = ev_w_in.shape[1] - o - MLA_ROPE
    return pl.pallas_call(
        _front0_body,
        grid=(t // tm,),
        in_specs=[
            pl.BlockSpec((tm, d), lambda i: (i, 0)),
            pl.BlockSpec((1, 6, d), lambda i: (i // per_b, 0, 0)),
            pl.BlockSpec((tm, 2 * LANES), lambda i: (i % per_b, 0)),
            pl.BlockSpec(win.shape, const),
            pl.BlockSpec((1, MLA_Q_RANK), const),
            pl.BlockSpec((1, MLA_KV_RANK), const),
            pl.BlockSpec(wq1.shape, const),
            pl.BlockSpec(wq2.shape, const),
            pl.BlockSpec(wkn.shape, const),
            pl.BlockSpec(wv.shape, const),
        ],
        out_specs=[
            pl.BlockSpec((tm, hq_w), lambda i: (i, 0)),
            pl.BlockSpec((tm, hq_w), lambda i: (i, 0)),
            pl.BlockSpec((tm, hv_w), lambda i: (i, 0)),
            pl.BlockSpec((tm, s5w), lambda i: (i, 0)),
        ],
        out_shape=[
            jax.ShapeDtypeStruct((t, hq_w), BF16),
            jax.ShapeDtypeStruct((t, hq_w), BF16),
            jax.ShapeDtypeStruct((t, hv_w), BF16),
            jax.ShapeDtypeStruct((t, s5w), F32),
        ],
        compiler_params=_cp("parallel"),
        name="front0",
    )(x2, mod, cs, win, q_norm.reshape(1, -1), kv_norm.reshape(1, -1), wq1, wq2, wkn, wv)


def _flash_body(q_ref, k_ref, v_ref, o_ref, *, tile, heads):
    i = pl.program_id(2)
    qs = [q_ref[:, h * MLA_HEAD_PAD:(h + 1) * MLA_HEAD_PAD] for h in range(heads)]

    def one_head(h, start, carry, diagonal):
        m, l, acc = carry
        k = k_ref[pl.ds(start, tile), h * MLA_HEAD_PAD:(h + 1) * MLA_HEAD_PAD]
        v = v_ref[pl.ds(start, tile), h * MLA_V:(h + 1) * MLA_V]
        s = _dot_nt(qs[h], k)
        if diagonal:
            row = lax.broadcasted_iota(jnp.int32, s.shape, 0)
            col = lax.broadcasted_iota(jnp.int32, s.shape, 1)
            s = jnp.where(col <= row, s, NEG_INF)
        m_new = jnp.maximum(m, jnp.max(s, -1, keepdims=True))
        alpha = jnp.exp(m - m_new)
        p = jnp.exp(s - m_new)
        l = l * alpha + jnp.sum(p, -1, keepdims=True)
        acc = acc * alpha + _dot(p.astype(BF16), v)
        return m_new, l, acc

    def step(j, carry, diagonal):
        start = pl.multiple_of(j * tile, tile)
        return tuple(one_head(h, start, carry[h], diagonal) for h in range(heads))

    init = tuple((jnp.full((tile, 1), NEG_INF, F32), jnp.zeros((tile, 1), F32),
                  jnp.zeros((tile, MLA_V), F32)) for _ in range(heads))
    carry = lax.fori_loop(0, i, lambda j, c: step(j, c, False), init)
    carry = step(i, carry, True)
    for h, (_, l, acc) in enumerate(carry):
        o_ref[:, h * MLA_V:(h + 1) * MLA_V] = (acc / l).astype(BF16)


def _flash(q, k, v, batch, seq):
    t = q.shape[0]
    tile = 512
    heads = 2
    nq = seq // tile
    return pl.pallas_call(
        functools.partial(_flash_body, tile=tile, heads=heads),
        grid=(batch, MLA_HEADS // heads, nq),
        in_specs=[
            pl.BlockSpec((tile, heads * MLA_HEAD_PAD), lambda b, h, i: (b * nq + i, h)),
            pl.BlockSpec((seq, heads * MLA_HEAD_PAD), lambda b, h, i: (b, h)),
            pl.BlockSpec((seq, heads * MLA_V), lambda b, h, i: (b, h)),
        ],
        out_specs=pl.BlockSpec((tile, heads * MLA_V), lambda b, h, i: (b * nq + i, h)),
        out_shape=jax.ShapeDtypeStruct((t, MLA_HEADS * MLA_V), BF16),
        compiler_params=_cp("parallel", "parallel", "arbitrary"),
        name="mla_attention",
    )(q, k, v)


S5_GROUPS_PER_BLOCK = LANES // S5_GROUP


def _s5_body(u_ref, w_ref, v_ref, bd_ref, ac_ref, bc_ref, y_ref, ucat, yx, *, n_chunks, n_steps):
    lc = S5_CHUNK
    for t in range(lc):
        ucat[:, t * LANES:(t + 1) * LANES] = u_ref[pl.ds(t, n_chunks, stride=lc), :].astype(BF16)
    z = _dot(ucat[...], w_ref[0])
    half = z.shape[1] // 2
    cidx = lax.broadcasted_iota(jnp.int32, z.shape, 0)
    for step in range(n_steps):
        d = 1 << step
        zs = jnp.where(cidx >= d, pltpu.roll(z, d, 0), 0.0)
        z = z + zs * ac_ref[0, step:step + 1, :] + pltpu.roll(zs, half, 1) * bc_ref[0, step:step + 1, :]
    xprev = jnp.where(cidx >= 1, pltpu.roll(z, 1, 0), 0.0)
    yx[...] = _dot(xprev.astype(BF16), v_ref[0])
    for t in range(lc):
        k = (t + 1) * LANES
        y = _dot(ucat[:, :k], bd_ref[0, (lc - 1 - t) * LANES:, :]) + yx[:, t * LANES:(t + 1) * LANES]
        y_ref[pl.ds(t, n_chunks, stride=lc), :] = y


def _s5_matrices(lam_re, lam_im, b_re, b_im, c_re, c_im, log_dt, n_steps):
    lc = S5_CHUNK
    gb = S5_GROUPS_PER_BLOCK
    lam = lax.complex(lam_re.astype(F32), lam_im.astype(F32))
    n_blk = lam.shape[0] // gb
    dt = jnp.exp(log_dt.astype(F32))[:, None]
    lam_bar = jnp.exp(lam * dt)
    b_bar = ((lam_bar - 1.0) / lam)[..., None] * lax.complex(b_re.astype(F32), b_im.astype(F32))
    c = lax.complex(c_re.astype(F32), c_im.astype(F32))
    j = jnp.arange(lc + 1, dtype=F32)[:, None, None]
    pows = jnp.exp(j * (lam * dt)[None])
    eye = jnp.eye(gb, dtype=F32)
    kern = jnp.einsum("gop,jgp,gpi->jgio", c, pows[:lc], b_bar).real
    kern = kern.reshape(lc, n_blk, gb, S5_GROUP, S5_GROUP)
    bd = jnp.einsum("jbgio,gh->bjgiho", kern, eye).reshape(n_blk, lc, LANES, LANES)
    bdrev = bd[:, ::-1].reshape(n_blk, lc * LANES, LANES)
    tt = jnp.arange(lc)
    wm = pows[lc - 1 - tt][..., None] * b_bar[None]
    wm = jnp.stack([wm.real, wm.imag], 2)
    wm = wm.reshape(lc, n_blk, gb, 2, S5_STATE, S5_GROUP)
    wcat = jnp.einsum("tbgrpi,gh->btgirhp", wm, eye).reshape(n_blk, lc * LANES, 2 * gb * S5_STATE)
    cv = c[None] * pows[1:lc + 1][:, :, None, :]
    cv = jnp.stack([cv.real, -cv.imag], 3)
    cv = cv.reshape(lc, n_blk, gb, S5_GROUP, 2, S5_STATE)
    vcat = jnp.einsum("tbgorp,gh->brgptho", cv, eye).reshape(n_blk, 2 * gb * S5_STATE, lc * LANES)
    stride = (lc * (2.0 ** jnp.arange(n_steps, dtype=F32)))[None, :, None]
    ap = jnp.exp(stride * (lam * dt)[:, None, :])

    def per_block(a):
        return a.reshape(n_blk, gb, n_steps, S5_STATE).transpose(0, 2, 1, 3).reshape(n_blk, n_steps, -1)

    ar, ai = per_block(ap.real), per_block(ap.imag)
    ac = jnp.concatenate([ar, ar], -1)
    bc = jnp.concatenate([-ai, ai], -1)
    return wcat.astype(BF16), vcat.astype(BF16), bdrev.astype(BF16), ac, bc


def _s5_scan(u, batch, seq, mats):
    wcat, vcat, bdrev, ac, bc = mats
    n_blk = wcat.shape[0]
    lc = S5_CHUNK
    nc = seq // lc
    n_steps = ac.shape[1]
    wide = wcat.shape[2]
    per_blk = lambda i, b: (i, 0, 0)
    return pl.pallas_call(
        functools.partial(_s5_body, n_chunks=nc, n_steps=n_steps),
        grid=(n_blk, batch),
        in_specs=[
            pl.BlockSpec((seq, LANES), lambda i, b: (b, i)),
            pl.BlockSpec((1, lc * LANES, wide), per_blk),
            pl.BlockSpec((1, wide, lc * LANES), per_blk),
            pl.BlockSpec((1, lc * LANES, LANES), per_blk),
            pl.BlockSpec((1, n_steps, wide), per_blk),
            pl.BlockSpec((1, n_steps, wide), per_blk),
        ],
        out_specs=pl.BlockSpec((seq, LANES), lambda i, b: (b, i)),
        out_shape=jax.ShapeDtypeStruct(u.shape, F32),
        scratch_shapes=[pltpu.VMEM((nc, lc * LANES), BF16), pltpu.VMEM((nc, lc * LANES), F32)],
        compiler_params=_cp("parallel", "arbitrary"),
        name="s5_scan",
    )(u, wcat, vcat, bdrev, ac, bc)


def _s5_glu_body(y_ref, u_ref, d_ref, w_ref, o_ref):
    y = y_ref[...] + d_ref[...] * u_ref[...]
    y = 0.5 * y * (1.0 + jnp.tanh(math.sqrt(2.0 / math.pi) * (y + 0.044715 * (y * y * y))))
    gate = jax.nn.sigmoid(_dot(y.astype(BF16), w_ref[...]))
    o_ref[...] = (y * gate).astype(BF16)


def _s5_glu(y, u, d_skip, w_glu):
    t, n = y.shape
    tm = 512
    row = lambda i: (i, 0)
    const = lambda i: (0, 0)
    return pl.pallas_call(
        _s5_glu_body,
        grid=(t // tm,),
        in_specs=[pl.BlockSpec((tm, n), row), pl.BlockSpec((tm, n), row),
                  pl.BlockSpec((1, n), const), pl.BlockSpec((n, n), const)],
        out_specs=pl.BlockSpec((tm, n), row),
        out_shape=jax.ShapeDtypeStruct((t, n), BF16),
        compiler_params=_cp("parallel"),
        name="s5_glu",
    )(y, u, d_skip.reshape(1, n), w_glu.astype(BF16))


def _outproj_body(a1_ref, a2_ref, w1_ref, w2_ref, x_ref, mod_ref, g_ref, b_ref, rwh_ref, rwl_ref,
                  xo_ref, h_ref, lg_ref):
    y = _dot(a1_ref[...], w1_ref[...]) + _dot(a2_ref[...], w2_ref[...])
    m = mod_ref[0]
    xn = _layernorm(DN_ALPHA * x_ref[...] + (1.0 + m[2:3]) * y, g_ref[...], b_ref[...])
    xo_ref[...] = xn
    h = xn * (1.0 + m[4:5]) + m[3:4]
    h_ref[...] = _pack_halves(h)
    h_hi, h_lo = _split_bf16(h, 2)
    lg_ref[...] = _dot(h_hi, rwh_ref[...]) + (_dot(h_lo, rwh_ref[...]) + _dot(h_hi, rwl_ref[...]))


def _outproj(a1, a2, w_out, x2, mod, ln_g, ln_b, router_w, seq):
    t, d = x2.shape
    k1 = a1.shape[1]
    k2 = a2.shape[1]
    tm = 256
    per_b = seq // tm
    w1 = w_out[:k1].astype(BF16)
    w2 = w_out[k1:].astype(BF16)
    rwh, rwl = _split_bf16(router_w, 2)
    ne = router_w.shape[1]
    row = lambda i: (i, 0)
    const = lambda i: (0, 0)
    return pl.pallas_call(
        _outproj_body,
        grid=(t // tm,),
        in_specs=[
            pl.BlockSpec((tm, k1), row), pl.BlockSpec((tm, k2), row),
            pl.BlockSpec((k1, d), const), pl.BlockSpec((k2, d), const),
            pl.BlockSpec((tm, d), row),
            pl.BlockSpec((1, 6, d), lambda i: (i // per_b, 0, 0)),
            pl.BlockSpec((1, d), const), pl.BlockSpec((1, d), const),
            pl.BlockSpec((d, ne), const), pl.BlockSpec((d, ne), const),
        ],
        out_specs=[pl.BlockSpec((tm, d), row), pl.BlockSpec((tm, d // 2), row), pl.BlockSpec((tm, ne), row)],
        out_shape=[jax.ShapeDtypeStruct((t, d), F32), jax.ShapeDtypeStruct((t, d // 2), jnp.uint32),
                   jax.ShapeDtypeStruct((t, ne), F32)],
        compiler_params=_cp("parallel"),
        name="outproj_ln",
    )(a1, a2, w1, w2, x2, mod, ln_g.reshape(1, d), ln_b.reshape(1, d), rwh, rwl)


def _route_body(lg_ref, rb_ref, cls_ref, glo_ref, ghi_ref):
    s = jax.nn.sigmoid(lg_ref[...])
    sb = s + rb_ref[...]
    rows = [sb[e:e + 1, :] for e in range(N_EXPERTS)]
    raw = [s[e:e + 1, :] for e in range(N_EXPERTS)]
    gscore = []
    for g in range(N_EXPERT_GROUPS):
        v = rows[g * EXPERTS_PER_GROUP:(g + 1) * EXPERTS_PER_GROUP]
        best = None
        for a in range(EXPERTS_PER_GROUP):
            for b in range(a + 1, EXPERTS_PER_GROUP):
                pair = v[a] + v[b]
                best = pair if best is None else jnp.maximum(best, pair)
        gscore.append(best)
    bg = jnp.zeros_like(gscore[0], dtype=jnp.int32)
    bs = gscore[0]
    for g in range(1, N_EXPERT_GROUPS):
        upd = gscore[g] > bs
        bg = jnp.where(upd, g, bg)
        bs = jnp.where(upd, gscore[g], bs)

    def pick(vals, k):
        out = vals[k]
        for g in range(1, N_EXPERT_GROUPS):
            out = jnp.where(bg == g, vals[g * EXPERTS_PER_GROUP + k], out)
        return out

    v = [pick(rows, k) for k in range(EXPERTS_PER_GROUP)]
    r = [pick(raw, k) for k in range(EXPERTS_PER_GROUP)]
    i1 = jnp.zeros_like(bg)
    b1 = v[0]
    for k in range(1, EXPERTS_PER_GROUP):
        upd = v[k] > b1
        i1 = jnp.where(upd, k, i1)
        b1 = jnp.where(upd, v[k], b1)
    i2 = jnp.zeros_like(bg)
    b2 = jnp.full_like(b1, -jnp.inf)
    for k in range(EXPERTS_PER_GROUP):
        upd = (i1 != k) & (v[k] > b2)
        i2 = jnp.where(upd, k, i2)
        b2 = jnp.where(upd, v[k], b2)
    s1 = r[0]
    s2 = r[0]
    for k in range(1, EXPERTS_PER_GROUP):
        s1 = jnp.where(i1 == k, r[k], s1)
        s2 = jnp.where(i2 == k, r[k], s2)
    den = s1 + s2
    g1 = s1 / den
    g2 = s2 / den
    lo = jnp.minimum(i1, i2)
    hi = jnp.maximum(i1, i2)
    base = jnp.where(lo == 0, 0, jnp.where(lo == 1, 3, 5))
    cls_ref[...] = bg * N_PAIRS + base + hi - lo - 1
    first_lo = i1 < i2
    glo_ref[...] = jnp.where(first_lo, g1, g2)
    ghi_ref[...] = jnp.where(first_lo, g2, g1)


def _route(logits, router_b):
    t, ne = logits.shape
    tt = min(2048, t)
    col = lambda i: (0, i)
    cls, glo, ghi = pl.pallas_call(
        _route_body,
        grid=(t // tt,),
        in_specs=[pl.BlockSpec((ne, tt), col), pl.BlockSpec((ne, 1), lambda i: (0, 0))],
        out_specs=[pl.BlockSpec((1, tt), col)] * 3,
        out_shape=[jax.ShapeDtypeStruct((1, t), jnp.int32), jax.ShapeDtypeStruct((1, t), F32),
                   jax.ShapeDtypeStruct((1, t), F32)],
        compiler_params=_cp("parallel"),
        name="moe_route",
    )(logits.T, router_b.reshape(ne, 1).astype(F32))
    return cls[0], glo[0], ghi[0]


def _moe_body(e1_ref, e2_ref, valid_ref, h_ref, g_ref, wg1_ref, wg2_ref, wd1_ref, wd2_ref, o_ref):
    i = pl.program_id(0)

    @pl.when(valid_ref[i] == 1)
    def _():
        h_lo, h_hi = (v.astype(BF16) for v in _unpack_halves(h_ref[...]))
        half = h_lo.shape[1]
        g = g_ref[...]

        def expert(wg_ref, wd_ref):
            gu = _dot(h_lo, wg_ref[0, 0, :half, :]) + _dot(h_hi, wg_ref[0, 0, half:, :])
            gt = gu[:, :D_EXPERT]
            act = (gt * jax.nn.sigmoid(gt) * gu[:, D_EXPERT:]).astype(BF16)
            return _dot(act, wd_ref[0, 0])

        y = g[:, 0:1] * expert(wg1_ref, wd1_ref) + g[:, 1:2] * expert(wg2_ref, wd2_ref)
        o_ref[...] = _pack_halves(y)

    @pl.when(valid_ref[i] == 0)
    def _():
        o_ref[...] = jnp.zeros_like(o_ref)


_PAIR_LO = (0, 0, 0, 1, 1, 2)
_PAIR_HI = (1, 2, 3, 2, 3, 3)


def _moe(h, cls, glo, ghi, wgu, wdn, layer):
    t = h.shape[0]
    d = 2 * h.shape[1]
    tm = MOE_TILE
    n_tiles = t // tm + N_CLASSES
    rows = n_tiles * tm
    onehot = (cls[:, None] == jnp.arange(N_CLASSES)[None, :]).astype(jnp.int32)
    csum = jnp.cumsum(onehot, 0)
    rank = jnp.take_along_axis(csum, cls[:, None], 1)[:, 0] - 1
    counts = csum[-1]
    padded = -(-counts // tm) * tm
    ends = jnp.cumsum(padded)
    dest = (ends - padded)[cls] + rank
    src = jnp.zeros((rows,), jnp.int32).at[dest].set(jnp.arange(t, dtype=jnp.int32))
    tile_start = jnp.arange(n_tiles, dtype=jnp.int32) * tm
    valid = (tile_start < ends[-1]).astype(jnp.int32)
    last_cls = jnp.max(jnp.where(counts > 0, jnp.arange(N_CLASSES), 0))
    tile_cls = jnp.minimum(jnp.searchsorted(ends, tile_start, side="right"), last_cls).astype(jnp.int32)
    grp = tile_cls // N_PAIRS
    pair = tile_cls % N_PAIRS
    e1 = (grp * EXPERTS_PER_GROUP + jnp.asarray(_PAIR_LO, jnp.int32)[pair]).astype(jnp.int32)
    e2 = (grp * EXPERTS_PER_GROUP + jnp.asarray(_PAIR_HI, jnp.int32)[pair]).astype(jnp.int32)
    hs = _gather_rows(h, src)
    gs = jnp.stack([glo, ghi], 1).at[src].get(mode="promise_in_bounds")
    ys = pl.pallas_call(
        _moe_body,
        grid_spec=pltpu.PrefetchScalarGridSpec(
            num_scalar_prefetch=3,
            grid=(n_tiles,),
            in_specs=[
                pl.BlockSpec((tm, d // 2), lambda i, e1, e2, va: (i, 0)),
                pl.BlockSpec((tm, 2), lambda i, e1, e2, va: (i, 0)),
                pl.BlockSpec((1, 1, d, 2 * D_EXPERT), lambda i, e1, e2, va: (layer, e1[i], 0, 0)),
                pl.BlockSpec((1, 1, d, 2 * D_EXPERT), lambda i, e1, e2, va: (layer, e2[i], 0, 0)),
                pl.BlockSpec((1, 1, D_EXPERT, d), lambda i, e1, e2, va: (layer, e1[i], 0, 0)),
                pl.BlockSpec((1, 1, D_EXPERT, d), lambda i, e1, e2, va: (layer, e2[i], 0, 0)),
            ],
            out_specs=pl.BlockSpec((tm, d // 2), lambda i, e1, e2, va: (i, 0)),
        ),
        out_shape=jax.ShapeDtypeStruct((rows, d // 2), jnp.uint32),
        compiler_params=_cp("arbitrary"),
        name="moe_experts",
    )(e1, e2, valid, hs, gs, wgu, wgu, wdn, wdn)
    return _gather_rows(ys, dest.astype(jnp.int32))


def _resid_ln_body(x_ref, y_ref, mod_ref, g_ref, b_ref, o_ref):
    m = mod_ref[0]
    y = jnp.concatenate(_unpack_halves(y_ref[...]), 1)
    o_ref[...] = _layernorm(DN_ALPHA * x_ref[...] + (1.0 + m[5:6]) * y, g_ref[...], b_ref[...])


def _resid_ln(x2, y, mod, ln_g, ln_b, seq):
    t, d = x2.shape
    tm = 512
    per_b = seq // tm
    row = lambda i: (i, 0)
    const = lambda i: (0, 0)
    return pl.pallas_call(
        _resid_ln_body,
        grid=(t // tm,),
        in_specs=[pl.BlockSpec((tm, d), row), pl.BlockSpec((tm, d // 2), row),
                  pl.BlockSpec((1, 6, d), lambda i: (i // per_b, 0, 0)),
                  pl.BlockSpec((1, d), const), pl.BlockSpec((1, d), const)],
        out_specs=pl.BlockSpec((tm, d), row),
        out_shape=jax.ShapeDtypeStruct((t, d), F32),
        compiler_params=_cp("parallel"),
        name="resid_ln",
    )(x2, y, mod, ln_g.reshape(1, d), ln_b.reshape(1, d))


def _front1_body(x_ref, mod_ref, wa_ref, wb_ref, zc0_ref, zc1_ref, zc2_ref, zr_ref, h_scr, res_scr, *, dils, tm):
    j = pl.program_id(1)
    n_att = 3 * len(dils)

    @pl.when(j == 0)
    def _():
        m = mod_ref[0]
        h_scr[...] = (x_ref[...] * (1.0 + m[1:2]) + m[0:1]).astype(BF16)

    kh = wa_ref.shape[0]
    res = _dot(h_scr[:, :kh], wa_ref[...]) + _dot(h_scr[:, kh:], wb_ref[...])

    @pl.when(j >= n_att)
    def _():
        zr_ref[...] = res.astype(BF16)

    n_slab = res.shape[1] // LANES

    @pl.when(j < n_att)
    def _():
        for c in range(n_slab):
            res_scr[c] = res[:, c * LANES:(c + 1) * LANES]

    for g, (o_ref, dil) in enumerate(zip((zc0_ref, zc1_ref, zc2_ref), dils)):
        @pl.when(j // 3 == g)
        def _(o_ref=o_ref, dil=dil):
            rows = tm // dil
            for r in range(dil):
                for c in range(n_slab):
                    o_ref[0, r, :, c * LANES:(c + 1) * LANES] = (
                        res_scr[c, pl.ds(r, rows, stride=dil), :].astype(BF16))


def _front1(x2, mod, od_w_in, batch, seq):
    t, d = x2.shape
    tm = min(1024, seq)
    tn = DIL_WIDTH
    per_b = seq // tm
    n_in = od_w_in.shape[1]
    n = -(-n_in // tn) * tn
    w = jnp.pad(od_w_in, ((0, 0), (0, n - n_in))).astype(BF16)
    dils = tuple(dil for _, dil in DIL_PATTERNS)
    n_att = 3 * len(dils)
    n_r = n // tn - n_att

    def att_spec(g, dil):
        return pl.BlockSpec((1, dil, tm // dil, tn),
                            lambda i, j: (i // per_b, 0, i % per_b, jnp.clip(j - 3 * g, 0, 2)))

    return pl.pallas_call(
        functools.partial(_front1_body, dils=dils, tm=tm),
        grid=(t // tm, n // tn),
        in_specs=[pl.BlockSpec((tm, d), lambda i, j: (i, 0)),
                  pl.BlockSpec((1, 6, d), lambda i, j: (i // per_b, 0, 0)),
                  pl.BlockSpec((d // 2, tn), lambda i, j: (0, j)),
                  pl.BlockSpec((d // 2, tn), lambda i, j: (1, j))],
        out_specs=[att_spec(g, dil) for g, dil in enumerate(dils)]
        + [pl.BlockSpec((tm, tn), lambda i, j: (i, jnp.clip(j - n_att, 0, n_r - 1)))],
        out_shape=[jax.ShapeDtypeStruct((batch, dil, seq // dil, 3 * tn), BF16) for dil in dils]
        + [jax.ShapeDtypeStruct((t, n_r * tn), BF16)],
        scratch_shapes=[pltpu.VMEM((tm, d), BF16), pltpu.VMEM((tn // LANES, tm, LANES), F32)],
        compiler_params=_cp("parallel", "arbitrary"),
        name="front1",
    )(x2, mod, w, w)


def _dil_body(q_ref, kp_ref, kc_ref, vp_ref, vc_ref, bias_ref, o_ref, lse_ref):
    n = pl.program_id(2)
    span = DIL_SPAN
    q = q_ref[0, 0]
    k = jnp.concatenate([kp_ref[0, 0], kc_ref[0, 0]], 0)
    v = jnp.concatenate([vp_ref[0, 0], vc_ref[0, 0]], 0)
    qi = lax.broadcasted_iota(jnp.int32, (span, 2 * span), 0)
    ku = lax.broadcasted_iota(jnp.int32, (span, 2 * span), 1)
    valid = (ku >= qi) & (ku <= qi + span) & ((n >= 1) | (ku >= span))
    lane = lax.broadcasted_iota(jnp.int32, (span, LANES), 1)
    first = lane < DIL_HEAD_DIM
    lse_all = jnp.zeros((span, LANES), F32)
    scale = DIL_HEAD_DIM ** -0.5
    for hp in range(DIL_HEADS // 2):
        sl = slice(hp * LANES, (hp + 1) * LANES)
        q2, k2, v2 = q[:, sl], k[:, sl], v[:, sl]
        outs = []
        for hh in range(2):
            head = 2 * hp + hh
            qm = jnp.where(first if hh == 0 else jnp.logical_not(first), q2, jnp.zeros_like(q2))
            s = _dot_nt(qm, k2) * scale + bias_ref[head]
            s = jnp.where(valid, s, NEG_INF)
            m = jnp.max(s, -1, keepdims=True)
            p = jnp.exp(s - m)
            l = jnp.sum(p, -1, keepdims=True)
            outs.append(_dot(p.astype(BF16), v2) / l)
            lse_all = jnp.where(lane == head, m + jnp.log(l), lse_all)
        o_ref[0, 0, :, sl] = jnp.where(first, outs[0], outs[1]).astype(BF16)
    lse_ref[0, 0] = lse_all


def _t5_bucket(dist):
    exact = T5_BUCKETS // 2
    logd = jnp.log(jnp.maximum(dist, 1).astype(F32) / exact) / math.log(T5_MAX_DIST / exact)
    large = jnp.minimum(exact + (logd * (T5_BUCKETS - exact)).astype(jnp.int32), T5_BUCKETS - 1)
    return jnp.where(dist < exact, dist, large)


def _dilated_group(zc, rel_bias, gi, dil, batch, seq):
    span = DIL_SPAN
    l_stream = seq // dil
    nb = l_stream // span
    qi = jnp.arange(span)[:, None]
    ku = jnp.arange(2 * span)[None, :]
    delta = jnp.clip(span + qi - ku, 0, span) * dil
    onehot = (_t5_bucket(delta)[None] == jnp.arange(T5_BUCKETS)[:, None, None]).astype(F32)
    table = rel_bias.astype(F32)[:, gi * DIL_HEADS:(gi + 1) * DIL_HEADS]
    bias = jnp.einsum("kh,kqu->hqu", table, onehot, precision=lax.Precision.HIGHEST)

    def cur(j):
        return lambda b, r, n: (b, r, n, j)

    def prev(j):
        return lambda b, r, n: (b, r, jnp.maximum(n - 1, 0), j)

    blk = (1, 1, span, DIL_WIDTH)
    return pl.pallas_call(
        _dil_body,
        grid=(batch, dil, nb),
        in_specs=[pl.BlockSpec(blk, cur(0)), pl.BlockSpec(blk, prev(1)), pl.BlockSpec(blk, cur(1)),
                  pl.BlockSpec(blk, prev(2)), pl.BlockSpec(blk, cur(2)),
                  pl.BlockSpec((DIL_HEADS, span, 2 * span), lambda b, r, n: (0, 0, 0))],
        out_specs=[pl.BlockSpec(blk, cur(0)), pl.BlockSpec((1, 1, span, LANES), cur(0))],
        out_shape=[jax.ShapeDtypeStruct((batch, dil, l_stream, DIL_WIDTH), BF16),
                   jax.ShapeDtypeStruct((batch, dil, l_stream, LANES), F32)],
        compiler_params=_cp("parallel", "parallel", "arbitrary"),
        name=f"dilated_attention_{dil}",
    )(zc, zc, zc, zc, zc, bias)


def _dil_merge_body(o1_ref, o2_ref, o3_ref, l1_ref, l2_ref, l3_ref, e_ref, out_ref, o_scr, l1_scr, l2_scr,
                    l3_scr, *, dils, tm):
    def unstream(ref, scr, dil):
        rows = tm // dil
        n_slab = scr.shape[0]
        for r in range(dil):
            x = ref[0, r].astype(F32)
            for c in range(n_slab):
                scr[c, pl.ds(r, rows, stride=dil), :] = x[:, c * LANES:(c + 1) * LANES]
        return jnp.concatenate([scr[c] for c in range(n_slab)], 1)

    lses = [unstream(ref, scr, dil) for ref, scr, dil in
            zip((l1_ref, l2_ref, l3_ref), (l1_scr, l2_scr, l3_scr), dils)]
    top = jnp.maximum(jnp.maximum(lses[0], lses[1]), lses[2])
    num = None
    den = None
    for o_ref, lse, dil in zip((o1_ref, o2_ref, o3_ref), lses, dils):
        w = _dot_exact_rhs(jnp.exp(lse - top), e_ref[...], 2)
        o = unstream(o_ref, o_scr, dil)
        num = w * o if num is None else num + w * o
        den = w if den is None else den + w
    out_ref[...] = (num / den).astype(BF16)


def _dil_merge(outs, lses, batch, seq):
    tm = 512
    per_b = seq // tm
    dils = tuple(dil for _, dil in DIL_PATTERNS)
    head_of_lane = jnp.arange(DIL_WIDTH) // DIL_HEAD_DIM
    expand = (jnp.arange(LANES)[:, None] == head_of_lane[None, :]).astype(BF16)

    def spec(dil, width):
        return pl.BlockSpec((1, dil, tm // dil, width), lambda i: (i // per_b, 0, i % per_b, 0))

    return pl.pallas_call(
        functools.partial(_dil_merge_body, dils=dils, tm=tm),
        grid=(batch * per_b,),
        in_specs=[spec(dil, DIL_WIDTH) for dil in dils] + [spec(dil, LANES) for dil in dils]
        + [pl.BlockSpec((LANES, DIL_WIDTH), lambda i: (0, 0))],
        out_specs=pl.BlockSpec((tm, DIL_WIDTH), lambda i: (i, 0)),
        out_shape=jax.ShapeDtypeStruct((batch * seq, DIL_WIDTH), BF16),
        scratch_shapes=[pltpu.VMEM((DIL_WIDTH // LANES, tm, LANES), F32)] + [pltpu.VMEM((1, tm, LANES), F32)] * 3,
        compiler_params=_cp("parallel"),
        name="dilated_merge",
    )(*outs, *lses, expand)


def _rwkv_body(zr_ref, zk_ref, zv_ref, zm_ref, mur_ref, muk_ref, muv_ref, mum_ref, w0_ref, w2_ref,
               a0_ref, a2_ref, g2_ref, kk_ref, ka_ref, rk_ref, lng_ref, lnb_ref, o_ref,
               st_ref, pr_ref, pk_ref, pv_ref, pm_ref, xr_s, xk_s, xv_s, lw_s, as_s, g_s, cum_s):
    lc = RW_CHUNK
    n_slabs = st_ref.shape[0]

    @pl.when(pl.program_id(1) == 0)
    def _():
        st_ref[...] = jnp.zeros_like(st_ref)
        pr_ref[...] = jnp.zeros_like(pr_ref)
        pk_ref[...] = jnp.zeros_like(pk_ref)
        pv_ref[...] = jnp.zeros_like(pv_ref)
        pm_ref[...] = jnp.zeros_like(pm_ref)

    def shifted(z_ref, prev_ref, mu_ref):
        z = z_ref[...].astype(F32)
        first_row = lax.broadcasted_iota(jnp.int32, z.shape, 0) == 0
        prev = jnp.where(first_row, prev_ref[...], pltpu.roll(z, 1, 0))
        prev_ref[...] = z[lc - 1:lc, :]
        return z + (prev - z) * mu_ref[...]

    xr_s[...] = shifted(zr_ref, pr_ref, mur_ref)
    xk_s[...] = shifted(zk_ref, pk_ref, muk_ref)
    xv_s[...] = shifted(zv_ref, pv_ref, muv_ref)
    xm = shifted(zm_ref, pm_ref, mum_ref)
    lora_in = xm[:, :LANES]
    wl = w0_ref[...] + _dot(jnp.tanh(lora_in).astype(BF16), w2_ref[...])
    w_log = -(jnp.maximum(-wl, 0.0) + jnp.log(1.0 + jnp.exp(-jnp.abs(wl)))) - 0.5
    lw = -jnp.exp(w_log)
    lw_s[...] = lw
    as_s[...] = jax.nn.sigmoid(a0_ref[...] + _dot(lora_in.astype(BF16), a2_ref[...]))
    g_s[...] = _dot(jax.nn.sigmoid(xm[:, LANES:]).astype(BF16), g2_ref[...])
    ti = lax.broadcasted_iota(jnp.int32, (lc, lc), 0)
    si = lax.broadcasted_iota(jnp.int32, (lc, lc), 1)
    cum_s[...] = _dot_exact_lhs((si <= ti).astype(BF16), lw, 3)

    row = lax.broadcasted_iota(jnp.int32, (2 * lc, 2 * lc), 0)
    colm = lax.broadcasted_iota(jnp.int32, (2 * lc, 2 * lc), 1)
    same_head = (row >= lc) == (colm >= lc)
    t_in = row & (lc - 1)
    s_in = colm & (lc - 1)
    strict = same_head & (s_in < t_in)
    incl = same_head & (s_in <= t_in)
    eye = row == colm
    ones_bd = same_head.astype(BF16)
    lane = lax.broadcasted_iota(jnp.int32, (lc, LANES), 1)
    first = lane < RW_HEAD

    def seg(x):
        return _dot_exact_rhs(x, ones_bd, 2)

    def stack_heads(x):
        return jnp.concatenate([jnp.where(first, x, 0.0), jnp.where(first, 0.0, x)], 0)

    slabs = range(n_slabs)
    sls = [slice(hp * LANES, (hp + 1) * LANES) for hp in slabs]
    kk_raw = [xk_s[:, sl] * kk_ref[:, sl] for sl in sls]
    kk_ss = [seg(kk * kk) for kk in kk_raw]
    lhs, rhs, bk_t, v_bs, k2s, pl_cols = [], [], [], [], [], []
    for hp, sl in enumerate(sls):
        a_sig = as_s[:, sl]
        cum = cum_s[:, sl]
        kk = kk_raw[hp] / jnp.maximum(jnp.sqrt(kk_ss[hp]), 1e-12)
        k2 = xk_s[:, sl] * (1.0 + (a_sig - 1.0) * ka_ref[:, sl])
        b_p = kk * a_sig
        cl = cum[lc - 1:lc, :]
        e_neg = jnp.exp(-cum)
        e_end = jnp.exp(cl - cum)
        at = -kk * jnp.exp(cum - lw_s[:, sl])
        rt = xr_s[:, sl] * jnp.exp(cum)
        lhs.append(jnp.concatenate([stack_heads(at), stack_heads(rt)], 0).astype(BF16))
        bt = (b_p * e_neg).astype(BF16)
        kt = (k2 * e_neg).astype(BF16)
        rhs.append(jnp.concatenate([bt, bt, kt, kt], 0))
        bk_t.append(jnp.concatenate([b_p * e_end, k2 * e_end], 0).T.astype(BF16))
        v_bs.append(xv_s[:, sl].astype(BF16))
        k2s.append(k2)
        pl_cols.append(jnp.sum(jnp.where(eye, jnp.broadcast_to(jnp.exp(cl), (2 * lc, LANES)), 0.0), 1,
                               keepdims=True))
    qms = [_dot_nt(lhs[hp], rhs[hp]) for hp in slabs]
    states = [st_ref[hp] for hp in slabs]
    arss = [_dot(lhs[hp], states[hp].astype(BF16)) for hp in slabs]
    a_abs = [jnp.where(strict, qm[:2 * lc, :2 * lc], 0.0).astype(BF16) for qm in qms]
    xs = [jnp.where(eye, 1.0, 0.0) + a.astype(F32) for a in a_abs]
    ps = [_dot(a, a).astype(BF16) for a in a_abs]
    n_double = int(math.log2(lc)) - 1
    for it in range(n_double - 1):
        res = [_dot(ps[hp], jnp.concatenate([xs[hp].astype(BF16), ps[hp]], 1)) for hp in slabs]
        xs = [xs[hp] + res[hp][:, :2 * lc] for hp in slabs]
        ps = [res[hp][:, 2 * lc:].astype(BF16) for hp in slabs]
    xs = [xs[hp] + _dot(ps[hp], xs[hp].astype(BF16)) for hp in slabs]
    vss = [jnp.concatenate([v_b, v_b], 0) for v_b in v_bs]
    gmats = [_dot(jnp.where(strict, qms[hp][:2 * lc, 2 * lc:], 0.0).astype(BF16), vss[hp]) + arss[hp][:2 * lc]
             for hp in slabs]
    uss = [_dot(xs[hp].astype(BF16), gmats[hp].astype(BF16)) for hp in slabs]
    incl2 = jnp.concatenate([incl, incl], 1)
    yss = [_dot(jnp.where(incl2, qms[hp][2 * lc:, :], 0.0).astype(BF16),
                jnp.concatenate([uss[hp].astype(BF16), vss[hp]], 0)) + arss[hp][2 * lc:] for hp in slabs]
    us_p = [jnp.where(first, us[:lc], us[lc:]) for us in uss]
    ys_p = [jnp.where(first, ys[:lc], ys[lc:]) for ys in yss]
    upds = [_dot(bk_t[hp], jnp.concatenate([us_p[hp].astype(BF16), v_bs[hp]], 0)) for hp in slabs]
    for hp in slabs:
        st_ref[hp] = pl_cols[hp] * states[hp] + jnp.where(same_head, upds[hp], 0.0)
    means = [seg(y) * (1.0 / RW_HEAD) for y in ys_p]
    dys = [ys_p[hp] - means[hp] for hp in slabs]
    variances = [seg(dy * dy) * (1.0 / RW_HEAD) for dy in dys]
    bonus = [seg(xr_s[:, sl] * k2s[hp] * rk_ref[:, sl]) for hp, sl in enumerate(sls)]
    for hp, sl in enumerate(sls):
        yn = dys[hp] * lax.rsqrt(variances[hp] + RW_GN_EPS) * lng_ref[:, sl] + lnb_ref[:, sl]
        o_ref[:, sl] = ((yn + bonus[hp] * xv_s[:, sl]) * g_s[:, sl]).astype(BF16)


def _rwkv(z, batch, seq, mu, w0, w2, a0, a2, g2, k_k, k_a, r_k, lnx_g, lnx_b):
    t = z.shape[0]
    lc = RW_CHUNK
    nc = seq // lc
    width = w0.shape[0]
    n_slabs = width // LANES
    col0 = 0
    misc_blk = 3 * width // RW_MISC
    n_lora = RW_LORA_W + RW_LORA_A + RW_LORA_G
    mu_m = jnp.pad(mu[3 * width:], (0, RW_MISC - n_lora)).reshape(1, RW_MISC)
    w2p = jnp.pad(w2, ((0, LANES - RW_LORA_W), (0, 0))).astype(BF16)
    a2p = jnp.pad(a2, ((RW_LORA_W, LANES - RW_LORA_W - RW_LORA_A), (0, 0))).astype(BF16)
    g2p = jnp.pad(g2, ((0, RW_MISC - LANES - RW_LORA_G), (0, 0))).astype(BF16)
    vec = lambda a: a.reshape(1, width).astype(F32)
    const = lambda b, c: (0, 0)
    wide = lambda: pl.BlockSpec((1, width), const)
    return pl.pallas_call(
        _rwkv_body,
        grid=(batch, nc),
        in_specs=[
            pl.BlockSpec((lc, width), lambda b, c: (b * nc + c, col0)),
            pl.BlockSpec((lc, width), lambda b, c: (b * nc + c, col0 + 1)),
            pl.BlockSpec((lc, width), lambda b, c: (b * nc + c, col0 + 2)),
            pl.BlockSpec((lc, RW_MISC), lambda b, c: (b * nc + c, misc_blk)),
            wide(), wide(), wide(), pl.BlockSpec((1, RW_MISC), const),
            wide(), pl.BlockSpec((LANES, width), const),
            wide(), pl.BlockSpec((LANES, width), const),
            pl.BlockSpec((RW_MISC - LANES, width), const),
            wide(), wide(), wide(), wide(), wide(),
        ],
        out_specs=pl.BlockSpec((lc, width), lambda b, c: (b * nc + c, 0)),
        out_shape=jax.ShapeDtypeStruct((t, width), BF16),
        scratch_shapes=[pltpu.VMEM((n_slabs, LANES, LANES), F32),
                        pltpu.VMEM((1, width), F32), pltpu.VMEM((1, width), F32),
                        pltpu.VMEM((1, width), F32), pltpu.VMEM((1, RW_MISC), F32)]
        + [pltpu.VMEM((lc, width), F32)] * 7,
        compiler_params=_cp("parallel", "arbitrary"),
        name="rwkv7",
    )(z, z, z, z, vec(mu[:width]), vec(mu[width:2 * width]), vec(mu[2 * width:3 * width]), mu_m,
      vec(w0), w2p, vec(a0), a2p, g2p, vec(k_k), vec(k_a), vec(r_k), vec(lnx_g), vec(lnx_b))


def kernel(x, c, ada_w, ada_b, ln_mix_g, ln_mix_b, ln_ffn_g, ln_ffn_b, router_w, router_b, moe_w_gate_up, moe_w_down, rel_bias, ev_w_in, mla_q_norm, mla_w_uq, mla_kv_norm, mla_w_ukv, s5_lambda_re, s5_lambda_im, s5_b_re, s5_b_im, s5_c_re, s5_c_im, s5_d, s5_log_dt, s5_w_glu, ev_w_out, od_w_in, rw_mu, rw_w0, rw_w2, rw_a0, rw_a2, rw_g2, rw_k_k, rw_k_a, rw_r_k, rw_lnx_g, rw_lnx_b, od_w_out):
    batch, seq, d = x.shape
    assert seq % 2048 == 0 and d % LANES == 0
    x2 = x.reshape(batch * seq, d)
    mods = _ada(c, ada_w, ada_b)
    wgu_all = _cast_bf16(moe_w_gate_up, 2 * moe_w_gate_up.shape[-2])
    wdn_all = _cast_bf16(moe_w_down, 4 * moe_w_down.shape[-2])
    for layer in range(DEPTH):
        mod = mods[layer]
        if layer % 2 == 0:
            e = layer // 2
            q, k, v, u = _front0(x2, mod, seq, ev_w_in[e], mla_q_norm[e], mla_w_uq[e], mla_kv_norm[e],
                                 mla_w_ukv[e])
            att = _flash(q, k, v, batch, seq)
            n_steps = max(1, math.ceil(math.log2(seq // S5_CHUNK)))
            mats = _s5_matrices(s5_lambda_re[e], s5_lambda_im[e], s5_b_re[e], s5_b_im[e], s5_c_re[e],
                                s5_c_im[e], s5_log_dt[e], n_steps)
            ssm = _s5_glu(_s5_scan(u, batch, seq, mats), u, s5_d[e], s5_w_glu[e])
            x2, h, logits = _outproj(att, ssm, ev_w_out[e], x2, mod, ln_mix_g[layer], ln_mix_b[layer],
                                     router_w, seq)
        else:
            o = layer // 2
            *zcs, z = _front1(x2, mod, od_w_in[o], batch, seq)
            outs, lses = [], []
            for gi, (win, dil) in enumerate(DIL_PATTERNS):
                assert win // dil == DIL_SPAN
                og, lg = _dilated_group(zcs[gi], rel_bias, gi, dil, batch, seq)
                outs.append(og)
                lses.append(lg)
            att = _dil_merge(outs, lses, batch, seq)
            tm_out = _rwkv(z, batch, seq, rw_mu[o], rw_w0[o], rw_w2[o], rw_a0[o], rw_a2[o], rw_g2[o],
                           rw_k_k[o], rw_k_a[o], rw_r_k[o], rw_lnx_g[o], rw_lnx_b[o])
            x2, h, logits = _outproj(att, tm_out, od_w_out[o], x2, mod, ln_mix_g[layer], ln_mix_b[layer],
                                     router_w, seq)
        cls, glo, ghi = _route(logits, router_b)
        y = _moe(h, cls, glo, ghi, wgu_all, wdn_all, layer)
        x2 = _resid_ln(x2, y, mod, ln_ffn_g[layer], ln_ffn_b[layer], seq)
    return x2.reshape(batch, seq, d)
```

```python
import functools
import math

import jax
import jax.numpy as jnp
from jax import lax
from jax.experimental import pallas as pl
from jax.experimental.pallas import tpu as pltpu

F32 = jnp.float32
BF16 = jnp.bfloat16

DEPTH = 2
DN_ALPHA = (2.0 * DEPTH) ** 0.25
LN_EPS = 1e-5
RMS_EPS = 1e-6
NEG_INF = -1e30

MLA_HEADS = 8
MLA_NOPE = 128
MLA_ROPE = 64
MLA_V = 128
MLA_QK = MLA_NOPE + MLA_ROPE
MLA_Q_RANK = 512
MLA_KV_RANK = 256
ROPE_THETA = 10000.0
MLA_HEAD_PAD = 256

S5_GROUP = 16
S5_STATE = 64
S5_CHUNK = 16

DIL_PATTERNS = ((128, 1), (512, 4), (2048, 16))
DIL_SPAN = 128
DIL_HEADS = 8
DIL_HEAD_DIM = 64
DIL_WIDTH = DIL_HEADS * DIL_HEAD_DIM
DIL_IN = len(DIL_PATTERNS) * 3 * DIL_WIDTH
T5_BUCKETS = 32
T5_MAX_DIST = 2048

RW_HEAD = 64
RW_LORA_W = 64
RW_LORA_A = 64
RW_LORA_G = 224
RW_GN_EPS = 64e-5
RW_CHUNK = 64
RW_MISC = 512

N_EXPERTS = 16
N_EXPERT_GROUPS = 4
EXPERTS_PER_GROUP = 4
D_EXPERT = 512
N_PAIRS = 6
N_CLASSES = N_EXPERT_GROUPS * N_PAIRS
MOE_TILE = 256

LANES = 128
VMEM_LIMIT = 56 * 1024 * 1024


def _cp(*sem):
    return pltpu.CompilerParams(dimension_semantics=sem, vmem_limit_bytes=VMEM_LIMIT)


def _dot(a, b):
    return jnp.dot(a, b, preferred_element_type=F32)


def _dot_nt(a, b):
    return lax.dot_general(a, b, (((1,), (1,)), ((), ())), preferred_element_type=F32)


def _split_bf16(x, parts):
    out = []
    for _ in range(parts):
        hi = x.astype(BF16)
        out.append(hi)
        x = x - hi.astype(F32)
    return out


def _dot_exact_rhs(x, w_bf16, parts=3):
    acc = None
    for p in _split_bf16(x, parts):
        t = _dot(p, w_bf16)
        acc = t if acc is None else acc + t
    return acc


def _dot_exact_lhs(w_bf16, x, parts=3):
    acc = None
    for p in _split_bf16(x, parts):
        t = _dot(w_bf16, p)
        acc = t if acc is None else acc + t
    return acc


def _layernorm(x, g, b):
    mu = jnp.mean(x, -1, keepdims=True)
    d = x - mu
    var = jnp.mean(d * d, -1, keepdims=True)
    return d * lax.rsqrt(var + LN_EPS) * g + b


def _cast_body(x_ref, o_ref):
    o_ref[...] = x_ref[...].astype(BF16)


def _cast_bf16(w, block_rows, col_blocks=2):
    w2 = w.reshape(-1, w.shape[-1])
    rows, cols = w2.shape
    spec = pl.BlockSpec((block_rows, cols // col_blocks), lambda i, j: (i, j))
    out = pl.pallas_call(
        _cast_body, grid=(rows // block_rows, col_blocks), in_specs=[spec], out_specs=spec,
        out_shape=jax.ShapeDtypeStruct((rows, cols), BF16), compiler_params=_cp("parallel", "parallel"),
        name="cast_bf16",
    )(w2)
    return out.reshape(w.shape)


def _pack_halves(x):
    n = x.shape[1] // 2
    xb = x.astype(BF16).astype(F32)
    lo = lax.shift_right_logical(pltpu.bitcast(xb[:, :n], jnp.uint32), jnp.uint32(16))
    hi = pltpu.bitcast(xb[:, n:], jnp.uint32) & jnp.uint32(0xFFFF0000)
    return lo | hi


def _unpack_halves(p):
    lo = pltpu.bitcast(lax.shift_left(p, jnp.uint32(16)), F32)
    hi = pltpu.bitcast(p & jnp.uint32(0xFFFF0000), F32)
    return lo, hi


def _gather_body(idx_ref, x_hbm, o_ref, sem, *, step):
    base = pl.program_id(0) * step

    def issue(r, carry):
        pltpu.make_async_copy(x_hbm.at[pl.ds(idx_ref[base + r], 1)], o_ref.at[pl.ds(r, 1)], sem.at[0]).start()
        return carry

    lax.fori_loop(0, step, issue, 0, unroll=8)
    pltpu.make_async_copy(x_hbm.at[pl.ds(0, step)], o_ref, sem.at[0]).wait()


def _gather_rows(x, idx):
    rows = idx.shape[0]
    d = x.shape[1]
    step = 2048 if rows % 2048 == 0 else MOE_TILE
    return pl.pallas_call(
        functools.partial(_gather_body, step=step),
        grid_spec=pltpu.PrefetchScalarGridSpec(
            num_scalar_prefetch=1,
            grid=(rows // step,),
            in_specs=[pl.BlockSpec(memory_space=pl.ANY)],
            out_specs=pl.BlockSpec((step, d), lambda i, idx: (i, 0)),
            scratch_shapes=[pltpu.SemaphoreType.DMA((1,))],
        ),
        out_shape=jax.ShapeDtypeStruct((rows, d), x.dtype),
        compiler_params=_cp("arbitrary"),
        name="gather_rows",
    )(idx, x)


def _ada_body(c_ref, w_ref, b_ref, o_ref):
    c = c_ref[...]
    cond = c * jax.nn.sigmoid(c)
    o_ref[0] = _dot_exact_rhs(cond, w_ref[0].astype(BF16), 2) + b_ref[0]


def _ada(c, ada_w, ada_b):
    depth, d, n = ada_w.shape
    b = c.shape[0]
    rows = -(-b // 8) * 8
    cp = jnp.pad(c, ((0, rows - b), (0, 0)))
    tn = 1024
    out = pl.pallas_call(
        _ada_body,
        grid=(depth, n // tn),
        in_specs=[
            pl.BlockSpec((rows, d), lambda l, j: (0, 0)),
            pl.BlockSpec((1, d, tn), lambda l, j: (l, 0, j)),
            pl.BlockSpec((1, 1, tn), lambda l, j: (l, 0, j)),
        ],
        out_specs=pl.BlockSpec((1, rows, tn), lambda l, j: (l, 0, j)),
        out_shape=jax.ShapeDtypeStruct((depth, rows, n), F32),
        compiler_params=_cp("parallel", "parallel"),
        name="ada",
    )(cp, ada_w, ada_b.reshape(depth, 1, n))
    return out[:, :b].reshape(depth, b, 6, d)


def _front0_body(x_ref, mod_ref, cs_ref, win_ref, qn_ref, kvn_ref, wq1_ref, wq2_ref, wkn_ref,
                 wv_ref, q_ref, k_ref, v_ref, u_ref):
    m = mod_ref[0]
    h = (x_ref[...] * (1.0 + m[1:2]) + m[0:1]).astype(BF16)
    z = _dot(h, win_ref[...])
    cs = cs_ref[...]
    c2 = cs[:, :LANES]
    s2 = cs[:, LANES:]
    q_c = z[:, :MLA_Q_RANK]
    kv_c = z[:, MLA_Q_RANK:MLA_Q_RANK + MLA_KV_RANK]
    o = MLA_Q_RANK + MLA_KV_RANK
    krope = (z[:, o:o + LANES] * c2 + z[:, o + LANES:o + 2 * LANES] * s2).astype(BF16)
    u_ref[...] = z[:, o + 2 * LANES:]
    hq = (q_c * lax.rsqrt(jnp.mean(q_c * q_c, -1, keepdims=True) + RMS_EPS) * qn_ref[...]).astype(BF16)
    hkv = (kv_c * lax.rsqrt(jnp.mean(kv_c * kv_c, -1, keepdims=True) + RMS_EPS) * kvn_ref[...]).astype(BF16)
    full = _dot(hq, wq1_ref[...])
    sw = _dot(hq, wq2_ref[...])
    kn = _dot(hkv, wkn_ref[...])
    v_ref[...] = _dot(hkv, wv_ref[...]).astype(BF16)
    for j in range(MLA_HEADS):
        a = j * MLA_HEAD_PAD
        q_ref[:, a:a + LANES] = full[:, a:a + LANES].astype(BF16)
        q_ref[:, a + LANES:a + 2 * LANES] = (
            full[:, a + LANES:a + 2 * LANES] * c2 + sw[:, j * LANES:(j + 1) * LANES] * s2).astype(BF16)
        k_ref[:, a:a + LANES] = kn[:, j * LANES:(j + 1) * LANES].astype(BF16)
        k_ref[:, a + LANES:a + 2 * LANES] = krope


def _rope_swapped(w):
    half = w.shape[-1] // 2
    return jnp.concatenate([-w[..., half:], w[..., :half]], -1)


def _front0(x2, mod, seq, ev_w_in, q_norm, w_uq, kv_norm, w_ukv):
    t, d = x2.shape
    tm = 256
    per_b = seq // tm
    zpad = jnp.zeros((d, LANES - MLA_ROPE), F32)
    o = MLA_Q_RANK + MLA_KV_RANK
    w_kr = ev_w_in[:, o:o + MLA_ROPE]
    win = jnp.concatenate([ev_w_in[:, :o], w_kr, zpad, _rope_swapped(w_kr), zpad,
                           ev_w_in[:, o + MLA_ROPE:]], 1).astype(BF16)
    scale = MLA_QK ** -0.5
    wq = w_uq.reshape(MLA_Q_RANK, MLA_HEADS, MLA_QK) * scale
    zq = jnp.zeros((MLA_Q_RANK, MLA_HEADS, MLA_HEAD_PAD - MLA_QK), F32)
    wq1 = jnp.concatenate([wq, zq], -1).reshape(MLA_Q_RANK, -1).astype(BF16)
    wq2 = jnp.concatenate([_rope_swapped(wq[..., MLA_NOPE:]), zq], -1).reshape(MLA_Q_RANK, -1).astype(BF16)
    wkv = w_ukv.reshape(MLA_KV_RANK, MLA_HEADS, MLA_NOPE + MLA_V)
    wkn = wkv[..., :MLA_NOPE].reshape(MLA_KV_RANK, -1).astype(BF16)
    wv = wkv[..., MLA_NOPE:].reshape(MLA_KV_RANK, -1).astype(BF16)
    inv = ROPE_THETA ** (-jnp.arange(0, MLA_ROPE, 2, dtype=F32) / MLA_ROPE)
    ang = jnp.arange(seq, dtype=F32)[:, None] * inv[None]
    zt = jnp.zeros((seq, LANES - MLA_ROPE), F32)
    cs = jnp.concatenate([jnp.cos(ang), jnp.cos(ang), zt, jnp.sin(ang), jnp.sin(ang), zt], 1)
    const = lambda i: (0, 0)
    hq_w = MLA_HEADS * MLA_HEAD_PAD
    hv_w = MLA_HEADS * MLA_V
    s5w = ev_w_in.shape[1] - o - MLA_ROPE
    return pl.pallas_call(
        _front0_body,
        grid=(t // tm,),
        in_specs=[
            pl.BlockSpec((tm, d), lambda i: (i, 0)),
            pl.BlockSpec((1, 6, d), lambda i: (i // per_b, 0, 0)),
            pl.BlockSpec((tm, 2 * LANES), lambda i: (i % per_b, 0)),
            pl.BlockSpec(win.shape, const),
            pl.BlockSpec((1, MLA_Q_RANK), const),
            pl.BlockSpec((1, MLA_KV_RANK), const),
            pl.BlockSpec(wq1.shape, const),
            pl.BlockSpec(wq2.shape, const),
            pl.BlockSpec(wkn.shape, const),
            pl.BlockSpec(wv.shape, const),
        ],
        out_specs=[
            pl.BlockSpec((tm, hq_w), lambda i: (i, 0)),
            pl.BlockSpec((tm, hq_w), lambda i: (i, 0)),
            pl.BlockSpec((tm, hv_w), lambda i: (i, 0)),
            pl.BlockSpec((tm, s5w), lambda i: (i, 0)),
        ],
        out_shape=[
            jax.ShapeDtypeStruct((t, hq_w), BF16),
            jax.ShapeDtypeStruct((t, hq_w), BF16),
            jax.ShapeDtypeStruct((t, hv_w), BF16),
            jax.ShapeDtypeStruct((t, s5w), F32),
        ],
        compiler_params=_cp("parallel"),
        name="front0",
    )(x2, mod, cs, win, q_norm.reshape(1, -1), kv_norm.reshape(1, -1), wq1, wq2, wkn, wv)


def _flash_body(q_ref, k_ref, v_ref, o_ref, *, tile, heads):
    i = pl.program_id(2)
    qs = [q_ref[:, h * MLA_HEAD_PAD:(h + 1) * MLA_HEAD_PAD] for h in range(heads)]

    def one_head(h, start, carry, diagonal):
        m, l, acc = carry
        k = k_ref[pl.ds(start, tile), h * MLA_HEAD_PAD:(h + 1) * MLA_HEAD_PAD]
        v = v_ref[pl.ds(start, tile), h * MLA_V:(h + 1) * MLA_V]
        s = _dot_nt(qs[h], k)
        if diagonal:
            row = lax.broadcasted_iota(jnp.int32, s.shape, 0)
            col = lax.broadcasted_iota(jnp.int32, s.shape, 1)
            s = jnp.where(col <= row, s, NEG_INF)
        m_new = jnp.maximum(m, jnp.max(s, -1, keepdims=True))
        alpha = jnp.exp(m - m_new)
        p = jnp.exp(s - m_new)
        l = l * alpha + jnp.sum(p, -1, keepdims=True)
        acc = acc * alpha + _dot(p.astype(BF16), v)
        return m_new, l, acc

    def step(j, carry, diagonal):
        start = pl.multiple_of(j * tile, tile)
        return tuple(one_head(h, start, carry[h], diagonal) for h in range(heads))

    init = tuple((jnp.full((tile, 1), NEG_INF, F32), jnp.zeros((tile, 1), F32),
                  jnp.zeros((tile, MLA_V), F32)) for _ in range(heads))
    carry = lax.fori_loop(0, i, lambda j, c: step(j, c, False), init)
    carry = step(i, carry, True)
    for h, (_, l, acc) in enumerate(carry):
        o_ref[:, h * MLA_V:(h + 1) * MLA_V] = (acc / l).astype(BF16)


def _flash(q, k, v, batch, seq):
    t = q.shape[0]
    tile = 512
    heads = 2
    nq = seq // tile
    return pl.pallas_call(
        functools.partial(_flash_body, tile=tile, heads=heads),
        grid=(batch, MLA_HEADS // heads, nq),
        in_specs=[
            pl.BlockSpec((tile, heads * MLA_HEAD_PAD), lambda b, h, i: (b * nq + i, h)),
            pl.BlockSpec((seq, heads * MLA_HEAD_PAD), lambda b, h, i: (b, h)),
            pl.BlockSpec((seq, heads * MLA_V), lambda b, h, i: (b, h)),
        ],
        out_specs=pl.BlockSpec((tile, heads * MLA_V), lambda b, h, i: (b * nq + i, h)),
        out_shape=jax.ShapeDtypeStruct((t, MLA_HEADS * MLA_V), BF16),
        compiler_params=_cp("parallel", "parallel", "arbitrary"),
        name="mla_attention",
    )(q, k, v)


S5_GROUPS_PER_BLOCK = LANES // S5_GROUP


def _s5_body(u_ref, w_ref, v_ref, bd_ref, ac_ref, bc_ref, y_ref, ucat, yx, *, n_chunks, n_steps):
    lc = S5_CHUNK
    for t in range(lc):
        ucat[:, t * LANES:(t + 1) * LANES] = u_ref[pl.ds(t, n_chunks, stride=lc), :].astype(BF16)
    z = _dot(ucat[...], w_ref[0])
    half = z.shape[1] // 2
    cidx = lax.broadcasted_iota(jnp.int32, z.shape, 0)
    for step in range(n_steps):
        d = 1 << step
        zs = jnp.where(cidx >= d, pltpu.roll(z, d, 0), 0.0)
        z = z + zs * ac_ref[0, step:step + 1, :] + pltpu.roll(zs, half, 1) * bc_ref[0, step:step + 1, :]
    xprev = jnp.where(cidx >= 1, pltpu.roll(z, 1, 0), 0.0)
    yx[...] = _dot(xprev.astype(BF16), v_ref[0])
    for t in range(lc):
        k = (t + 1) * LANES
        y = _dot(ucat[:, :k], bd_ref[0, (lc - 1 - t) * LANES:, :]) + yx[:, t * LANES:(t + 1) * LANES]
        y_ref[pl.ds(t, n_chunks, stride=lc), :] = y


def _s5_matrices(lam_re, lam_im, b_re, b_im, c_re, c_im, log_dt, n_steps):
    lc = S5_CHUNK
    gb = S5_GROUPS_PER_BLOCK
    lam = lax.complex(lam_re.astype(F32), lam_im.astype(F32))
    n_blk = lam.shape[0] // gb
    dt = jnp.exp(log_dt.astype(F32))[:, None]
    lam_bar = jnp.exp(lam * dt)
    b_bar = ((lam_bar - 1.0) / lam)[..., None] * lax.complex(b_re.astype(F32), b_im.astype(F32))
    c = lax.complex(c_re.astype(F32), c_im.astype(F32))
    j = jnp.arange(lc + 1, dtype=F32)[:, None, None]
    pows = jnp.exp(j * (lam * dt)[None])
    eye = jnp.eye(gb, dtype=F32)
    kern = jnp.einsum("gop,jgp,gpi->jgio", c, pows[:lc], b_bar).real
    kern = kern.reshape(lc, n_blk, gb, S5_GROUP, S5_GROUP)
    bd = jnp.einsum("jbgio,gh->bjgiho", kern, eye).reshape(n_blk, lc, LANES, LANES)
    bdrev = bd[:, ::-1].reshape(n_blk, lc * LANES, LANES)
    tt = jnp.arange(lc)
    wm = pows[lc - 1 - tt][..., None] * b_bar[None]
    wm = jnp.stack([wm.real, wm.imag], 2)
    wm = wm.reshape(lc, n_blk, gb, 2, S5_STATE, S5_GROUP)
    wcat = jnp.einsum("tbgrpi,gh->btgirhp", wm, eye).reshape(n_blk, lc * LANES, 2 * gb * S5_STATE)
    cv = c[None] * pows[1:lc + 1][:, :, None, :]
    cv = jnp.stack([cv.real, -cv.imag], 3)
    cv = cv.reshape(lc, n_blk, gb, S5_GROUP, 2, S5_STATE)
    vcat = jnp.einsum("tbgorp,gh->brgptho", cv, eye).reshape(n_blk, 2 * gb * S5_STATE, lc * LANES)
    stride = (lc * (2.0 ** jnp.arange(n_steps, dtype=F32)))[None, :, None]
    ap = jnp.exp(stride * (lam * dt)[:, None, :])

    def per_block(a):
        return a.reshape(n_blk, gb, n_steps, S5_STATE).transpose(0, 2, 1, 3).reshape(n_blk, n_steps, -1)

    ar, ai = per_block(ap.real), per_block(ap.imag)
    ac = jnp.concatenate([ar, ar], -1)
    bc = jnp.concatenate([-ai, ai], -1)
    return wcat.astype(BF16), vcat.astype(BF16), bdrev.astype(BF16), ac, bc


def _s5_scan(u, batch, seq, mats):
    wcat, vcat, bdrev, ac, bc = mats
    n_blk = wcat.shape[0]
    lc = S5_CHUNK
    nc = seq // lc
    n_steps = ac.shape[1]
    wide = wcat.shape[2]
    per_blk = lambda i, b: (i, 0, 0)
    return pl.pallas_call(
        functools.partial(_s5_body, n_chunks=nc, n_steps=n_steps),
        grid=(n_blk, batch),
        in_specs=[
            pl.BlockSpec((seq, LANES), lambda i, b: (b, i)),
            pl.BlockSpec((1, lc * LANES, wide), per_blk),
            pl.BlockSpec((1, wide, lc * LANES), per_blk),
            pl.BlockSpec((1, lc * LANES, LANES), per_blk),
            pl.BlockSpec((1, n_steps, wide), per_blk),
            pl.BlockSpec((1, n_steps, wide), per_blk),
        ],
        out_specs=pl.BlockSpec((seq, LANES), lambda i, b: (b, i)),
        out_shape=jax.ShapeDtypeStruct(u.shape, F32),
        scratch_shapes=[pltpu.VMEM((nc, lc * LANES), BF16), pltpu.VMEM((nc, lc * LANES), F32)],
        compiler_params=_cp("parallel", "arbitrary"),
        name="s5_scan",
    )(u, wcat, vcat, bdrev, ac, bc)


def _s5_glu_body(y_ref, u_ref, d_ref, w_ref, o_ref):
    y = y_ref[...] + d_ref[...] * u_ref[...]
    y = 0.5 * y * (1.0 + jnp.tanh(math.sqrt(2.0 / math.pi) * (y + 0.044715 * (y * y * y))))
    gate = jax.nn.sigmoid(_dot(y.astype(BF16), w_ref[...]))
    o_ref[...] = (y * gate).astype(BF16)


def _s5_glu(y, u, d_skip, w_glu):
    t, n = y.shape
    tm = 512
    row = lambda i: (i, 0)
    const = lambda i: (0, 0)
    return pl.pallas_call(
        _s5_glu_body,
        grid=(t // tm,),
        in_specs=[pl.BlockSpec((tm, n), row), pl.BlockSpec((tm, n), row),
                  pl.BlockSpec((1, n), const), pl.BlockSpec((n, n), const)],
        out_specs=pl.BlockSpec((tm, n), row),
        out_shape=jax.ShapeDtypeStruct((t, n), BF16),
        compiler_params=_cp("parallel"),
        name="s5_glu",
    )(y, u, d_skip.reshape(1, n), w_glu.astype(BF16))


def _outproj_body(a1_ref, a2_ref, w1_ref, w2_ref, x_ref, mod_ref, g_ref, b_ref, rwh_ref, rwl_ref,
                  xo_ref, h_ref, lg_ref):
    y = _dot(a1_ref[...], w1_ref[...]) + _dot(a2_ref[...], w2_ref[...])
    m = mod_ref[0]
    xn = _layernorm(DN_ALPHA * x_ref[...] + (1.0 + m[2:3]) * y, g_ref[...], b_ref[...])
    xo_ref[...] = xn
    h = xn * (1.0 + m[4:5]) + m[3:4]
    h_ref[...] = _pack_halves(h)
    h_hi, h_lo = _split_bf16(h, 2)
    lg_ref[...] = _dot(h_hi, rwh_ref[...]) + (_dot(h_lo, rwh_ref[...]) + _dot(h_hi, rwl_ref[...]))


def _outproj(a1, a2, w_out, x2, mod, ln_g, ln_b, router_w, seq):
    t, d = x2.shape
    k1 = a1.shape[1]
    k2 = a2.shape[1]
    tm = 256
    per_b = seq // tm
    w1 = w_out[:k1].astype(BF16)
    w2 = w_out[k1:].astype(BF16)
    rwh, rwl = _split_bf16(router_w, 2)
    ne = router_w.shape[1]
    row = lambda i: (i, 0)
    const = lambda i: (0, 0)
    return pl.pallas_call(
        _outproj_body,
        grid=(t // tm,),
        in_specs=[
            pl.BlockSpec((tm, k1), row), pl.BlockSpec((tm, k2), row),
            pl.BlockSpec((k1, d), const), pl.BlockSpec((k2, d), const),
            pl.BlockSpec((tm, d), row),
            pl.BlockSpec((1, 6, d), lambda i: (i // per_b, 0, 0)),
            pl.BlockSpec((1, d), const), pl.BlockSpec((1, d), const),
            pl.BlockSpec((d, ne), const), pl.BlockSpec((d, ne), const),
        ],
        out_specs=[pl.BlockSpec((tm, d), row), pl.BlockSpec((tm, d // 2), row), pl.BlockSpec((tm, ne), row)],
        out_shape=[jax.ShapeDtypeStruct((t, d), F32), jax.ShapeDtypeStruct((t, d // 2), jnp.uint32),
                   jax.ShapeDtypeStruct((t, ne), F32)],
        compiler_params=_cp("parallel"),
        name="outproj_ln",
    )(a1, a2, w1, w2, x2, mod, ln_g.reshape(1, d), ln_b.reshape(1, d), rwh, rwl)


def _route_body(lg_ref, rb_ref, cls_ref, glo_ref, ghi_ref):
    s = jax.nn.sigmoid(lg_ref[...])
    sb = s + rb_ref[...]
    rows = [sb[e:e + 1, :] for e in range(N_EXPERTS)]
    raw = [s[e:e + 1, :] for e in range(N_EXPERTS)]
    gscore = []
    for g in range(N_EXPERT_GROUPS):
        v = rows[g * EXPERTS_PER_GROUP:(g + 1) * EXPERTS_PER_GROUP]
        best = None
        for a in range(EXPERTS_PER_GROUP):
            for b in range(a + 1, EXPERTS_PER_GROUP):
                pair = v[a] + v[b]
                best = pair if best is None else jnp.maximum(best, pair)
        gscore.append(best)
    bg = jnp.zeros_like(gscore[0], dtype=jnp.int32)
    bs = gscore[0]
    for g in range(1, N_EXPERT_GROUPS):
        upd = gscore[g] > bs
        bg = jnp.where(upd, g, bg)
        bs = jnp.where(upd, gscore[g], bs)

    def pick(vals, k):
        out = vals[k]
        for g in range(1, N_EXPERT_GROUPS):
            out = jnp.where(bg == g, vals[g * EXPERTS_PER_GROUP + k], out)
        return out

    v = [pick(rows, k) for k in range(EXPERTS_PER_GROUP)]
    r = [pick(raw, k) for k in range(EXPERTS_PER_GROUP)]
    i1 = jnp.zeros_like(bg)
    b1 = v[0]
    for k in range(1, EXPERTS_PER_GROUP):
        upd = v[k] > b1
        i1 = jnp.where(upd, k, i1)
        b1 = jnp.where(upd, v[k], b1)
    i2 = jnp.zeros_like(bg)
    b2 = jnp.full_like(b1, -jnp.inf)
    for k in range(EXPERTS_PER_GROUP):
        upd = (i1 != k) & (v[k] > b2)
        i2 = jnp.where(upd, k, i2)
        b2 = jnp.where(upd, v[k], b2)
    s1 = r[0]
    s2 = r[0]
    for k in range(1, EXPERTS_PER_GROUP):
        s1 = jnp.where(i1 == k, r[k], s1)
        s2 = jnp.where(i2 == k, r[k], s2)
    den = s1 + s2
    g1 = s1 / den
    g2 = s2 / den
    lo = jnp.minimum(i1, i2)
    hi = jnp.maximum(i1, i2)
    base = jnp.where(lo == 0, 0, jnp.where(lo == 1, 3, 5))
    cls_ref[...] = bg * N_PAIRS + base + hi - lo - 1
    first_lo = i1 < i2
    glo_ref[...] = jnp.where(first_lo, g1, g2)
    ghi_ref[...] = jnp.where(first_lo, g2, g1)


def _route(logits, router_b):
    t, ne = logits.shape
    tt = min(2048, t)
    col = lambda i: (0, i)
    cls, glo, ghi = pl.pallas_call(
        _route_body,
        grid=(t // tt,),
        in_specs=[pl.BlockSpec((ne, tt), col), pl.BlockSpec((ne, 1), lambda i: (0, 0))],
        out_specs=[pl.BlockSpec((1, tt), col)] * 3,
        out_shape=[jax.ShapeDtypeStruct((1, t), jnp.int32), jax.ShapeDtypeStruct((1, t), F32),
                   jax.ShapeDtypeStruct((1, t), F32)],
        compiler_params=_cp("parallel"),
        name="moe_route",
    )(logits.T, router_b.reshape(ne, 1).astype(F32))
    return cls[0], glo[0], ghi[0]


def _moe_body(e1_ref, e2_ref, valid_ref, h_ref, g_ref, wg1_ref, wg2_ref, wd1_ref, wd2_ref, o_ref):
    i = pl.program_id(0)

    @pl.when(valid_ref[i] == 1)
    def _():
        h_lo, h_hi = (v.astype(BF16) for v in _unpack_halves(h_ref[...]))
        half = h_lo.shape[1]
        g = g_ref[...]

        def expert(wg_ref, wd_ref):
            gu = _dot(h_lo, wg_ref[0, 0, :half, :]) + _dot(h_hi, wg_ref[0, 0, half:, :])
            gt = gu[:, :D_EXPERT]
            act = (gt * jax.nn.sigmoid(gt) * gu[:, D_EXPERT:]).astype(BF16)
            return _dot(act, wd_ref[0, 0])

        y = g[:, 0:1] * expert(wg1_ref, wd1_ref) + g[:, 1:2] * expert(wg2_ref, wd2_ref)
        o_ref[...] = _pack_halves(y)

    @pl.when(valid_ref[i] == 0)
    def _():
        o_ref[...] = jnp.zeros_like(o_ref)


_PAIR_LO = (0, 0, 0, 1, 1, 2)
_PAIR_HI = (1, 2, 3, 2, 3, 3)


def _moe(h, cls, glo, ghi, wgu, wdn, layer):
    t = h.shape[0]
    d = 2 * h.shape[1]
    tm = MOE_TILE
    n_tiles = t // tm + N_CLASSES
    rows = n_tiles * tm
    onehot = (cls[:, None] == jnp.arange(N_CLASSES)[None, :]).astype(jnp.int32)
    csum = jnp.cumsum(onehot, 0)
    rank = jnp.take_along_axis(csum, cls[:, None], 1)[:, 0] - 1
    counts = csum[-1]
    padded = -(-counts // tm) * tm
    ends = jnp.cumsum(padded)
    dest = (ends - padded)[cls] + rank
    src = (jnp.arange(rows, dtype=jnp.int32) % t).at[dest].set(jnp.arange(t, dtype=jnp.int32))
    tile_start = jnp.arange(n_tiles, dtype=jnp.int32) * tm
    valid = (tile_start < ends[-1]).astype(jnp.int32)
    last_cls = jnp.max(jnp.where(counts > 0, jnp.arange(N_CLASSES), 0))
    tile_cls = jnp.minimum(jnp.searchsorted(ends, tile_start, side="right"), last_cls).astype(jnp.int32)
    grp = tile_cls // N_PAIRS
    pair = tile_cls % N_PAIRS
    e1 = (grp * EXPERTS_PER_GROUP + jnp.asarray(_PAIR_LO, jnp.int32)[pair]).astype(jnp.int32)
    e2 = (grp * EXPERTS_PER_GROUP + jnp.asarray(_PAIR_HI, jnp.int32)[pair]).astype(jnp.int32)
    hs = _gather_rows(h, src)
    gs = jnp.stack([glo, ghi], 1).at[src].get(mode="promise_in_bounds")
    ys = pl.pallas_call(
        _moe_body,
        grid_spec=pltpu.PrefetchScalarGridSpec(
            num_scalar_prefetch=3,
            grid=(n_tiles,),
            in_specs=[
                pl.BlockSpec((tm, d // 2), lambda i, e1, e2, va: (i, 0)),
                pl.BlockSpec((tm, 2), lambda i, e1, e2, va: (i, 0)),
                pl.BlockSpec((1, 1, d, 2 * D_EXPERT), lambda i, e1, e2, va: (layer, e1[i], 0, 0)),
                pl.BlockSpec((1, 1, d, 2 * D_EXPERT), lambda i, e1, e2, va: (layer, e2[i], 0, 0)),
                pl.BlockSpec((1, 1, D_EXPERT, d), lambda i, e1, e2, va: (layer, e1[i], 0, 0)),
                pl.BlockSpec((1, 1, D_EXPERT, d), lambda i, e1, e2, va: (layer, e2[i], 0, 0)),
            ],
            out_specs=pl.BlockSpec((tm, d // 2), lambda i, e1, e2, va: (i, 0)),
        ),
        out_shape=jax.ShapeDtypeStruct((rows, d // 2), jnp.uint32),
        compiler_params=_cp("arbitrary"),
        name="moe_experts",
    )(e1, e2, valid, hs, gs, wgu, wgu, wdn, wdn)
    return _gather_rows(ys, dest.astype(jnp.int32))


def _resid_ln_body(x_ref, y_ref, mod_ref, g_ref, b_ref, o_ref):
    m = mod_ref[0]
    y = jnp.concatenate(_unpack_halves(y_ref[...]), 1)
    o_ref[...] = _layernorm(DN_ALPHA * x_ref[...] + (1.0 + m[5:6]) * y, g_ref[...], b_ref[...])


def _resid_ln(x2, y, mod, ln_g, ln_b, seq):
    t, d = x2.shape
    tm = 512
    per_b = seq // tm
    row = lambda i: (i, 0)
    const = lambda i: (0, 0)
    return pl.pallas_call(
        _resid_ln_body,
        grid=(t // tm,),
        in_specs=[pl.BlockSpec((tm, d), row), pl.BlockSpec((tm, d // 2), row),
                  pl.BlockSpec((1, 6, d), lambda i: (i // per_b, 0, 0)),
                  pl.BlockSpec((1, d), const), pl.BlockSpec((1, d), const)],
        out_specs=pl.BlockSpec((tm, d), row),
        out_shape=jax.ShapeDtypeStruct((t, d), F32),
        compiler_params=_cp("parallel"),
        name="resid_ln",
    )(x2, y, mod, ln_g.reshape(1, d), ln_b.reshape(1, d))


def _front1_body(x_ref, mod_ref, w_ref, zc0_ref, zc1_ref, zc2_ref, zr_ref, h_scr, res_scr, *, dils, tm):
    j = pl.program_id(1)
    n_att = 3 * len(dils)
    half = w_ref.shape[1] // 2

    @pl.when(j == 0)
    def _():
        m = mod_ref[0]
        h_scr[...] = (x_ref[...] * (1.0 + m[1:2]) + m[0:1]).astype(BF16)

    def column_half(c):
        return _dot(h_scr[...], w_ref[:, c * half:(c + 1) * half])

    @pl.when(j >= n_att)
    def _():
        for c in range(2):
            zr_ref[:, c * half:(c + 1) * half] = column_half(c).astype(BF16)

    for g, (o_ref, dil) in enumerate(zip((zc0_ref, zc1_ref, zc2_ref), dils)):
        @pl.when(j // 3 == g)
        def _(o_ref=o_ref, dil=dil):
            if dil == 1:
                for c in range(2):
                    o_ref[0, 0, :, c * half:(c + 1) * half] = column_half(c).astype(BF16)
                return
            per_half = half // LANES
            for c in range(2):
                res = column_half(c)
                for s in range(per_half):
                    res_scr[c * per_half + s] = res[:, s * LANES:(s + 1) * LANES]
            rows = tm // dil
            for r in range(dil):
                for s in range(2 * per_half):
                    o_ref[0, r, :, s * LANES:(s + 1) * LANES] = (
                        res_scr[s, pl.ds(r, rows, stride=dil), :].astype(BF16))


def _front1(x2, mod, od_w_in, batch, seq):
    t, d = x2.shape
    tm = min(1024, seq)
    tn = DIL_WIDTH
    per_b = seq // tm
    n_in = od_w_in.shape[1]
    n = -(-n_in // tn) * tn
    w = jnp.pad(od_w_in, ((0, 0), (0, n - n_in))).astype(BF16)
    dils = tuple(dil for _, dil in DIL_PATTERNS)
    n_att = 3 * len(dils)
    n_r = n // tn - n_att

    def att_spec(g, dil):
        return pl.BlockSpec((1, dil, tm // dil, tn),
                            lambda i, j: (i // per_b, 0, i % per_b, jnp.clip(j - 3 * g, 0, 2)))

    return pl.pallas_call(
        functools.partial(_front1_body, dils=dils, tm=tm),
        grid=(t // tm, n // tn),
        in_specs=[pl.BlockSpec((tm, d), lambda i, j: (i, 0)),
                  pl.BlockSpec((1, 6, d), lambda i, j: (i // per_b, 0, 0)),
                  pl.BlockSpec((d, tn), lambda i, j: (0, j))],
        out_specs=[att_spec(g, dil) for g, dil in enumerate(dils)]
        + [pl.BlockSpec((tm, tn), lambda i, j: (i, jnp.clip(j - n_att, 0, n_r - 1)))],
        out_shape=[jax.ShapeDtypeStruct((batch, dil, seq // dil, 3 * tn), BF16) for dil in dils]
        + [jax.ShapeDtypeStruct((t, n_r * tn), BF16)],
        scratch_shapes=[pltpu.VMEM((tm, d), BF16), pltpu.VMEM((tn // LANES, tm, LANES), F32)],
        compiler_params=_cp("parallel", "arbitrary"),
        name="front1",
    )(x2, mod, w)


def _dil_body(q_ref, kp_ref, kc_ref, vp_ref, vc_ref, bias_ref, o_ref, lse_ref):
    n = pl.program_id(2)
    span = DIL_SPAN
    q = q_ref[0, 0]
    k = jnp.concatenate([kp_ref[0, 0], kc_ref[0, 0]], 0)
    v = jnp.concatenate([vp_ref[0, 0], vc_ref[0, 0]], 0)
    qi = lax.broadcasted_iota(jnp.int32, (span, 2 * span), 0)
    ku = lax.broadcasted_iota(jnp.int32, (span, 2 * span), 1)
    valid = (ku >= qi) & (ku <= qi + span) & ((n >= 1) | (ku >= span))
    lane = lax.broadcasted_iota(jnp.int32, (span, LANES), 1)
    first = lane < DIL_HEAD_DIM
    lse_all = jnp.zeros((span, LANES), F32)
    scale = DIL_HEAD_DIM ** -0.5
    for hp in range(DIL_HEADS // 2):
        sl = slice(hp * LANES, (hp + 1) * LANES)
        q2, k2, v2 = q[:, sl], k[:, sl], v[:, sl]
        outs = []
        for hh in range(2):
            head = 2 * hp + hh
            qm = jnp.where(first if hh == 0 else jnp.logical_not(first), q2, jnp.zeros_like(q2))
            s = _dot_nt(qm, k2) * scale + bias_ref[head]
            s = jnp.where(valid, s, NEG_INF)
            m = jnp.max(s, -1, keepdims=True)
            p = jnp.exp(s - m)
            l = jnp.sum(p, -1, keepdims=True)
            outs.append(_dot(p.astype(BF16), v2) / l)
            lse_all = jnp.where(lane == head, m + jnp.log(l), lse_all)
        o_ref[0, 0, :, sl] = jnp.where(first, outs[0], outs[1]).astype(BF16)
    lse_ref[0, 0] = lse_all


def _t5_bucket(dist):
    exact = T5_BUCKETS // 2
    logd = jnp.log(jnp.maximum(dist, 1).astype(F32) / exact) / math.log(T5_MAX_DIST / exact)
    large = jnp.minimum(exact + (logd * (T5_BUCKETS - exact)).astype(jnp.int32), T5_BUCKETS - 1)
    return jnp.where(dist < exact, dist, large)


def _dilated_group(zc, rel_bias, gi, dil, batch, seq):
    span = DIL_SPAN
    l_stream = seq // dil
    nb = l_stream // span
    qi = jnp.arange(span)[:, None]
    ku = jnp.arange(2 * span)[None, :]
    delta = jnp.clip(span + qi - ku, 0, span) * dil
    onehot = (_t5_bucket(delta)[None] == jnp.arange(T5_BUCKETS)[:, None, None]).astype(F32)
    table = rel_bias.astype(F32)[:, gi * DIL_HEADS:(gi + 1) * DIL_HEADS]
    bias = jnp.einsum("kh,kqu->hqu", table, onehot, precision=lax.Precision.HIGHEST)

    def cur(j):
        return lambda b, r, n: (b, r, n, j)

    def prev(j):
        return lambda b, r, n: (b, r, jnp.maximum(n - 1, 0), j)

    blk = (1, 1, span, DIL_WIDTH)
    return pl.pallas_call(
        _dil_body,
        grid=(batch, dil, nb),
        in_specs=[pl.BlockSpec(blk, cur(0)), pl.BlockSpec(blk, prev(1)), pl.BlockSpec(blk, cur(1)),
                  pl.BlockSpec(blk, prev(2)), pl.BlockSpec(blk, cur(2)),
                  pl.BlockSpec((DIL_HEADS, span, 2 * span), lambda b, r, n: (0, 0, 0))],
        out_specs=[pl.BlockSpec(blk, cur(0)), pl.BlockSpec((1, 1, span, LANES), cur(0))],
        out_shape=[jax.ShapeDtypeStruct((batch, dil, l_stream, DIL_WIDTH), BF16),
                   jax.ShapeDtypeStruct((batch, dil, l_stream, LANES), F32)],
        compiler_params=_cp("parallel", "parallel", "arbitrary"),
        name=f"dilated_attention_{dil}",
    )(zc, zc, zc, zc, zc, bias)


def _dil_merge_body(o1_ref, o2_ref, o3_ref, l1_ref, l2_ref, l3_ref, e_ref, out_ref, o_scr, l1_scr, l2_scr,
                    l3_scr, *, dils, tm):
    def unstream(ref, scr, dil):
        rows = tm // dil
        n_slab = scr.shape[0]
        for r in range(dil):
            x = ref[0, r].astype(F32)
            for c in range(n_slab):
                scr[c, pl.ds(r, rows, stride=dil), :] = x[:, c * LANES:(c + 1) * LANES]
        return jnp.concatenate([scr[c] for c in range(n_slab)], 1)

    lses = [unstream(ref, scr, dil) for ref, scr, dil in
            zip((l1_ref, l2_ref, l3_ref), (l1_scr, l2_scr, l3_scr), dils)]
    top = jnp.maximum(jnp.maximum(lses[0], lses[1]), lses[2])
    num = None
    den = None
    for o_ref, lse, dil in zip((o1_ref, o2_ref, o3_ref), lses, dils):
        w = _dot_exact_rhs(jnp.exp(lse - top), e_ref[...], 2)
        o = unstream(o_ref, o_scr, dil)
        num = w * o if num is None else num + w * o
        den = w if den is None else den + w
    out_ref[...] = (num / den).astype(BF16)


def _dil_merge(outs, lses, batch, seq):
    tm = 512
    per_b = seq // tm
    dils = tuple(dil for _, dil in DIL_PATTERNS)
    head_of_lane = jnp.arange(DIL_WIDTH) // DIL_HEAD_DIM
    expand = (jnp.arange(LANES)[:, None] == head_of_lane[None, :]).astype(BF16)

    def spec(dil, width):
        return pl.BlockSpec((1, dil, tm // dil, width), lambda i: (i // per_b, 0, i % per_b, 0))

    return pl.pallas_call(
        functools.partial(_dil_merge_body, dils=dils, tm=tm),
        grid=(batch * per_b,),
        in_specs=[spec(dil, DIL_WIDTH) for dil in dils] + [spec(dil, LANES) for dil in dils]
        + [pl.BlockSpec((LANES, DIL_WIDTH), lambda i: (0, 0))],
        out_specs=pl.BlockSpec((tm, DIL_WIDTH), lambda i: (i, 0)),
        out_shape=jax.ShapeDtypeStruct((batch * seq, DIL_WIDTH), BF16),
        scratch_shapes=[pltpu.VMEM((DIL_WIDTH // LANES, tm, LANES), F32)] + [pltpu.VMEM((1, tm, LANES), F32)] * 3,
        compiler_params=_cp("parallel"),
        name="dilated_merge",
    )(*outs, *lses, expand)


def _rwkv_body(zr_ref, zk_ref, zv_ref, zm_ref, mur_ref, muk_ref, muv_ref, mum_ref, w0_ref, w2_ref,
               a0_ref, a2_ref, g2_ref, kk_ref, ka_ref, rk_ref, lng_ref, lnb_ref, o_ref,
               st_ref, pr_ref, pk_ref, pv_ref, pm_ref, xr_s, xk_s, xv_s, lw_s, as_s, g_s, cum_s):
    lc = RW_CHUNK
    n_slabs = st_ref.shape[0]

    @pl.when(pl.program_id(1) == 0)
    def _():
        st_ref[...] = jnp.zeros_like(st_ref)
        pr_ref[...] = jnp.zeros_like(pr_ref)
        pk_ref[...] = jnp.zeros_like(pk_ref)
        pv_ref[...] = jnp.zeros_like(pv_ref)
        pm_ref[...] = jnp.zeros_like(pm_ref)

    def shifted(z_ref, prev_ref, mu_ref):
        z = z_ref[...].astype(F32)
        first_row = lax.broadcasted_iota(jnp.int32, z.shape, 0) == 0
        prev = jnp.where(first_row, prev_ref[...], pltpu.roll(z, 1, 0))
        prev_ref[...] = z[lc - 1:lc, :]
        return z + (prev - z) * mu_ref[...]

    xr_s[...] = shifted(zr_ref, pr_ref, mur_ref)
    xk_s[...] = shifted(zk_ref, pk_ref, muk_ref)
    xv_s[...] = shifted(zv_ref, pv_ref, muv_ref)
    xm = shifted(zm_ref, pm_ref, mum_ref)
    lora_in = xm[:, :LANES]
    wl = w0_ref[...] + _dot(jnp.tanh(lora_in).astype(BF16), w2_ref[...])
    w_log = -(jnp.maximum(-wl, 0.0) + jnp.log(1.0 + jnp.exp(-jnp.abs(wl)))) - 0.5
    lw = -jnp.exp(w_log)
    lw_s[...] = lw
    as_s[...] = jax.nn.sigmoid(a0_ref[...] + _dot(lora_in.astype(BF16), a2_ref[...]))
    g_s[...] = _dot(jax.nn.sigmoid(xm[:, LANES:]).astype(BF16), g2_ref[...])
    ti = lax.broadcasted_iota(jnp.int32, (lc, lc), 0)
    si = lax.broadcasted_iota(jnp.int32, (lc, lc), 1)
    cum_s[...] = _dot_exact_lhs((si <= ti).astype(BF16), lw, 3)

    row = lax.broadcasted_iota(jnp.int32, (2 * lc, 2 * lc), 0)
    colm = lax.broadcasted_iota(jnp.int32, (2 * lc, 2 * lc), 1)
    same_head = (row >= lc) == (colm >= lc)
    t_in = row & (lc - 1)
    s_in = colm & (lc - 1)
    strict = same_head & (s_in < t_in)
    incl = same_head & (s_in <= t_in)
    eye = row == colm
    ones_bd = same_head.astype(BF16)
    lane = lax.broadcasted_iota(jnp.int32, (lc, LANES), 1)
    first = lane < RW_HEAD

    def seg(x):
        return _dot_exact_rhs(x, ones_bd, 2)

    def stack_heads(x):
        return jnp.concatenate([jnp.where(first, x, 0.0), jnp.where(first, 0.0, x)], 0)

    slabs = range(n_slabs)
    sls = [slice(hp * LANES, (hp + 1) * LANES) for hp in slabs]
    kk_raw = [xk_s[:, sl] * kk_ref[:, sl] for sl in sls]
    kk_ss = [seg(kk * kk) for kk in kk_raw]
    lhs, rhs, bk_t, v_bs, k2s, pl_cols = [], [], [], [], [], []
    for hp, sl in enumerate(sls):
        a_sig = as_s[:, sl]
        cum = cum_s[:, sl]
        kk = kk_raw[hp] / jnp.maximum(jnp.sqrt(kk_ss[hp]), 1e-12)
        k2 = xk_s[:, sl] * (1.0 + (a_sig - 1.0) * ka_ref[:, sl])
        b_p = kk * a_sig
        cl = cum[lc - 1:lc, :]
        e_neg = jnp.exp(-cum)
        e_end = jnp.exp(cl - cum)
        at = -kk * jnp.exp(cum - lw_s[:, sl])
        rt = xr_s[:, sl] * jnp.exp(cum)
        lhs.append(jnp.concatenate([stack_heads(at), stack_heads(rt)], 0).astype(BF16))
        bt = (b_p * e_neg).astype(BF16)
        kt = (k2 * e_neg).astype(BF16)
        rhs.append(jnp.concatenate([bt, bt, kt, kt], 0))
        bk_t.append(jnp.concatenate([b_p * e_end, k2 * e_end], 0).T.astype(BF16))
        v_bs.append(xv_s[:, sl].astype(BF16))
        k2s.append(k2)
        pl_cols.append(jnp.sum(jnp.where(eye, jnp.broadcast_to(jnp.exp(cl), (2 * lc, LANES)), 0.0), 1,
                               keepdims=True))
    qms = [_dot_nt(lhs[hp], rhs[hp]) for hp in slabs]
    states = [st_ref[hp] for hp in slabs]
    arss = [_dot(lhs[hp], states[hp].astype(BF16)) for hp in slabs]
    a_abs = [jnp.where(strict, qm[:2 * lc, :2 * lc], 0.0).astype(BF16) for qm in qms]
    xs = [jnp.where(eye, 1.0, 0.0) + a.astype(F32) for a in a_abs]
    ps = [_dot(a, a).astype(BF16) for a in a_abs]
    n_double = int(math.log2(lc)) - 1
    for it in range(n_double - 1):
        res = [_dot(ps[hp], jnp.concatenate([xs[hp].astype(BF16), ps[hp]], 1)) for hp in slabs]
        xs = [xs[hp] + res[hp][:, :2 * lc] for hp in slabs]
        ps = [res[hp][:, 2 * lc:].astype(BF16) for hp in slabs]
    xs = [xs[hp] + _dot(ps[hp], xs[hp].astype(BF16)) for hp in slabs]
    vss = [jnp.concatenate([v_b, v_b], 0) for v_b in v_bs]
    gmats = [_dot(jnp.where(strict, qms[hp][:2 * lc, 2 * lc:], 0.0).astype(BF16), vss[hp]) + arss[hp][:2 * lc]
             for hp in slabs]
    uss = [_dot(xs[hp].astype(BF16), gmats[hp].astype(BF16)) for hp in slabs]
    incl2 = jnp.concatenate([incl, incl], 1)
    yss = [_dot(jnp.where(incl2, qms[hp][2 * lc:, :], 0.0).astype(BF16),
                jnp.concatenate([uss[hp].astype(BF16), vss[hp]], 0)) + arss[hp][2 * lc:] for hp in slabs]
    us_p = [jnp.where(first, us[:lc], us[lc:]) for us in uss]
    ys_p = [jnp.where(first, ys[:lc], ys[lc:]) for ys in yss]
    upds = [_dot(bk_t[hp], jnp.concatenate([us_p[hp].astype(BF16), v_bs[hp]], 0)) for hp in slabs]
    for hp in slabs:
        st_ref[hp] = pl_cols[hp] * states[hp] + jnp.where(same_head, upds[hp], 0.0)
    means = [seg(y) * (1.0 / RW_HEAD) for y in ys_p]
    dys = [ys_p[hp] - means[hp] for hp in slabs]
    variances = [seg(dy * dy) * (1.0 / RW_HEAD) for dy in dys]
    bonus = [seg(xr_s[:, sl] * k2s[hp] * rk_ref[:, sl]) for hp, sl in enumerate(sls)]
    for hp, sl in enumerate(sls):
        yn = dys[hp] * lax.rsqrt(variances[hp] + RW_GN_EPS) * lng_ref[:, sl] + lnb_ref[:, sl]
        o_ref[:, sl] = ((yn + bonus[hp] * xv_s[:, sl]) * g_s[:, sl]).astype(BF16)


def _rwkv(z, batch, seq, mu, w0, w2, a0, a2, g2, k_k, k_a, r_k, lnx_g, lnx_b):
    t = z.shape[0]
    lc = RW_CHUNK
    nc = seq // lc
    width = w0.shape[0]
    n_slabs = width // LANES
    col0 = 0
    misc_blk = 3 * width // RW_MISC
    n_lora = RW_LORA_W + RW_LORA_A + RW_LORA_G
    mu_m = jnp.pad(mu[3 * width:], (0, RW_MISC - n_lora)).reshape(1, RW_MISC)
    w2p = jnp.pad(w2, ((0, LANES - RW_LORA_W), (0, 0))).astype(BF16)
    a2p = jnp.pad(a2, ((RW_LORA_W, LANES - RW_LORA_W - RW_LORA_A), (0, 0))).astype(BF16)
    g2p = jnp.pad(g2, ((0, RW_MISC - LANES - RW_LORA_G), (0, 0))).astype(BF16)
    vec = lambda a: a.reshape(1, width).astype(F32)
    const = lambda b, c: (0, 0)
    wide = lambda: pl.BlockSpec((1, width), const)
    return pl.pallas_call(
        _rwkv_body,
        grid=(batch, nc),
        in_specs=[
            pl.BlockSpec((lc, width), lambda b, c: (b * nc + c, col0)),
            pl.BlockSpec((lc, width), lambda b, c: (b * nc + c, col0 + 1)),
            pl.BlockSpec((lc, width), lambda b, c: (b * nc + c, col0 + 2)),
            pl.BlockSpec((lc, RW_MISC), lambda b, c: (b * nc + c, misc_blk)),
            wide(), wide(), wide(), pl.BlockSpec((1, RW_MISC), const),
            wide(), pl.BlockSpec((LANES, width), const),
            wide(), pl.BlockSpec((LANES, width), const),
            pl.BlockSpec((RW_MISC - LANES, width), const),
            wide(), wide(), wide(), wide(), wide(),
        ],
        out_specs=pl.BlockSpec((lc, width), lambda b, c: (b * nc + c, 0)),
        out_shape=jax.ShapeDtypeStruct((t, width), BF16),
        scratch_shapes=[pltpu.VMEM((n_slabs, LANES, LANES), F32),
                        pltpu.VMEM((1, width), F32), pltpu.VMEM((1, width), F32),
                        pltpu.VMEM((1, width), F32), pltpu.VMEM((1, RW_MISC), F32)]
        + [pltpu.VMEM((lc, width), F32)] * 7,
        compiler_params=_cp("parallel", "arbitrary"),
        name="rwkv7",
    )(z, z, z, z, vec(mu[:width]), vec(mu[width:2 * width]), vec(mu[2 * width:3 * width]), mu_m,
      vec(w0), w2p, vec(a0), a2p, g2p, vec(k_k), vec(k_a), vec(r_k), vec(lnx_g), vec(lnx_b))


def kernel(x, c, ada_w, ada_b, ln_mix_g, ln_mix_b, ln_ffn_g, ln_ffn_b, router_w, router_b, moe_w_gate_up, moe_w_down, rel_bias, ev_w_in, mla_q_norm, mla_w_uq, mla_kv_norm, mla_w_ukv, s5_lambda_re, s5_lambda_im, s5_b_re, s5_b_im, s5_c_re, s5_c_im, s5_d, s5_log_dt, s5_w_glu, ev_w_out, od_w_in, rw_mu, rw_w0, rw_w2, rw_a0, rw_a2, rw_g2, rw_k_k, rw_k_a, rw_r_k, rw_lnx_g, rw_lnx_b, od_w_out):
    batch, seq, d = x.shape
    assert seq % 2048 == 0 and d % LANES == 0
    x2 = x.reshape(batch * seq, d)
    mods = _ada(c, ada_w, ada_b)
    wgu_all = _cast_bf16(moe_w_gate_up, 2 * moe_w_gate_up.shape[-2])
    wdn_all = _cast_bf16(moe_w_down, 4 * moe_w_down.shape[-2])
    for layer in range(DEPTH):
        mod = mods[layer]
        if layer % 2 == 0:
            e = layer // 2
            q, k, v, u = _front0(x2, mod, seq, ev_w_in[e], mla_q_norm[e], mla_w_uq[e], mla_kv_norm[e],
                                 mla_w_ukv[e])
            att = _flash(q, k, v, batch, seq)
            n_steps = max(1, math.ceil(math.log2(seq // S5_CHUNK)))
            mats = _s5_matrices(s5_lambda_re[e], s5_lambda_im[e], s5_b_re[e], s5_b_im[e], s5_c_re[e],
                                s5_c_im[e], s5_log_dt[e], n_steps)
            ssm = _s5_glu(_s5_scan(u, batch, seq, mats), u, s5_d[e], s5_w_glu[e])
            x2, h, logits = _outproj(att, ssm, ev_w_out[e], x2, mod, ln_mix_g[layer], ln_mix_b[layer],
                                     router_w, seq)
        else:
            o = layer // 2
            *zcs, z = _front1(x2, mod, od_w_in[o], batch, seq)
            outs, lses = [], []
            for gi, (win, dil) in enumerate(DIL_PATTERNS):
                assert win // dil == DIL_SPAN
                og, lg = _dilated_group(zcs[gi], rel_bias, gi, dil, batch, seq)
                outs.append(og)
                lses.append(lg)
            att = _dil_merge(outs, lses, batch, seq)
            tm_out = _rwkv(z, batch, seq, rw_mu[o], rw_w0[o], rw_w2[o], rw_a0[o], rw_a2[o], rw_g2[o],
                           rw_k_k[o], rw_k_a[o], rw_r_k[o], rw_lnx_g[o], rw_lnx_b[o])
            x2, h, logits = _outproj(att, tm_out, od_w_out[o], x2, mod, ln_mix_g[layer], ln_mix_b[layer],
                                     router_w, seq)
        cls, glo, ghi = _route(logits, router_b)
        y = _moe(h, cls, glo, ghi, wgu_all, wdn_all, layer)
        x2 = _resid_ln(x2, y, mod, ln_ffn_g[layer], ln_ffn_b[layer], seq)
    return x2.reshape(batch, seq, d)
```

```python
import functools
import math

import jax
import jax.numpy as jnp
from jax import lax
from jax.experimental import pallas as pl
from jax.experimental.pallas import tpu as pltpu

F32 = jnp.float32
BF16 = jnp.bfloat16

DEPTH = 2
DN_ALPHA = (2.0 * DEPTH) ** 0.25
LN_EPS = 1e-5
RMS_EPS = 1e-6
NEG_INF = -1e30

MLA_HEADS = 8
MLA_NOPE = 128
MLA_ROPE = 64
MLA_V = 128
MLA_QK = MLA_NOPE + MLA_ROPE
MLA_Q_RANK = 512
MLA_KV_RANK = 256
ROPE_THETA = 10000.0
MLA_HEAD_PAD = 256

S5_GROUP = 16
S5_STATE = 64
S5_CHUNK = 16

DIL_PATTERNS = ((128, 1), (512, 4), (2048, 16))
DIL_SPAN = 128
DIL_HEADS = 8
DIL_HEAD_DIM = 64
DIL_WIDTH = DIL_HEADS * DIL_HEAD_DIM
DIL_IN = len(DIL_PATTERNS) * 3 * DIL_WIDTH
T5_BUCKETS = 32
T5_MAX_DIST = 2048

RW_HEAD = 64
RW_LORA_W = 64
RW_LORA_A = 64
RW_LORA_G = 224
RW_GN_EPS = 64e-5
RW_CHUNK = 64
RW_MISC = 512

N_EXPERTS = 16
N_EXPERT_GROUPS = 4
EXPERTS_PER_GROUP = 4
D_EXPERT = 512
N_PAIRS = 6
N_CLASSES = N_EXPERT_GROUPS * N_PAIRS
MOE_TILE = 256

LANES = 128
VMEM_LIMIT = 56 * 1024 * 1024


def _cp(*sem):
    return pltpu.CompilerParams(dimension_semantics=sem, vmem_limit_bytes=VMEM_LIMIT)


def _dot(a, b):
    return jnp.dot(a, b, preferred_element_type=F32)


def _dot_nt(a, b):
    return lax.dot_general(a, b, (((1,), (1,)), ((), ())), preferred_element_type=F32)


def _split_bf16(x, parts):
    out = []
    for _ in range(parts):
        hi = x.astype(BF16)
        out.append(hi)
        x = x - hi.astype(F32)
    return out


def _dot_exact_rhs(x, w_bf16, parts=3):
    acc = None
    for p in _split_bf16(x, parts):
        t = _dot(p, w_bf16)
        acc = t if acc is None else acc + t
    return acc


def _dot_exact_lhs(w_bf16, x, parts=3):
    acc = None
    for p in _split_bf16(x, parts):
        t = _dot(w_bf16, p)
        acc = t if acc is None else acc + t
    return acc


def _layernorm(x, g, b):
    mu = jnp.mean(x, -1, keepdims=True)
    d = x - mu
    var = jnp.mean(d * d, -1, keepdims=True)
    return d * lax.rsqrt(var + LN_EPS) * g + b


def _cast_body(x_ref, o_ref):
    o_ref[...] = x_ref[...].astype(BF16)


def _cast_bf16(w, block_rows, col_blocks=2):
    w2 = w.reshape(-1, w.shape[-1])
    rows, cols = w2.shape
    spec = pl.BlockSpec((block_rows, cols // col_blocks), lambda i, j: (i, j))
    out = pl.pallas_call(
        _cast_body, grid=(rows // block_rows, col_blocks), in_specs=[spec], out_specs=spec,
        out_shape=jax.ShapeDtypeStruct((rows, cols), BF16), compiler_params=_cp("parallel", "parallel"),
        name="cast_bf16",
    )(w2)
    return out.reshape(w.shape)


def _pack_halves(x):
    n = x.shape[1] // 2
    xb = x.astype(BF16).astype(F32)
    lo = lax.shift_right_logical(pltpu.bitcast(xb[:, :n], jnp.uint32), jnp.uint32(16))
    hi = pltpu.bitcast(xb[:, n:], jnp.uint32) & jnp.uint32(0xFFFF0000)
    return lo | hi


def _unpack_halves(p):
    lo = pltpu.bitcast(lax.shift_left(p, jnp.uint32(16)), F32)
    hi = pltpu.bitcast(p & jnp.uint32(0xFFFF0000), F32)
    return lo, hi


def _gather_body(idx_ref, x_hbm, o_ref, sem, *, step):
    base = pl.program_id(0) * step

    def issue(r, carry):
        pltpu.make_async_copy(x_hbm.at[pl.ds(idx_ref[base + r], 1)], o_ref.at[pl.ds(r, 1)], sem.at[0]).start()
        return carry

    lax.fori_loop(0, step, issue, 0, unroll=8)
    pltpu.make_async_copy(x_hbm.at[pl.ds(0, step)], o_ref, sem.at[0]).wait()


def _gather_rows(x, idx):
    rows = idx.shape[0]
    d = x.shape[1]
    step = 2048 if rows % 2048 == 0 else MOE_TILE
    return pl.pallas_call(
        functools.partial(_gather_body, step=step),
        grid_spec=pltpu.PrefetchScalarGridSpec(
            num_scalar_prefetch=1,
            grid=(rows // step,),
            in_specs=[pl.BlockSpec(memory_space=pl.ANY)],
            out_specs=pl.BlockSpec((step, d), lambda i, idx: (i, 0)),
            scratch_shapes=[pltpu.SemaphoreType.DMA((1,))],
        ),
        out_shape=jax.ShapeDtypeStruct((rows, d), x.dtype),
        compiler_params=_cp("arbitrary"),
        name="gather_rows",
    )(idx, x)


def _ada_body(c_ref, w_ref, b_ref, o_ref):
    c = c_ref[...]
    cond = c * jax.nn.sigmoid(c)
    o_ref[0] = _dot_exact_rhs(cond, w_ref[0].astype(BF16), 2) + b_ref[0]


def _ada(c, ada_w, ada_b):
    depth, d, n = ada_w.shape
    b = c.shape[0]
    rows = -(-b // 8) * 8
    cp = jnp.pad(c, ((0, rows - b), (0, 0)))
    tn = 1024
    out = pl.pallas_call(
        _ada_body,
        grid=(depth, n // tn),
        in_specs=[
            pl.BlockSpec((rows, d), lambda l, j: (0, 0)),
            pl.BlockSpec((1, d, tn), lambda l, j: (l, 0, j)),
            pl.BlockSpec((1, 1, tn), lambda l, j: (l, 0, j)),
        ],
        out_specs=pl.BlockSpec((1, rows, tn), lambda l, j: (l, 0, j)),
        out_shape=jax.ShapeDtypeStruct((depth, rows, n), F32),
        compiler_params=_cp("parallel", "parallel"),
        name="ada",
    )(cp, ada_w, ada_b.reshape(depth, 1, n))
    return out[:, :b].reshape(depth, b, 6, d)


def _front0_body(x_ref, mod_ref, cs_ref, win_ref, qn_ref, kvn_ref, wq1_ref, wq2_ref, wkn_ref,
                 wv_ref, q_ref, k_ref, v_ref, u_ref):
    m = mod_ref[0]
    h = (x_ref[...] * (1.0 + m[1:2]) + m[0:1]).astype(BF16)
    z = _dot(h, win_ref[...])
    cs = cs_ref[...]
    c2 = cs[:, :LANES]
    s2 = cs[:, LANES:]
    q_c = z[:, :MLA_Q_RANK]
    kv_c = z[:, MLA_Q_RANK:MLA_Q_RANK + MLA_KV_RANK]
    o = MLA_Q_RANK + MLA_KV_RANK
    krope = (z[:, o:o + LANES] * c2 + z[:, o + LANES:o + 2 * LANES] * s2).astype(BF16)
    u_ref[...] = z[:, o + 2 * LANES:]
    hq = (q_c * lax.rsqrt(jnp.mean(q_c * q_c, -1, keepdims=True) + RMS_EPS) * qn_ref[...]).astype(BF16)
    hkv = (kv_c * lax.rsqrt(jnp.mean(kv_c * kv_c, -1, keepdims=True) + RMS_EPS) * kvn_ref[...]).astype(BF16)
    full = _dot(hq, wq1_ref[...])
    sw = _dot(hq, wq2_ref[...])
    kn = _dot(hkv, wkn_ref[...])
    v_ref[...] = _dot(hkv, wv_ref[...]).astype(BF16)
    for j in range(MLA_HEADS):
        a = j * MLA_HEAD_PAD
        q_ref[:, a:a + LANES] = full[:, a:a + LANES].astype(BF16)
        q_ref[:, a + LANES:a + 2 * LANES] = (
            full[:, a + LANES:a + 2 * LANES] * c2 + sw[:, j * LANES:(j + 1) * LANES] * s2).astype(BF16)
        k_ref[:, a:a + LANES] = kn[:, j * LANES:(j + 1) * LANES].astype(BF16)
        k_ref[:, a + LANES:a + 2 * LANES] = krope


def _rope_swapped(w):
    half = w.shape[-1] // 2
    return jnp.concatenate([-w[..., half:], w[..., :half]], -1)


def _front0(x2, mod, seq, ev_w_in, q_norm, w_uq, kv_norm, w_ukv):
    t, d = x2.shape
    tm = 256
    per_b = seq // tm
    zpad = jnp.zeros((d, LANES - MLA_ROPE), F32)
    o = MLA_Q_RANK + MLA_KV_RANK
    w_kr = ev_w_in[:, o:o + MLA_ROPE]
    win = jnp.concatenate([ev_w_in[:, :o], w_kr, zpad, _rope_swapped(w_kr), zpad,
                           ev_w_in[:, o + MLA_ROPE:]], 1).astype(BF16)
    scale = MLA_QK ** -0.5
    wq = w_uq.reshape(MLA_Q_RANK, MLA_HEADS, MLA_QK) * scale
    zq = jnp.zeros((MLA_Q_RANK, MLA_HEADS, MLA_HEAD_PAD - MLA_QK), F32)
    wq1 = jnp.concatenate([wq, zq], -1).reshape(MLA_Q_RANK, -1).astype(BF16)
    wq2 = jnp.concatenate([_rope_swapped(wq[..., MLA_NOPE:]), zq], -1).reshape(MLA_Q_RANK, -1).astype(BF16)
    wkv = w_ukv.reshape(MLA_KV_RANK, MLA_HEADS, MLA_NOPE + MLA_V)
    wkn = wkv[..., :MLA_NOPE].reshape(MLA_KV_RANK, -1).astype(BF16)
    wv = wkv[..., MLA_NOPE:].reshape(MLA_KV_RANK, -1).astype(BF16)
    inv = ROPE_THETA ** (-jnp.arange(0, MLA_ROPE, 2, dtype=F32) / MLA_ROPE)
    ang = jnp.arange(seq, dtype=F32)[:, None] * inv[None]
    zt = jnp.zeros((seq, LANES - MLA_ROPE), F32)
    cs = jnp.concatenate([jnp.cos(ang), jnp.cos(ang), zt, jnp.sin(ang), jnp.sin(ang), zt], 1)
    const = lambda i: (0, 0)
    hq_w = MLA_HEADS * MLA_HEAD_PAD
    hv_w = MLA_HEADS * MLA_V
    s5w = ev_w_in.shape[1] - o - MLA_ROPE
    return pl.pallas_call(
        _front0_body,
        grid=(t // tm,),
        in_specs=[
            pl.BlockSpec((tm, d), lambda i: (i, 0)),
            pl.BlockSpec((1, 6, d), lambda i: (i // per_b, 0, 0)),
            pl.BlockSpec((tm, 2 * LANES), lambda i: (i % per_b, 0)),
            pl.BlockSpec(win.shape, const),
            pl.BlockSpec((1, MLA_Q_RANK), const),
            pl.BlockSpec((1, MLA_KV_RANK), const),
            pl.BlockSpec(wq1.shape, const),
            pl.BlockSpec(wq2.shape, const),
            pl.BlockSpec(wkn.shape, const),
            pl.BlockSpec(wv.shape, const),
        ],
        out_specs=[
            pl.BlockSpec((tm, hq_w), lambda i: (i, 0)),
            pl.BlockSpec((tm, hq_w), lambda i: (i, 0)),
            pl.BlockSpec((tm, hv_w), lambda i: (i, 0)),
            pl.BlockSpec((tm, s5w), lambda i: (i, 0)),
        ],
        out_shape=[
            jax.ShapeDtypeStruct((t, hq_w), BF16),
            jax.ShapeDtypeStruct((t, hq_w), BF16),
            jax.ShapeDtypeStruct((t, hv_w), BF16),
            jax.ShapeDtypeStruct((t, s5w), F32),
        ],
        compiler_params=_cp("parallel"),
        name="front0",
    )(x2, mod, cs, win, q_norm.reshape(1, -1), kv_norm.reshape(1, -1), wq1, wq2, wkn, wv)


def _flash_body(q_ref, k_ref, v_ref, o_ref, *, tile, heads):
    i = pl.program_id(2)
    qs = [q_ref[:, h * MLA_HEAD_PAD:(h + 1) * MLA_HEAD_PAD] for h in range(heads)]

    def one_head(h, start, carry, diagonal):
        m, l, acc = carry
        k = k_ref[pl.ds(start, tile), h * MLA_HEAD_PAD:(h + 1) * MLA_HEAD_PAD]
        v = v_ref[pl.ds(start, tile), h * MLA_V:(h + 1) * MLA_V]
        s = _dot_nt(qs[h], k)
        if diagonal:
            row = lax.broadcasted_iota(jnp.int32, s.shape, 0)
            col = lax.broadcasted_iota(jnp.int32, s.shape, 1)
            s = jnp.where(col <= row, s, NEG_INF)
        m_new = jnp.maximum(m, jnp.max(s, -1, keepdims=True))
        alpha = jnp.exp(m - m_new)
        p = jnp.exp(s - m_new)
        l = l * alpha + jnp.sum(p, -1, keepdims=True)
        acc = acc * alpha + _dot(p.astype(BF16), v)
        return m_new, l, acc

    def step(j, carry, diagonal):
        start = pl.multiple_of(j * tile, tile)
        return tuple(one_head(h, start, carry[h], diagonal) for h in range(heads))

    init = tuple((jnp.full((tile, 1), NEG_INF, F32), jnp.zeros((tile, 1), F32),
                  jnp.zeros((tile, MLA_V), F32)) for _ in range(heads))
    carry = lax.fori_loop(0, i, lambda j, c: step(j, c, False), init)
    carry = step(i, carry, True)
    for h, (_, l, acc) in enumerate(carry):
        o_ref[:, h * MLA_V:(h + 1) * MLA_V] = (acc / l).astype(BF16)


def _flash(q, k, v, batch, seq):
    t = q.shape[0]
    tile = 512
    heads = 2
    nq = seq // tile
    return pl.pallas_call(
        functools.partial(_flash_body, tile=tile, heads=heads),
        grid=(batch, MLA_HEADS // heads, nq),
        in_specs=[
            pl.BlockSpec((tile, heads * MLA_HEAD_PAD), lambda b, h, i: (b * nq + i, h)),
            pl.BlockSpec((seq, heads * MLA_HEAD_PAD), lambda b, h, i: (b, h)),
            pl.BlockSpec((seq, heads * MLA_V), lambda b, h, i: (b, h)),
        ],
        out_specs=pl.BlockSpec((tile, heads * MLA_V), lambda b, h, i: (b * nq + i, h)),
        out_shape=jax.ShapeDtypeStruct((t, MLA_HEADS * MLA_V), BF16),
        compiler_params=_cp("parallel", "parallel", "arbitrary"),
        name="mla_attention",
    )(q, k, v)


S5_GROUPS_PER_BLOCK = LANES // S5_GROUP


def _s5_body(u_ref, w_ref, v_ref, bd_ref, ac_ref, bc_ref, y_ref, ucat, yx, *, n_chunks, n_steps):
    lc = S5_CHUNK
    for t in range(lc):
        ucat[:, t * LANES:(t + 1) * LANES] = u_ref[pl.ds(t, n_chunks, stride=lc), :].astype(BF16)
    z = _dot(ucat[...], w_ref[0])
    half = z.shape[1] // 2
    cidx = lax.broadcasted_iota(jnp.int32, z.shape, 0)
    for step in range(n_steps):
        d = 1 << step
        zs = jnp.where(cidx >= d, pltpu.roll(z, d, 0), 0.0)
        z = z + zs * ac_ref[0, step:step + 1, :] + pltpu.roll(zs, half, 1) * bc_ref[0, step:step + 1, :]
    xprev = jnp.where(cidx >= 1, pltpu.roll(z, 1, 0), 0.0)
    yx[...] = _dot(xprev.astype(BF16), v_ref[0])
    for t in range(lc):
        k = (t + 1) * LANES
        y = _dot(ucat[:, :k], bd_ref[0, (lc - 1 - t) * LANES:, :]) + yx[:, t * LANES:(t + 1) * LANES]
        y_ref[pl.ds(t, n_chunks, stride=lc), :] = y


def _s5_matrices(lam_re, lam_im, b_re, b_im, c_re, c_im, log_dt, n_steps):
    lc = S5_CHUNK
    gb = S5_GROUPS_PER_BLOCK
    lam = lax.complex(lam_re.astype(F32), lam_im.astype(F32))
    n_blk = lam.shape[0] // gb
    dt = jnp.exp(log_dt.astype(F32))[:, None]
    lam_bar = jnp.exp(lam * dt)
    b_bar = ((lam_bar - 1.0) / lam)[..., None] * lax.complex(b_re.astype(F32), b_im.astype(F32))
    c = lax.complex(c_re.astype(F32), c_im.astype(F32))
    j = jnp.arange(lc + 1, dtype=F32)[:, None, None]
    pows = jnp.exp(j * (lam * dt)[None])
    eye = jnp.eye(gb, dtype=F32)
    kern = jnp.einsum("gop,jgp,gpi->jgio", c, pows[:lc], b_bar).real
    kern = kern.reshape(lc, n_blk, gb, S5_GROUP, S5_GROUP).transpose(1, 0, 2, 3, 4)
    mask = eye.astype(BF16)
    bd = (kern.astype(BF16)[:, :, :, :, None, :] * mask[None, None, :, None, :, None]).reshape(
        n_blk, lc, LANES, LANES)
    bdrev = bd[:, ::-1].reshape(n_blk, lc * LANES, LANES)
    tt = jnp.arange(lc)
    wm = pows[lc - 1 - tt][..., None] * b_bar[None]
    wm = jnp.stack([wm.real, wm.imag], 2)
    wm = wm.reshape(lc, n_blk, gb, 2, S5_STATE, S5_GROUP).transpose(1, 0, 2, 5, 3, 4)
    wcat = (wm.astype(BF16)[:, :, :, :, :, None, :] * mask[None, None, :, None, None, :, None]).reshape(
        n_blk, lc * LANES, 2 * gb * S5_STATE)
    cv = c[None] * pows[1:lc + 1][:, :, None, :]
    cv = jnp.stack([cv.real, -cv.imag], 3)
    cv = cv.reshape(lc, n_blk, gb, S5_GROUP, 2, S5_STATE).transpose(1, 4, 2, 5, 0, 3)
    vcat = (cv.astype(BF16)[:, :, :, :, :, None, :] * mask[None, None, :, None, None, :, None]).reshape(
        n_blk, 2 * gb * S5_STATE, lc * LANES)
    stride = (lc * (2.0 ** jnp.arange(n_steps, dtype=F32)))[None, :, None]
    ap = jnp.exp(stride * (lam * dt)[:, None, :])

    def per_block(a):
        return a.reshape(n_blk, gb, n_steps, S5_STATE).transpose(0, 2, 1, 3).reshape(n_blk, n_steps, -1)

    ar, ai = per_block(ap.real), per_block(ap.imag)
    ac = jnp.concatenate([ar, ar], -1)
    bc = jnp.concatenate([-ai, ai], -1)
    return wcat, vcat, bdrev, ac, bc


def _s5_scan(u, batch, seq, mats):
    wcat, vcat, bdrev, ac, bc = mats
    n_blk = wcat.shape[0]
    lc = S5_CHUNK
    nc = seq // lc
    n_steps = ac.shape[1]
    wide = wcat.shape[2]
    per_blk = lambda i, b: (i, 0, 0)
    return pl.pallas_call(
        functools.partial(_s5_body, n_chunks=nc, n_steps=n_steps),
        grid=(n_blk, batch),
        in_specs=[
            pl.BlockSpec((seq, LANES), lambda i, b: (b, i)),
            pl.BlockSpec((1, lc * LANES, wide), per_blk),
            pl.BlockSpec((1, wide, lc * LANES), per_blk),
            pl.BlockSpec((1, lc * LANES, LANES), per_blk),
            pl.BlockSpec((1, n_steps, wide), per_blk),
            pl.BlockSpec((1, n_steps, wide), per_blk),
        ],
        out_specs=pl.BlockSpec((seq, LANES), lambda i, b: (b, i)),
        out_shape=jax.ShapeDtypeStruct(u.shape, F32),
        scratch_shapes=[pltpu.VMEM((nc, lc * LANES), BF16), pltpu.VMEM((nc, lc * LANES), F32)],
        compiler_params=_cp("parallel", "arbitrary"),
        name="s5_scan",
    )(u, wcat, vcat, bdrev, ac, bc)


def _s5_glu_body(y_ref, u_ref, d_ref, w_ref, o_ref):
    y = y_ref[...] + d_ref[...] * u_ref[...]
    y = 0.5 * y * (1.0 + jnp.tanh(math.sqrt(2.0 / math.pi) * (y + 0.044715 * (y * y * y))))
    gate = jax.nn.sigmoid(_dot(y.astype(BF16), w_ref[...]))
    o_ref[...] = (y * gate).astype(BF16)


def _s5_glu(y, u, d_skip, w_glu):
    t, n = y.shape
    tm = 512
    row = lambda i: (i, 0)
    const = lambda i: (0, 0)
    return pl.pallas_call(
        _s5_glu_body,
        grid=(t // tm,),
        in_specs=[pl.BlockSpec((tm, n), row), pl.BlockSpec((tm, n), row),
                  pl.BlockSpec((1, n), const), pl.BlockSpec((n, n), const)],
        out_specs=pl.BlockSpec((tm, n), row),
        out_shape=jax.ShapeDtypeStruct((t, n), BF16),
        compiler_params=_cp("parallel"),
        name="s5_glu",
    )(y, u, d_skip.reshape(1, n), w_glu.astype(BF16))


def _outproj_body(a1_ref, a2_ref, w1_ref, w2_ref, x_ref, mod_ref, g_ref, b_ref, rwh_ref, xo_ref, h_ref,
                  lg_ref):
    y = _dot(a1_ref[...], w1_ref[...]) + _dot(a2_ref[...], w2_ref[...])
    m = mod_ref[0]
    xn = _layernorm(DN_ALPHA * x_ref[...] + (1.0 + m[2:3]) * y, g_ref[...], b_ref[...])
    xo_ref[...] = xn
    h = xn * (1.0 + m[4:5]) + m[3:4]
    h_ref[...] = _pack_halves(h)
    h_hi, h_lo = _split_bf16(h, 2)
    ne = lg_ref.shape[1]
    both = _dot(h_hi, rwh_ref[...])
    lg_ref[...] = both[:, :ne] + (_dot(h_lo, rwh_ref[:, :ne]) + both[:, ne:])


def _outproj(a1, a2, w_out, x2, mod, ln_g, ln_b, router_w, seq):
    t, d = x2.shape
    k1 = a1.shape[1]
    k2 = a2.shape[1]
    tm = 256
    per_b = seq // tm
    w1 = w_out[:k1].astype(BF16)
    w2 = w_out[k1:].astype(BF16)
    rwh = jnp.concatenate(_split_bf16(router_w, 2), 1)
    ne = router_w.shape[1]
    row = lambda i: (i, 0)
    const = lambda i: (0, 0)
    return pl.pallas_call(
        _outproj_body,
        grid=(t // tm,),
        in_specs=[
            pl.BlockSpec((tm, k1), row), pl.BlockSpec((tm, k2), row),
            pl.BlockSpec((k1, d), const), pl.BlockSpec((k2, d), const),
            pl.BlockSpec((tm, d), row),
            pl.BlockSpec((1, 6, d), lambda i: (i // per_b, 0, 0)),
            pl.BlockSpec((1, d), const), pl.BlockSpec((1, d), const),
            pl.BlockSpec((d, 2 * ne), const),
        ],
        out_specs=[pl.BlockSpec((tm, d), row), pl.BlockSpec((tm, d // 2), row), pl.BlockSpec((tm, ne), row)],
        out_shape=[jax.ShapeDtypeStruct((t, d), F32), jax.ShapeDtypeStruct((t, d // 2), jnp.uint32),
                   jax.ShapeDtypeStruct((t, ne), F32)],
        compiler_params=_cp("parallel"),
        name="outproj_ln",
    )(a1, a2, w1, w2, x2, mod, ln_g.reshape(1, d), ln_b.reshape(1, d), rwh)


def _route_body(lg_ref, rb_ref, cls_ref, glo_ref, ghi_ref):
    s = jax.nn.sigmoid(lg_ref[...])
    sb = s + rb_ref[...]
    rows = [sb[e:e + 1, :] for e in range(N_EXPERTS)]
    raw = [s[e:e + 1, :] for e in range(N_EXPERTS)]
    gscore = []
    for g in range(N_EXPERT_GROUPS):
        v = rows[g * EXPERTS_PER_GROUP:(g + 1) * EXPERTS_PER_GROUP]
        best = None
        for a in range(EXPERTS_PER_GROUP):
            for b in range(a + 1, EXPERTS_PER_GROUP):
                pair = v[a] + v[b]
                best = pair if best is None else jnp.maximum(best, pair)
        gscore.append(best)
    bg = jnp.zeros_like(gscore[0], dtype=jnp.int32)
    bs = gscore[0]
    for g in range(1, N_EXPERT_GROUPS):
        upd = gscore[g] > bs
        bg = jnp.where(upd, g, bg)
        bs = jnp.where(upd, gscore[g], bs)

    def pick(vals, k):
        out = vals[k]
        for g in range(1, N_EXPERT_GROUPS):
            out = jnp.where(bg == g, vals[g * EXPERTS_PER_GROUP + k], out)
        return out

    v = [pick(rows, k) for k in range(EXPERTS_PER_GROUP)]
    r = [pick(raw, k) for k in range(EXPERTS_PER_GROUP)]
    i1 = jnp.zeros_like(bg)
    b1 = v[0]
    for k in range(1, EXPERTS_PER_GROUP):
        upd = v[k] > b1
        i1 = jnp.where(upd, k, i1)
        b1 = jnp.where(upd, v[k], b1)
    i2 = jnp.zeros_like(bg)
    b2 = jnp.full_like(b1, -jnp.inf)
    for k in range(EXPERTS_PER_GROUP):
        upd = (i1 != k) & (v[k] > b2)
        i2 = jnp.where(upd, k, i2)
        b2 = jnp.where(upd, v[k], b2)
    s1 = r[0]
    s2 = r[0]
    for k in range(1, EXPERTS_PER_GROUP):
        s1 = jnp.where(i1 == k, r[k], s1)
        s2 = jnp.where(i2 == k, r[k], s2)
    den = s1 + s2
    g1 = s1 / den
    g2 = s2 / den
    lo = jnp.minimum(i1, i2)
    hi = jnp.maximum(i1, i2)
    base = jnp.where(lo == 0, 0, jnp.where(lo == 1, 3, 5))
    cls_ref[...] = bg * N_PAIRS + base + hi - lo - 1
    first_lo = i1 < i2
    glo_ref[...] = jnp.where(first_lo, g1, g2)
    ghi_ref[...] = jnp.where(first_lo, g2, g1)


def _route(logits, router_b):
    t, ne = logits.shape
    tt = min(2048, t)
    col = lambda i: (0, i)
    cls, glo, ghi = pl.pallas_call(
        _route_body,
        grid=(t // tt,),
        in_specs=[pl.BlockSpec((ne, tt), col), pl.BlockSpec((ne, 1), lambda i: (0, 0))],
        out_specs=[pl.BlockSpec((1, tt), col)] * 3,
        out_shape=[jax.ShapeDtypeStruct((1, t), jnp.int32), jax.ShapeDtypeStruct((1, t), F32),
                   jax.ShapeDtypeStruct((1, t), F32)],
        compiler_params=_cp("parallel"),
        name="moe_route",
    )(logits.T, router_b.reshape(ne, 1).astype(F32))
    return cls[0], glo[0], ghi[0]


def _moe_body(e1_ref, e2_ref, valid_ref, h_ref, g_ref, wg1_ref, wg2_ref, wd1_ref, wd2_ref, o_ref):
    i = pl.program_id(0)

    @pl.when(valid_ref[i] == 1)
    def _():
        h_lo, h_hi = (v.astype(BF16) for v in _unpack_halves(h_ref[...]))
        half = h_lo.shape[1]
        g = g_ref[...]

        def expert(wg_ref, wd_ref):
            gu = _dot(h_lo, wg_ref[0, 0, :half, :]) + _dot(h_hi, wg_ref[0, 0, half:, :])
            gt = gu[:, :D_EXPERT]
            act = (gt * jax.nn.sigmoid(gt) * gu[:, D_EXPERT:]).astype(BF16)
            return _dot(act, wd_ref[0, 0])

        y = g[:, 0:1] * expert(wg1_ref, wd1_ref) + g[:, 1:2] * expert(wg2_ref, wd2_ref)
        o_ref[...] = _pack_halves(y)

    @pl.when(valid_ref[i] == 0)
    def _():
        o_ref[...] = jnp.zeros_like(o_ref)


_PAIR_LO = (0, 0, 0, 1, 1, 2)
_PAIR_HI = (1, 2, 3, 2, 3, 3)


def _moe(h, cls, glo, ghi, wgu, wdn, layer):
    t = h.shape[0]
    d = 2 * h.shape[1]
    tm = MOE_TILE
    n_tiles = t // tm + N_CLASSES
    rows = n_tiles * tm
    onehot = (cls[:, None] == jnp.arange(N_CLASSES)[None, :]).astype(jnp.int32)
    csum = jnp.cumsum(onehot, 0)
    rank = jnp.take_along_axis(csum, cls[:, None], 1)[:, 0] - 1
    counts = csum[-1]
    padded = -(-counts // tm) * tm
    ends = jnp.cumsum(padded)
    dest = (ends - padded)[cls] + rank
    src = (jnp.arange(rows, dtype=jnp.int32) % t).at[dest].set(jnp.arange(t, dtype=jnp.int32))
    tile_start = jnp.arange(n_tiles, dtype=jnp.int32) * tm
    valid = (tile_start < ends[-1]).astype(jnp.int32)
    last_cls = jnp.max(jnp.where(counts > 0, jnp.arange(N_CLASSES), 0))
    tile_cls = jnp.minimum(jnp.searchsorted(ends, tile_start, side="right"), last_cls).astype(jnp.int32)
    grp = tile_cls // N_PAIRS
    pair = tile_cls % N_PAIRS
    e1 = (grp * EXPERTS_PER_GROUP + jnp.asarray(_PAIR_LO, jnp.int32)[pair]).astype(jnp.int32)
    e2 = (grp * EXPERTS_PER_GROUP + jnp.asarray(_PAIR_HI, jnp.int32)[pair]).astype(jnp.int32)
    hs = _gather_rows(h, src)
    gs = jnp.stack([glo, ghi], 1).at[src].get(mode="promise_in_bounds")
    ys = pl.pallas_call(
        _moe_body,
        grid_spec=pltpu.PrefetchScalarGridSpec(
            num_scalar_prefetch=3,
            grid=(n_tiles,),
            in_specs=[
                pl.BlockSpec((tm, d // 2), lambda i, e1, e2, va: (i, 0)),
                pl.BlockSpec((tm, 2), lambda i, e1, e2, va: (i, 0)),
                pl.BlockSpec((1, 1, d, 2 * D_EXPERT), lambda i, e1, e2, va: (layer, e1[i], 0, 0)),
                pl.BlockSpec((1, 1, d, 2 * D_EXPERT), lambda i, e1, e2, va: (layer, e2[i], 0, 0)),
                pl.BlockSpec((1, 1, D_EXPERT, d), lambda i, e1, e2, va: (layer, e1[i], 0, 0)),
                pl.BlockSpec((1, 1, D_EXPERT, d), lambda i, e1, e2, va: (layer, e2[i], 0, 0)),
            ],
            out_specs=pl.BlockSpec((tm, d // 2), lambda i, e1, e2, va: (i, 0)),
        ),
        out_shape=jax.ShapeDtypeStruct((rows, d // 2), jnp.uint32),
        compiler_params=_cp("arbitrary"),
        name="moe_experts",
    )(e1, e2, valid, hs, gs, wgu, wgu, wdn, wdn)
    return _gather_rows(ys, dest.astype(jnp.int32))


def _resid_ln_body(x_ref, y_ref, mod_ref, g_ref, b_ref, o_ref):
    m = mod_ref[0]
    y = jnp.concatenate(_unpack_halves(y_ref[...]), 1)
    o_ref[...] = _layernorm(DN_ALPHA * x_ref[...] + (1.0 + m[5:6]) * y, g_ref[...], b_ref[...])


def _resid_ln(x2, y, mod, ln_g, ln_b, seq):
    t, d = x2.shape
    tm = 512
    per_b = seq // tm
    row = lambda i: (i, 0)
    const = lambda i: (0, 0)
    return pl.pallas_call(
        _resid_ln_body,
        grid=(t // tm,),
        in_specs=[pl.BlockSpec((tm, d), row), pl.BlockSpec((tm, d // 2), row),
                  pl.BlockSpec((1, 6, d), lambda i: (i // per_b, 0, 0)),
                  pl.BlockSpec((1, d), const), pl.BlockSpec((1, d), const)],
        out_specs=pl.BlockSpec((tm, d), row),
        out_shape=jax.ShapeDtypeStruct((t, d), F32),
        compiler_params=_cp("parallel"),
        name="resid_ln",
    )(x2, y, mod, ln_g.reshape(1, d), ln_b.reshape(1, d))


def _front1_body(x_ref, mod_ref, w_ref, zc0_ref, zc1_ref, zc2_ref, zr_ref, h_scr, res_scr, *, dils, tm):
    j = pl.program_id(1)
    n_att = 3 * len(dils)
    half = w_ref.shape[1] // 2

    @pl.when(j == 0)
    def _():
        m = mod_ref[0]
        h_scr[...] = (x_ref[...] * (1.0 + m[1:2]) + m[0:1]).astype(BF16)

    def column_half(c):
        return _dot(h_scr[...], w_ref[:, c * half:(c + 1) * half])

    @pl.when(j >= n_att)
    def _():
        for c in range(2):
            zr_ref[:, c * half:(c + 1) * half] = column_half(c).astype(BF16)

    for g, (o_ref, dil) in enumerate(zip((zc0_ref, zc1_ref, zc2_ref), dils)):
        @pl.when(j // 3 == g)
        def _(o_ref=o_ref, dil=dil):
            if dil == 1:
                for c in range(2):
                    o_ref[0, 0, :, c * half:(c + 1) * half] = column_half(c).astype(BF16)
                return
            per_half = half // LANES
            for c in range(2):
                res = column_half(c)
                for s in range(per_half):
                    res_scr[c * per_half + s] = res[:, s * LANES:(s + 1) * LANES]
            rows = tm // dil
            for r in range(dil):
                for s in range(2 * per_half):
                    o_ref[0, r, :, s * LANES:(s + 1) * LANES] = (
                        res_scr[s, pl.ds(r, rows, stride=dil), :].astype(BF16))


def _front1(x2, mod, od_w_in, batch, seq):
    t, d = x2.shape
    tm = min(1024, seq)
    tn = DIL_WIDTH
    per_b = seq // tm
    n_in = od_w_in.shape[1]
    n = -(-n_in // tn) * tn
    w = jnp.pad(od_w_in, ((0, 0), (0, n - n_in))).astype(BF16)
    dils = tuple(dil for _, dil in DIL_PATTERNS)
    n_att = 3 * len(dils)
    n_r = n // tn - n_att

    def att_spec(g, dil):
        return pl.BlockSpec((1, dil, tm // dil, tn),
                            lambda i, j: (i // per_b, 0, i % per_b, jnp.clip(j - 3 * g, 0, 2)))

    return pl.pallas_call(
        functools.partial(_front1_body, dils=dils, tm=tm),
        grid=(t // tm, n // tn),
        in_specs=[pl.BlockSpec((tm, d), lambda i, j: (i, 0)),
                  pl.BlockSpec((1, 6, d), lambda i, j: (i // per_b, 0, 0)),
                  pl.BlockSpec((d, tn), lambda i, j: (0, j))],
        out_specs=[att_spec(g, dil) for g, dil in enumerate(dils)]
        + [pl.BlockSpec((tm, tn), lambda i, j: (i, jnp.clip(j - n_att, 0, n_r - 1)))],
        out_shape=[jax.ShapeDtypeStruct((batch, dil, seq // dil, 3 * tn), BF16) for dil in dils]
        + [jax.ShapeDtypeStruct((t, n_r * tn), BF16)],
        scratch_shapes=[pltpu.VMEM((tm, d), BF16), pltpu.VMEM((tn // LANES, tm, LANES), F32)],
        compiler_params=_cp("parallel", "arbitrary"),
        name="front1",
    )(x2, mod, w)


def _dil_body(q_ref, kp_ref, kc_ref, vp_ref, vc_ref, bias_ref, o_ref, lse_ref):
    n = pl.program_id(2)
    span = DIL_SPAN
    q = q_ref[0, 0]
    k = jnp.concatenate([kp_ref[0, 0], kc_ref[0, 0]], 0)
    v = jnp.concatenate([vp_ref[0, 0], vc_ref[0, 0]], 0)
    qi = lax.broadcasted_iota(jnp.int32, (span, 2 * span), 0)
    ku = lax.broadcasted_iota(jnp.int32, (span, 2 * span), 1)
    valid = (ku >= qi) & (ku <= qi + span) & ((n >= 1) | (ku >= span))
    lane = lax.broadcasted_iota(jnp.int32, (span, LANES), 1)
    first = lane < DIL_HEAD_DIM
    lse_all = jnp.zeros((span, LANES), F32)
    scale = DIL_HEAD_DIM ** -0.5
    for hp in range(DIL_HEADS // 2):
        sl = slice(hp * LANES, (hp + 1) * LANES)
        q2, k2, v2 = q[:, sl], k[:, sl], v[:, sl]
        outs = []
        for hh in range(2):
            head = 2 * hp + hh
            qm = jnp.where(first if hh == 0 else jnp.logical_not(first), q2, jnp.zeros_like(q2))
            s = _dot_nt(qm, k2) * scale + bias_ref[head]
            s = jnp.where(valid, s, NEG_INF)
            m = jnp.max(s, -1, keepdims=True)
            p = jnp.exp(s - m)
            l = jnp.sum(p, -1, keepdims=True)
            outs.append(_dot(p.astype(BF16), v2) / l)
            lse_all = jnp.where(lane == head, m + jnp.log(l), lse_all)
        o_ref[0, 0, :, sl] = jnp.where(first, outs[0], outs[1]).astype(BF16)
    lse_ref[0, 0] = lse_all


def _t5_bucket(dist):
    exact = T5_BUCKETS // 2
    logd = jnp.log(jnp.maximum(dist, 1).astype(F32) / exact) / math.log(T5_MAX_DIST / exact)
    large = jnp.minimum(exact + (logd * (T5_BUCKETS - exact)).astype(jnp.int32), T5_BUCKETS - 1)
    return jnp.where(dist < exact, dist, large)


def _dilated_group(zc, rel_bias, gi, dil, batch, seq):
    span = DIL_SPAN
    l_stream = seq // dil
    nb = l_stream // span
    qi = jnp.arange(span)[:, None]
    ku = jnp.arange(2 * span)[None, :]
    delta = jnp.clip(span + qi - ku, 0, span) * dil
    onehot = (_t5_bucket(delta)[None] == jnp.arange(T5_BUCKETS)[:, None, None]).astype(F32)
    table = rel_bias.astype(F32)[:, gi * DIL_HEADS:(gi + 1) * DIL_HEADS]
    bias = jnp.einsum("kh,kqu->hqu", table, onehot, precision=lax.Precision.HIGHEST)

    def cur(j):
        return lambda b, r, n: (b, r, n, j)

    def prev(j):
        return lambda b, r, n: (b, r, jnp.maximum(n - 1, 0), j)

    blk = (1, 1, span, DIL_WIDTH)
    return pl.pallas_call(
        _dil_body,
        grid=(batch, dil, nb),
        in_specs=[pl.BlockSpec(blk, cur(0)), pl.BlockSpec(blk, prev(1)), pl.BlockSpec(blk, cur(1)),
                  pl.BlockSpec(blk, prev(2)), pl.BlockSpec(blk, cur(2)),
                  pl.BlockSpec((DIL_HEADS, span, 2 * span), lambda b, r, n: (0, 0, 0))],
        out_specs=[pl.BlockSpec(blk, cur(0)), pl.BlockSpec((1, 1, span, LANES), cur(0))],
        out_shape=[jax.ShapeDtypeStruct((batch, dil, l_stream, DIL_WIDTH), BF16),
                   jax.ShapeDtypeStruct((batch, dil, l_stream, LANES), F32)],
        compiler_params=_cp("parallel", "parallel", "arbitrary"),
        name=f"dilated_attention_{dil}",
    )(zc, zc, zc, zc, zc, bias)


def _dil_merge_body(o1_ref, o2_ref, o3_ref, l1_ref, l2_ref, l3_ref, e_ref, out_ref, o_scr, l1_scr, l2_scr,
                    l3_scr, *, dils, tm):
    def unstream(ref, scr, dil):
        rows = tm // dil
        n_slab = scr.shape[0]
        for r in range(dil):
            x = ref[0, r].astype(F32)
            for c in range(n_slab):
                scr[c, pl.ds(r, rows, stride=dil), :] = x[:, c * LANES:(c + 1) * LANES]
        return jnp.concatenate([scr[c] for c in range(n_slab)], 1)

    lses = [unstream(ref, scr, dil) for ref, scr, dil in
            zip((l1_ref, l2_ref, l3_ref), (l1_scr, l2_scr, l3_scr), dils)]
    top = jnp.maximum(jnp.maximum(lses[0], lses[1]), lses[2])
    num = None
    den = None
    for o_ref, lse, dil in zip((o1_ref, o2_ref, o3_ref), lses, dils):
        w = _dot_exact_rhs(jnp.exp(lse - top), e_ref[...], 2)
        o = unstream(o_ref, o_scr, dil)
        num = w * o if num is None else num + w * o
        den = w if den is None else den + w
    out_ref[...] = (num / den).astype(BF16)


def _dil_merge(outs, lses, batch, seq):
    tm = 512
    per_b = seq // tm
    dils = tuple(dil for _, dil in DIL_PATTERNS)
    head_of_lane = jnp.arange(DIL_WIDTH) // DIL_HEAD_DIM
    expand = (jnp.arange(LANES)[:, None] == head_of_lane[None, :]).astype(BF16)

    def spec(dil, width):
        return pl.BlockSpec((1, dil, tm // dil, width), lambda i: (i // per_b, 0, i % per_b, 0))

    return pl.pallas_call(
        functools.partial(_dil_merge_body, dils=dils, tm=tm),
        grid=(batch * per_b,),
        in_specs=[spec(dil, DIL_WIDTH) for dil in dils] + [spec(dil, LANES) for dil in dils]
        + [pl.BlockSpec((LANES, DIL_WIDTH), lambda i: (0, 0))],
        out_specs=pl.BlockSpec((tm, DIL_WIDTH), lambda i: (i, 0)),
        out_shape=jax.ShapeDtypeStruct((batch * seq, DIL_WIDTH), BF16),
        scratch_shapes=[pltpu.VMEM((DIL_WIDTH // LANES, tm, LANES), F32)] + [pltpu.VMEM((1, tm, LANES), F32)] * 3,
        compiler_params=_cp("parallel"),
        name="dilated_merge",
    )(*outs, *lses, expand)


def _rwkv_body(zr_ref, zk_ref, zv_ref, zm_ref, mur_ref, muk_ref, muv_ref, mum_ref, w0_ref, w2_ref,
               a0_ref, a2_ref, g2_ref, kk_ref, ka_ref, rk_ref, lng_ref, lnb_ref, o_ref,
               st_ref, pr_ref, pk_ref, pv_ref, pm_ref, xr_s, xk_s, xv_s, lw_s, as_s, g_s, cum_s):
    lc = RW_CHUNK
    n_slabs = st_ref.shape[0]

    @pl.when(pl.program_id(1) == 0)
    def _():
        st_ref[...] = jnp.zeros_like(st_ref)
        pr_ref[...] = jnp.zeros_like(pr_ref)
        pk_ref[...] = jnp.zeros_like(pk_ref)
        pv_ref[...] = jnp.zeros_like(pv_ref)
        pm_ref[...] = jnp.zeros_like(pm_ref)

    def shifted(z_ref, prev_ref, mu_ref):
        z = z_ref[...].astype(F32)
        first_row = lax.broadcasted_iota(jnp.int32, z.shape, 0) == 0
        prev = jnp.where(first_row, prev_ref[...], pltpu.roll(z, 1, 0))
        prev_ref[...] = z[lc - 1:lc, :]
        return z + (prev - z) * mu_ref[...]

    xr_s[...] = shifted(zr_ref, pr_ref, mur_ref)
    xk_s[...] = shifted(zk_ref, pk_ref, muk_ref)
    xv_s[...] = shifted(zv_ref, pv_ref, muv_ref)
    xm = shifted(zm_ref, pm_ref, mum_ref)
    lora_in = xm[:, :LANES]
    wl = w0_ref[...] + _dot(jnp.tanh(lora_in).astype(BF16), w2_ref[...])
    w_log = -(jnp.maximum(-wl, 0.0) + jnp.log(1.0 + jnp.exp(-jnp.abs(wl)))) - 0.5
    lw = -jnp.exp(w_log)
    lw_s[...] = lw
    as_s[...] = jax.nn.sigmoid(a0_ref[...] + _dot(lora_in.astype(BF16), a2_ref[...]))
    g_s[...] = _dot(jax.nn.sigmoid(xm[:, LANES:]).astype(BF16), g2_ref[...])
    ti = lax.broadcasted_iota(jnp.int32, (lc, lc), 0)
    si = lax.broadcasted_iota(jnp.int32, (lc, lc), 1)
    cum_s[...] = _dot_exact_lhs((si <= ti).astype(BF16), lw, 3)

    row = lax.broadcasted_iota(jnp.int32, (2 * lc, 2 * lc), 0)
    colm = lax.broadcasted_iota(jnp.int32, (2 * lc, 2 * lc), 1)
    same_head = (row >= lc) == (colm >= lc)
    t_in = row & (lc - 1)
    s_in = colm & (lc - 1)
    strict = same_head & (s_in < t_in)
    incl = same_head & (s_in <= t_in)
    eye = row == colm
    ones_bd = same_head.astype(BF16)
    lane = lax.broadcasted_iota(jnp.int32, (lc, LANES), 1)
    first = lane < RW_HEAD

    def seg(x, parts=2):
        return _dot_exact_rhs(x, ones_bd, parts)

    def stack_heads(x):
        return jnp.concatenate([jnp.where(first, x, 0.0), jnp.where(first, 0.0, x)], 0)

    slabs = range(n_slabs)
    sls = [slice(hp * LANES, (hp + 1) * LANES) for hp in slabs]
    kk_raw = [xk_s[:, sl] * kk_ref[:, sl] for sl in sls]
    kk_ss = [seg(kk * kk) for kk in kk_raw]
    lhs, rhs, bk_t, v_bs, k2s, pl_cols = [], [], [], [], [], []
    for hp, sl in enumerate(sls):
        a_sig = as_s[:, sl]
        cum = cum_s[:, sl]
        kk = kk_raw[hp] / jnp.maximum(jnp.sqrt(kk_ss[hp]), 1e-12)
        k2 = xk_s[:, sl] * (1.0 + (a_sig - 1.0) * ka_ref[:, sl])
        b_p = kk * a_sig
        cl = cum[lc - 1:lc, :]
        e_neg = jnp.exp(-cum)
        e_end = jnp.exp(cl - cum)
        at = -kk * jnp.exp(cum - lw_s[:, sl])
        rt = xr_s[:, sl] * jnp.exp(cum)
        lhs.append(jnp.concatenate([stack_heads(at), stack_heads(rt)], 0).astype(BF16))
        bt = (b_p * e_neg).astype(BF16)
        kt = (k2 * e_neg).astype(BF16)
        rhs.append(jnp.concatenate([bt, bt, kt, kt], 0))
        bk_t.append(jnp.concatenate([b_p * e_end, k2 * e_end], 0).T.astype(BF16))
        v_bs.append(xv_s[:, sl].astype(BF16))
        k2s.append(k2)
        pl_cols.append(jnp.sum(jnp.where(eye, jnp.broadcast_to(jnp.exp(cl), (2 * lc, LANES)), 0.0), 1,
                               keepdims=True))
    qms = [_dot_nt(lhs[hp], rhs[hp]) for hp in slabs]
    states = [st_ref[hp] for hp in slabs]
    arss = [_dot(lhs[hp], states[hp].astype(BF16)) for hp in slabs]
    a_abs = [jnp.where(strict, qm[:2 * lc, :2 * lc], 0.0).astype(BF16) for qm in qms]
    xs = [jnp.where(eye, 1.0, 0.0) + a.astype(F32) for a in a_abs]
    ps = [_dot(a, a).astype(BF16) for a in a_abs]
    n_double = int(math.log2(lc)) - 1
    for it in range(n_double - 1):
        res = [_dot(ps[hp], jnp.concatenate([xs[hp].astype(BF16), ps[hp]], 1)) for hp in slabs]
        xs = [xs[hp] + res[hp][:, :2 * lc] for hp in slabs]
        ps = [res[hp][:, 2 * lc:].astype(BF16) for hp in slabs]
    xs = [xs[hp] + _dot(ps[hp], xs[hp].astype(BF16)) for hp in slabs]
    vss = [jnp.concatenate([v_b, v_b], 0) for v_b in v_bs]
    gmats = [_dot(jnp.where(strict, qms[hp][:2 * lc, 2 * lc:], 0.0).astype(BF16), vss[hp]) + arss[hp][:2 * lc]
             for hp in slabs]
    uss = [_dot(xs[hp].astype(BF16), gmats[hp].astype(BF16)) for hp in slabs]
    incl2 = jnp.concatenate([incl, incl], 1)
    yss = [_dot(jnp.where(incl2, qms[hp][2 * lc:, :], 0.0).astype(BF16),
                jnp.concatenate([uss[hp].astype(BF16), vss[hp]], 0)) + arss[hp][2 * lc:] for hp in slabs]
    us_p = [jnp.where(first, us[:lc], us[lc:]) for us in uss]
    ys_p = [jnp.where(first, ys[:lc], ys[lc:]) for ys in yss]
    upds = [_dot(bk_t[hp], jnp.concatenate([us_p[hp].astype(BF16), v_bs[hp]], 0)) for hp in slabs]
    for hp in slabs:
        st_ref[hp] = pl_cols[hp] * states[hp] + jnp.where(same_head, upds[hp], 0.0)
    means = [seg(y) * (1.0 / RW_HEAD) for y in ys_p]
    dys = [ys_p[hp] - means[hp] for hp in slabs]
    variances = [seg(dy * dy, 1) * (1.0 / RW_HEAD) for dy in dys]
    bonus = [seg(xr_s[:, sl] * k2s[hp] * rk_ref[:, sl], 1) for hp, sl in enumerate(sls)]
    for hp, sl in enumerate(sls):
        yn = dys[hp] * lax.rsqrt(variances[hp] + RW_GN_EPS) * lng_ref[:, sl] + lnb_ref[:, sl]
        o_ref[:, sl] = ((yn + bonus[hp] * xv_s[:, sl]) * g_s[:, sl]).astype(BF16)


def _rwkv(z, batch, seq, mu, w0, w2, a0, a2, g2, k_k, k_a, r_k, lnx_g, lnx_b):
    t = z.shape[0]
    lc = RW_CHUNK
    nc = seq // lc
    width = w0.shape[0]
    n_slabs = width // LANES
    col0 = 0
    misc_blk = 3 * width // RW_MISC
    n_lora = RW_LORA_W + RW_LORA_A + RW_LORA_G
    mu_m = jnp.pad(mu[3 * width:], (0, RW_MISC - n_lora)).reshape(1, RW_MISC)
    w2p = jnp.pad(w2, ((0, LANES - RW_LORA_W), (0, 0))).astype(BF16)
    a2p = jnp.pad(a2, ((RW_LORA_W, LANES - RW_LORA_W - RW_LORA_A), (0, 0))).astype(BF16)
    g2p = jnp.pad(g2, ((0, RW_MISC - LANES - RW_LORA_G), (0, 0))).astype(BF16)
    vec = lambda a: a.reshape(1, width).astype(F32)
    const = lambda b, c: (0, 0)
    wide = lambda: pl.BlockSpec((1, width), const)
    return pl.pallas_call(
        _rwkv_body,
        grid=(batch, nc),
        in_specs=[
            pl.BlockSpec((lc, width), lambda b, c: (b * nc + c, col0)),
            pl.BlockSpec((lc, width), lambda b, c: (b * nc + c, col0 + 1)),
            pl.BlockSpec((lc, width), lambda b, c: (b * nc + c, col0 + 2)),
            pl.BlockSpec((lc, RW_MISC), lambda b, c: (b * nc + c, misc_blk)),
            wide(), wide(), wide(), pl.BlockSpec((1, RW_MISC), const),
            wide(), pl.BlockSpec((LANES, width), const),
            wide(), pl.BlockSpec((LANES, width), const),
            pl.BlockSpec((RW_MISC - LANES, width), const),
            wide(), wide(), wide(), wide(), wide(),
        ],
        out_specs=pl.BlockSpec((lc, width), lambda b, c: (b * nc + c, 0)),
        out_shape=jax.ShapeDtypeStruct((t, width), BF16),
        scratch_shapes=[pltpu.VMEM((n_slabs, LANES, LANES), F32),
                        pltpu.VMEM((1, width), F32), pltpu.VMEM((1, width), F32),
                        pltpu.VMEM((1, width), F32), pltpu.VMEM((1, RW_MISC), F32)]
        + [pltpu.VMEM((lc, width), F32)] * 7,
        compiler_params=_cp("parallel", "arbitrary"),
        name="rwkv7",
    )(z, z, z, z, vec(mu[:width]), vec(mu[width:2 * width]), vec(mu[2 * width:3 * width]), mu_m,
      vec(w0), w2p, vec(a0), a2p, g2p, vec(k_k), vec(k_a), vec(r_k), vec(lnx_g), vec(lnx_b))


def kernel(x, c, ada_w, ada_b, ln_mix_g, ln_mix_b, ln_ffn_g, ln_ffn_b, router_w, router_b, moe_w_gate_up, moe_w_down, rel_bias, ev_w_in, mla_q_norm, mla_w_uq, mla_kv_norm, mla_w_ukv, s5_lambda_re, s5_lambda_im, s5_b_re, s5_b_im, s5_c_re, s5_c_im, s5_d, s5_log_dt, s5_w_glu, ev_w_out, od_w_in, rw_mu, rw_w0, rw_w2, rw_a0, rw_a2, rw_g2, rw_k_k, rw_k_a, rw_r_k, rw_lnx_g, rw_lnx_b, od_w_out):
    batch, seq, d = x.shape
    assert seq % 2048 == 0 and d % LANES == 0
    x2 = x.reshape(batch * seq, d)
    mods = _ada(c, ada_w, ada_b)
    wgu_all = _cast_bf16(moe_w_gate_up, 2 * moe_w_gate_up.shape[-2])
    wdn_all = _cast_bf16(moe_w_down, 4 * moe_w_down.shape[-2])
    for layer in range(DEPTH):
        mod = mods[layer]
        if layer % 2 == 0:
            e = layer // 2
            q, k, v, u = _front0(x2, mod, seq, ev_w_in[e], mla_q_norm[e], mla_w_uq[e], mla_kv_norm[e],
                                 mla_w_ukv[e])
            att = _flash(q, k, v, batch, seq)
            n_steps = max(1, math.ceil(math.log2(seq // S5_CHUNK)))
            mats = _s5_matrices(s5_lambda_re[e], s5_lambda_im[e], s5_b_re[e], s5_b_im[e], s5_c_re[e],
                                s5_c_im[e], s5_log_dt[e], n_steps)
            ssm = _s5_glu(_s5_scan(u, batch, seq, mats), u, s5_d[e], s5_w_glu[e])
            x2, h, logits = _outproj(att, ssm, ev_w_out[e], x2, mod, ln_mix_g[layer], ln_mix_b[layer],
                                     router_w, seq)
        else:
            o = layer // 2
            *zcs, z = _front1(x2, mod, od_w_in[o], batch, seq)
            outs, lses = [], []
            for gi, (win, dil) in enumerate(DIL_PATTERNS):
                assert win // dil == DIL_SPAN
                og, lg = _dilated_group(zcs[gi], rel_bias, gi, dil, batch, seq)
                outs.append(og)
                lses.append(lg)
            att = _dil_merge(outs, lses, batch, seq)
            tm_out = _rwkv(z, batch, seq, rw_mu[o], rw_w0[o], rw_w2[o], rw_a0[o], rw_a2[o], rw_g2[o],
                           rw_k_k[o], rw_k_a[o], rw_r_k[o], rw_lnx_g[o], rw_lnx_b[o])
            x2, h, logits = _outproj(att, tm_out, od_w_out[o], x2, mod, ln_mix_g[layer], ln_mix_b[layer],
                                     router_w, seq)
        cls, glo, ghi = _route(logits, router_b)
        y = _moe(h, cls, glo, ghi, wgu_all, wdn_all, layer)
        x2 = _resid_ln(x2, y, mod, ln_ffn_g[layer], ln_ffn_b[layer], seq)
    return x2.reshape(batch, seq, d)
```

```python
import functools
import math

import jax
import jax.numpy as jnp
from jax import lax
from jax.experimental import pallas as pl
from jax.experimental.pallas import tpu as pltpu

F32 = jnp.float32
BF16 = jnp.bfloat16

DEPTH = 2
DN_ALPHA = (2.0 * DEPTH) ** 0.25
LN_EPS = 1e-5
RMS_EPS = 1e-6
NEG_INF = -1e30

MLA_HEADS = 8
MLA_NOPE = 128
MLA_ROPE = 64
MLA_V = 128
MLA_QK = MLA_NOPE + MLA_ROPE
MLA_Q_RANK = 512
MLA_KV_RANK = 256
ROPE_THETA = 10000.0
MLA_HEAD_PAD = 256

S5_GROUP = 16
S5_STATE = 64
S5_CHUNK = 16

DIL_PATTERNS = ((128, 1), (512, 4), (2048, 16))
DIL_SPAN = 128
DIL_HEADS = 8
DIL_HEAD_DIM = 64
DIL_WIDTH = DIL_HEADS * DIL_HEAD_DIM
DIL_IN = len(DIL_PATTERNS) * 3 * DIL_WIDTH
T5_BUCKETS = 32
T5_MAX_DIST = 2048

RW_HEAD = 64
RW_LORA_W = 64
RW_LORA_A = 64
RW_LORA_G = 224
RW_GN_EPS = 64e-5
RW_CHUNK = 64
RW_MISC = 512

N_EXPERTS = 16
N_EXPERT_GROUPS = 4
EXPERTS_PER_GROUP = 4
D_EXPERT = 512
N_PAIRS = 6
N_CLASSES = N_EXPERT_GROUPS * N_PAIRS
MOE_TILE = 256

LANES = 128
VMEM_LIMIT = 56 * 1024 * 1024


def _cp(*sem):
    return pltpu.CompilerParams(dimension_semantics=sem, vmem_limit_bytes=VMEM_LIMIT)


def _dot(a, b):
    return jnp.dot(a, b, preferred_element_type=F32)


def _dot_nt(a, b):
    return lax.dot_general(a, b, (((1,), (1,)), ((), ())), preferred_element_type=F32)


def _split_bf16(x, parts):
    out = []
    for _ in range(parts):
        hi = x.astype(BF16)
        out.append(hi)
        x = x - hi.astype(F32)
    return out


def _dot_exact_rhs(x, w_bf16, parts=3):
    acc = None
    for p in _split_bf16(x, parts):
        t = _dot(p, w_bf16)
        acc = t if acc is None else acc + t
    return acc


def _dot_exact_lhs(w_bf16, x, parts=3):
    acc = None
    for p in _split_bf16(x, parts):
        t = _dot(w_bf16, p)
        acc = t if acc is None else acc + t
    return acc


def _layernorm(x, g, b):
    mu = jnp.mean(x, -1, keepdims=True)
    d = x - mu
    var = jnp.mean(d * d, -1, keepdims=True)
    return d * lax.rsqrt(var + LN_EPS) * g + b


def _cast_body(x_ref, o_ref):
    o_ref[...] = x_ref[...].astype(BF16)


def _cast_bf16(w, block_rows, col_blocks=2):
    w2 = w.reshape(-1, w.shape[-1])
    rows, cols = w2.shape
    spec = pl.BlockSpec((block_rows, cols // col_blocks), lambda i, j: (i, j))
    out = pl.pallas_call(
        _cast_body, grid=(rows // block_rows, col_blocks), in_specs=[spec], out_specs=spec,
        out_shape=jax.ShapeDtypeStruct((rows, cols), BF16), compiler_params=_cp("parallel", "parallel"),
        name="cast_bf16",
    )(w2)
    return out.reshape(w.shape)


def _pack_halves(x):
    n = x.shape[1] // 2
    xb = x.astype(BF16).astype(F32)
    lo = lax.shift_right_logical(pltpu.bitcast(xb[:, :n], jnp.uint32), jnp.uint32(16))
    hi = pltpu.bitcast(xb[:, n:], jnp.uint32) & jnp.uint32(0xFFFF0000)
    return lo | hi


def _unpack_halves(p):
    lo = pltpu.bitcast(lax.shift_left(p, jnp.uint32(16)), F32)
    hi = pltpu.bitcast(p & jnp.uint32(0xFFFF0000), F32)
    return lo, hi


def _gather_body(idx_ref, x_hbm, o_ref, sem, *, step):
    base = pl.program_id(0) * step

    def issue(r, carry):
        pltpu.make_async_copy(x_hbm.at[pl.ds(idx_ref[base + r], 1)], o_ref.at[pl.ds(r, 1)], sem.at[0]).start()
        return carry

    lax.fori_loop(0, step, issue, 0, unroll=8)
    pltpu.make_async_copy(x_hbm.at[pl.ds(0, step)], o_ref, sem.at[0]).wait()


def _gather_rows(x, idx):
    rows = idx.shape[0]
    d = x.shape[1]
    step = 2048 if rows % 2048 == 0 else MOE_TILE
    return pl.pallas_call(
        functools.partial(_gather_body, step=step),
        grid_spec=pltpu.PrefetchScalarGridSpec(
            num_scalar_prefetch=1,
            grid=(rows // step,),
            in_specs=[pl.BlockSpec(memory_space=pl.ANY)],
            out_specs=pl.BlockSpec((step, d), lambda i, idx: (i, 0)),
            scratch_shapes=[pltpu.SemaphoreType.DMA((1,))],
        ),
        out_shape=jax.ShapeDtypeStruct((rows, d), x.dtype),
        compiler_params=_cp("arbitrary"),
        name="gather_rows",
    )(idx, x)


def _ada_body(c_ref, w_ref, b_ref, o_ref):
    c = c_ref[...]
    cond = c * jax.nn.sigmoid(c)
    o_ref[0] = _dot_exact_rhs(cond, w_ref[0].astype(BF16), 2) + b_ref[0]


def _ada(c, ada_w, ada_b):
    depth, d, n = ada_w.shape
    b = c.shape[0]
    rows = -(-b // 8) * 8
    cp = jnp.pad(c, ((0, rows - b), (0, 0)))
    tn = 1024
    out = pl.pallas_call(
        _ada_body,
        grid=(depth, n // tn),
        in_specs=[
            pl.BlockSpec((rows, d), lambda l, j: (0, 0)),
            pl.BlockSpec((1, d, tn), lambda l, j: (l, 0, j)),
            pl.BlockSpec((1, 1, tn), lambda l, j: (l, 0, j)),
        ],
        out_specs=pl.BlockSpec((1, rows, tn), lambda l, j: (l, 0, j)),
        out_shape=jax.ShapeDtypeStruct((depth, rows, n), F32),
        compiler_params=_cp("parallel", "parallel"),
        name="ada",
    )(cp, ada_w, ada_b.reshape(depth, 1, n))
    return out[:, :b].reshape(depth, b, 6, d)


def _front0_body(x_ref, mod_ref, cs_ref, win_ref, qn_ref, kvn_ref, wq1_ref, wq2_ref, wkn_ref,
                 wv_ref, q_ref, k_ref, v_ref, u_ref):
    m = mod_ref[0]
    h = (x_ref[...] * (1.0 + m[1:2]) + m[0:1]).astype(BF16)
    z = _dot(h, win_ref[...])
    cs = cs_ref[...]
    c2 = cs[:, :LANES]
    s2 = cs[:, LANES:]
    q_c = z[:, :MLA_Q_RANK]
    kv_c = z[:, MLA_Q_RANK:MLA_Q_RANK + MLA_KV_RANK]
    o = MLA_Q_RANK + MLA_KV_RANK
    krope = (z[:, o:o + LANES] * c2 + z[:, o + LANES:o + 2 * LANES] * s2).astype(BF16)
    u_ref[...] = z[:, o + 2 * LANES:]
    hq = (q_c * lax.rsqrt(jnp.mean(q_c * q_c, -1, keepdims=True) + RMS_EPS) * qn_ref[...]).astype(BF16)
    hkv = (kv_c * lax.rsqrt(jnp.mean(kv_c * kv_c, -1, keepdims=True) + RMS_EPS) * kvn_ref[...]).astype(BF16)
    full = _dot(hq, wq1_ref[...])
    sw = _dot(hq, wq2_ref[...])
    kn = _dot(hkv, wkn_ref[...])
    v_ref[...] = _dot(hkv, wv_ref[...]).astype(BF16)
    for j in range(MLA_HEADS):
        a = j * MLA_HEAD_PAD
        q_ref[:, a:a + LANES] = full[:, a:a + LANES].astype(BF16)
        q_ref[:, a + LANES:a + 2 * LANES] = (
            full[:, a + LANES:a + 2 * LANES] * c2 + sw[:, j * LANES:(j + 1) * LANES] * s2).astype(BF16)
        k_ref[:, a:a + LANES] = kn[:, j * LANES:(j + 1) * LANES].astype(BF16)
        k_ref[:, a + LANES:a + 2 * LANES] = krope


def _rope_swapped(w):
    half = w.shape[-1] // 2
    return jnp.concatenate([-w[..., half:], w[..., :half]], -1)


def _front0(x2, mod, seq, ev_w_in, q_norm, w_uq, kv_norm, w_ukv):
    t, d = x2.shape
    tm = 256
    per_b = seq // tm
    zpad = jnp.zeros((d, LANES - MLA_ROPE), F32)
    o = MLA_Q_RANK + MLA_KV_RANK
    w_kr = ev_w_in[:, o:o + MLA_ROPE]
    win = jnp.concatenate([ev_w_in[:, :o], w_kr, zpad, _rope_swapped(w_kr), zpad,
                           ev_w_in[:, o + MLA_ROPE:]], 1).astype(BF16)
    scale = MLA_QK ** -0.5
    wq = w_uq.reshape(MLA_Q_RANK, MLA_HEADS, MLA_QK) * scale
    zq = jnp.zeros((MLA_Q_RANK, MLA_HEADS, MLA_HEAD_PAD - MLA_QK), F32)
    wq1 = jnp.concatenate([wq, zq], -1).reshape(MLA_Q_RANK, -1).astype(BF16)
    wq2 = jnp.concatenate([_rope_swapped(wq[..., MLA_NOPE:]), zq], -1).reshape(MLA_Q_RANK, -1).astype(BF16)
    wkv = w_ukv.reshape(MLA_KV_RANK, MLA_HEADS, MLA_NOPE + MLA_V)
    wkn = wkv[..., :MLA_NOPE].reshape(MLA_KV_RANK, -1).astype(BF16)
    wv = wkv[..., MLA_NOPE:].reshape(MLA_KV_RANK, -1).astype(BF16)
    inv = ROPE_THETA ** (-jnp.arange(0, MLA_ROPE, 2, dtype=F32) / MLA_ROPE)
    ang = jnp.arange(seq, dtype=F32)[:, None] * inv[None]
    zt = jnp.zeros((seq, LANES - MLA_ROPE), F32)
    cs = jnp.concatenate([jnp.cos(ang), jnp.cos(ang), zt, jnp.sin(ang), jnp.sin(ang), zt], 1)
    const = lambda i: (0, 0)
    hq_w = MLA_HEADS * MLA_HEAD_PAD
    hv_w = MLA_HEADS * MLA_V
    s5w = ev_w_in.shape[1] - o - MLA_ROPE
    return pl.pallas_call(
        _front0_body,
        grid=(t // tm,),
        in_specs=[
            pl.BlockSpec((tm, d), lambda i: (i, 0)),
            pl.BlockSpec((1, 6, d), lambda i: (i // per_b, 0, 0)),
            pl.BlockSpec((tm, 2 * LANES), lambda i: (i % per_b, 0)),
            pl.BlockSpec(win.shape, const),
            pl.BlockSpec((1, MLA_Q_RANK), const),
            pl.BlockSpec((1, MLA_KV_RANK), const),
            pl.BlockSpec(wq1.shape, const),
            pl.BlockSpec(wq2.shape, const),
            pl.BlockSpec(wkn.shape, const),
            pl.BlockSpec(wv.shape, const),
        ],
        out_specs=[
            pl.BlockSpec((tm, hq_w), lambda i: (i, 0)),
            pl.BlockSpec((tm, hq_w), lambda i: (i, 0)),
            pl.BlockSpec((tm, hv_w), lambda i: (i, 0)),
            pl.BlockSpec((tm, s5w), lambda i: (i, 0)),
        ],
        out_shape=[
            jax.ShapeDtypeStruct((t, hq_w), BF16),
            jax.ShapeDtypeStruct((t, hq_w), BF16),
            jax.ShapeDtypeStruct((t, hv_w), BF16),
            jax.ShapeDtypeStruct((t, s5w), F32),
        ],
        compiler_params=_cp("parallel"),
        name="front0",
    )(x2, mod, cs, win, q_norm.reshape(1, -1), kv_norm.reshape(1, -1), wq1, wq2, wkn, wv)


def _flash_body(q_ref, k_ref, v_ref, o_ref, *, tile, heads):
    i = pl.program_id(2)
    qs = [q_ref[:, h * MLA_HEAD_PAD:(h + 1) * MLA_HEAD_PAD] for h in range(heads)]

    def one_head(h, start, carry, diagonal):
        m, l, acc = carry
        k = k_ref[pl.ds(start, tile), h * MLA_HEAD_PAD:(h + 1) * MLA_HEAD_PAD]
        v = v_ref[pl.ds(start, tile), h * MLA_V:(h + 1) * MLA_V]
        s = _dot_nt(qs[h], k)
        if diagonal:
            row = lax.broadcasted_iota(jnp.int32, s.shape, 0)
            col = lax.broadcasted_iota(jnp.int32, s.shape, 1)
            s = jnp.where(col <= row, s, NEG_INF)
        m_new = jnp.maximum(m, jnp.max(s, -1, keepdims=True))
        alpha = jnp.exp(m - m_new)
        p = jnp.exp(s - m_new)
        l = l * alpha + jnp.sum(p, -1, keepdims=True)
        acc = acc * alpha + _dot(p.astype(BF16), v)
        return m_new, l, acc

    def step(j, carry, diagonal):
        start = pl.multiple_of(j * tile, tile)
        return tuple(one_head(h, start, carry[h], diagonal) for h in range(heads))

    init = tuple((jnp.full((tile, 1), NEG_INF, F32), jnp.zeros((tile, 1), F32),
                  jnp.zeros((tile, MLA_V), F32)) for _ in range(heads))
    carry = lax.fori_loop(0, i, lambda j, c: step(j, c, False), init)
    carry = step(i, carry, True)
    for h, (_, l, acc) in enumerate(carry):
        o_ref[:, h * MLA_V:(h + 1) * MLA_V] = (acc / l).astype(BF16)


def _flash(q, k, v, batch, seq):
    t = q.shape[0]
    tile = 512
    heads = 4
    nq = seq // tile
    return pl.pallas_call(
        functools.partial(_flash_body, tile=tile, heads=heads),
        grid=(batch, MLA_HEADS // heads, nq),
        in_specs=[
            pl.BlockSpec((tile, heads * MLA_HEAD_PAD), lambda b, h, i: (b * nq + i, h)),
            pl.BlockSpec((seq, heads * MLA_HEAD_PAD), lambda b, h, i: (b, h)),
            pl.BlockSpec((seq, heads * MLA_V), lambda b, h, i: (b, h)),
        ],
        out_specs=pl.BlockSpec((tile, heads * MLA_V), lambda b, h, i: (b * nq + i, h)),
        out_shape=jax.ShapeDtypeStruct((t, MLA_HEADS * MLA_V), BF16),
        compiler_params=_cp("parallel", "parallel", "arbitrary"),
        name="mla_attention",
    )(q, k, v)


S5_GROUPS_PER_BLOCK = LANES // S5_GROUP


def _s5_body(u_ref, w_ref, v_ref, bd_ref, ac_ref, bc_ref, y_ref, ucat, yx, *, n_chunks, n_steps):
    lc = S5_CHUNK
    for t in range(lc):
        ucat[:, t * LANES:(t + 1) * LANES] = u_ref[pl.ds(t, n_chunks, stride=lc), :].astype(BF16)
    z = _dot(ucat[...], w_ref[0])
    half = z.shape[1] // 2
    cidx = lax.broadcasted_iota(jnp.int32, z.shape, 0)
    for step in range(n_steps):
        d = 1 << step
        zs = jnp.where(cidx >= d, pltpu.roll(z, d, 0), 0.0)
        z = z + zs * ac_ref[0, step:step + 1, :] + pltpu.roll(zs, half, 1) * bc_ref[0, step:step + 1, :]
    xprev = jnp.where(cidx >= 1, pltpu.roll(z, 1, 0), 0.0)
    yx[...] = _dot(xprev.astype(BF16), v_ref[0])
    for t in range(lc):
        k = (t + 1) * LANES
        y = _dot(ucat[:, :k], bd_ref[0, (lc - 1 - t) * LANES:, :]) + yx[:, t * LANES:(t + 1) * LANES]
        y_ref[pl.ds(t, n_chunks, stride=lc), :] = y


def _s5_matrices(lam_re, lam_im, b_re, b_im, c_re, c_im, log_dt, n_steps):
    lc = S5_CHUNK
    gb = S5_GROUPS_PER_BLOCK
    lam = lax.complex(lam_re.astype(F32), lam_im.astype(F32))
    n_blk = lam.shape[0] // gb
    dt = jnp.exp(log_dt.astype(F32))[:, None]
    lam_bar = jnp.exp(lam * dt)
    b_bar = ((lam_bar - 1.0) / lam)[..., None] * lax.complex(b_re.astype(F32), b_im.astype(F32))
    c = lax.complex(c_re.astype(F32), c_im.astype(F32))
    j = jnp.arange(lc + 1, dtype=F32)[:, None, None]
    pows = jnp.exp(j * (lam * dt)[None])
    eye = jnp.eye(gb, dtype=F32)
    kern = jnp.einsum("gop,jgp,gpi->jgio", c, pows[:lc], b_bar).real
    kern = kern.reshape(lc, n_blk, gb, S5_GROUP, S5_GROUP).transpose(1, 0, 2, 3, 4)
    mask = eye.astype(BF16)
    bd = (kern.astype(BF16)[:, :, :, :, None, :] * mask[None, None, :, None, :, None]).reshape(
        n_blk, lc, LANES, LANES)
    bdrev = bd[:, ::-1].reshape(n_blk, lc * LANES, LANES)
    tt = jnp.arange(lc)
    wm = pows[lc - 1 - tt][..., None] * b_bar[None]
    wm = jnp.stack([wm.real, wm.imag], 2)
    wm = wm.reshape(lc, n_blk, gb, 2, S5_STATE, S5_GROUP).transpose(1, 0, 2, 5, 3, 4)
    wm = wm.reshape(n_blk, lc * LANES, 2, S5_STATE).astype(BF16)
    row_g = (jnp.arange(lc * LANES) // S5_GROUP) % gb
    col_g = jnp.arange(gb * S5_STATE) // S5_STATE
    own = (row_g[:, None] == col_g[None, :]).astype(BF16)
    wcat = jnp.concatenate([jnp.tile(wm[:, :, r, :], (1, 1, gb)) * own[None] for r in range(2)], -1)
    cv = c[None] * pows[1:lc + 1][:, :, None, :]
    cv = jnp.stack([cv.real, -cv.imag], 3)
    cv = cv.reshape(lc, n_blk, gb, S5_GROUP, 2, S5_STATE).transpose(1, 4, 0, 3, 2, 5)
    cv = cv.reshape(n_blk, 2, lc, 1, S5_GROUP, gb * S5_STATE).astype(BF16)
    own_v = (jnp.arange(gb)[:, None] == col_g[None, :]).astype(BF16)
    vt = (cv * own_v[None, None, None, :, None, :]).reshape(n_blk, 2, lc * LANES, gb * S5_STATE)
    vcat = jnp.concatenate([vt[:, 0], vt[:, 1]], -1).transpose(0, 2, 1)
    stride = (lc * (2.0 ** jnp.arange(n_steps, dtype=F32)))[None, :, None]
    ap = jnp.exp(stride * (lam * dt)[:, None, :])

    def per_block(a):
        return a.reshape(n_blk, gb, n_steps, S5_STATE).transpose(0, 2, 1, 3).reshape(n_blk, n_steps, -1)

    ar, ai = per_block(ap.real), per_block(ap.imag)
    ac = jnp.concatenate([ar, ar], -1)
    bc = jnp.concatenate([-ai, ai], -1)
    return wcat, vcat, bdrev, ac, bc


def _s5_scan(u, batch, seq, mats):
    wcat, vcat, bdrev, ac, bc = mats
    n_blk = wcat.shape[0]
    lc = S5_CHUNK
    nc = seq // lc
    n_steps = ac.shape[1]
    wide = wcat.shape[2]
    per_blk = lambda i, b: (i, 0, 0)
    return pl.pallas_call(
        functools.partial(_s5_body, n_chunks=nc, n_steps=n_steps),
        grid=(n_blk, batch),
        in_specs=[
            pl.BlockSpec((seq, LANES), lambda i, b: (b, i)),
            pl.BlockSpec((1, lc * LANES, wide), per_blk),
            pl.BlockSpec((1, wide, lc * LANES), per_blk),
            pl.BlockSpec((1, lc * LANES, LANES), per_blk),
            pl.BlockSpec((1, n_steps, wide), per_blk),
            pl.BlockSpec((1, n_steps, wide), per_blk),
        ],
        out_specs=pl.BlockSpec((seq, LANES), lambda i, b: (b, i)),
        out_shape=jax.ShapeDtypeStruct(u.shape, F32),
        scratch_shapes=[pltpu.VMEM((nc, lc * LANES), BF16), pltpu.VMEM((nc, lc * LANES), F32)],
        compiler_params=_cp("parallel", "arbitrary"),
        name="s5_scan",
    )(u, wcat, vcat, bdrev, ac, bc)


def _s5_glu_body(y_ref, u_ref, d_ref, w_ref, o_ref):
    y = y_ref[...] + d_ref[...] * u_ref[...]
    y = 0.5 * y * (1.0 + jnp.tanh(math.sqrt(2.0 / math.pi) * (y + 0.044715 * (y * y * y))))
    gate = jax.nn.sigmoid(_dot(y.astype(BF16), w_ref[...]))
    o_ref[...] = (y * gate).astype(BF16)


def _s5_glu(y, u, d_skip, w_glu):
    t, n = y.shape
    tm = 512
    row = lambda i: (i, 0)
    const = lambda i: (0, 0)
    return pl.pallas_call(
        _s5_glu_body,
        grid=(t // tm,),
        in_specs=[pl.BlockSpec((tm, n), row), pl.BlockSpec((tm, n), row),
                  pl.BlockSpec((1, n), const), pl.BlockSpec((n, n), const)],
        out_specs=pl.BlockSpec((tm, n), row),
        out_shape=jax.ShapeDtypeStruct((t, n), BF16),
        compiler_params=_cp("parallel"),
        name="s5_glu",
    )(y, u, d_skip.reshape(1, n), w_glu.astype(BF16))


def _outproj_body(a1_ref, a2_ref, w1_ref, w2_ref, x_ref, mod_ref, g_ref, b_ref, rwh_ref, xo_ref, h_ref,
                  lg_ref):
    y = _dot(a1_ref[...], w1_ref[...]) + _dot(a2_ref[...], w2_ref[...])
    m = mod_ref[0]
    xn = _layernorm(DN_ALPHA * x_ref[...] + (1.0 + m[2:3]) * y, g_ref[...], b_ref[...])
    xo_ref[...] = xn
    h = xn * (1.0 + m[4:5]) + m[3:4]
    h_ref[...] = _pack_halves(h)
    h_hi, h_lo = _split_bf16(h, 2)
    ne = lg_ref.shape[1]
    both = _dot(h_hi, rwh_ref[...])
    lg_ref[...] = both[:, :ne] + (_dot(h_lo, rwh_ref[:, :ne]) + both[:, ne:])


def _outproj(a1, a2, w_out, x2, mod, ln_g, ln_b, router_w, seq):
    t, d = x2.shape
    k1 = a1.shape[1]
    k2 = a2.shape[1]
    tm = 256
    per_b = seq // tm
    w1 = w_out[:k1].astype(BF16)
    w2 = w_out[k1:].astype(BF16)
    rwh = jnp.concatenate(_split_bf16(router_w, 2), 1)
    ne = router_w.shape[1]
    row = lambda i: (i, 0)
    const = lambda i: (0, 0)
    return pl.pallas_call(
        _outproj_body,
        grid=(t // tm,),
        in_specs=[
            pl.BlockSpec((tm, k1), row), pl.BlockSpec((tm, k2), row),
            pl.BlockSpec((k1, d), const), pl.BlockSpec((k2, d), const),
            pl.BlockSpec((tm, d), row),
            pl.BlockSpec((1, 6, d), lambda i: (i // per_b, 0, 0)),
            pl.BlockSpec((1, d), const), pl.BlockSpec((1, d), const),
            pl.BlockSpec((d, 2 * ne), const),
        ],
        out_specs=[pl.BlockSpec((tm, d), row), pl.BlockSpec((tm, d // 2), row), pl.BlockSpec((tm, ne), row)],
        out_shape=[jax.ShapeDtypeStruct((t, d), F32), jax.ShapeDtypeStruct((t, d // 2), jnp.uint32),
                   jax.ShapeDtypeStruct((t, ne), F32)],
        compiler_params=_cp("parallel"),
        name="outproj_ln",
    )(a1, a2, w1, w2, x2, mod, ln_g.reshape(1, d), ln_b.reshape(1, d), rwh)


def _route_body(lg_ref, rb_ref, cls_ref, glo_ref, ghi_ref):
    s = jax.nn.sigmoid(lg_ref[...])
    sb = s + rb_ref[...]
    rows = [sb[e:e + 1, :] for e in range(N_EXPERTS)]
    raw = [s[e:e + 1, :] for e in range(N_EXPERTS)]
    gscore = []
    for g in range(N_EXPERT_GROUPS):
        v = rows[g * EXPERTS_PER_GROUP:(g + 1) * EXPERTS_PER_GROUP]
        best = None
        for a in range(EXPERTS_PER_GROUP):
            for b in range(a + 1, EXPERTS_PER_GROUP):
                pair = v[a] + v[b]
                best = pair if best is None else jnp.maximum(best, pair)
        gscore.append(best)
    bg = jnp.zeros_like(gscore[0], dtype=jnp.int32)
    bs = gscore[0]
    for g in range(1, N_EXPERT_GROUPS):
        upd = gscore[g] > bs
        bg = jnp.where(upd, g, bg)
        bs = jnp.where(upd, gscore[g], bs)

    def pick(vals, k):
        out = vals[k]
        for g in range(1, N_EXPERT_GROUPS):
            out = jnp.where(bg == g, vals[g * EXPERTS_PER_GROUP + k], out)
        return out

    v = [pick(rows, k) for k in range(EXPERTS_PER_GROUP)]
    r = [pick(raw, k) for k in range(EXPERTS_PER_GROUP)]
    i1 = jnp.zeros_like(bg)
    b1 = v[0]
    for k in range(1, EXPERTS_PER_GROUP):
        upd = v[k] > b1
        i1 = jnp.where(upd, k, i1)
        b1 = jnp.where(upd, v[k], b1)
    i2 = jnp.zeros_like(bg)
    b2 = jnp.full_like(b1, -jnp.inf)
    for k in range(EXPERTS_PER_GROUP):
        upd = (i1 != k) & (v[k] > b2)
        i2 = jnp.where(upd, k, i2)
        b2 = jnp.where(upd, v[k], b2)
    s1 = r[0]
    s2 = r[0]
    for k in range(1, EXPERTS_PER_GROUP):
        s1 = jnp.where(i1 == k, r[k], s1)
        s2 = jnp.where(i2 == k, r[k], s2)
    den = s1 + s2
    g1 = s1 / den
    g2 = s2 / den
    lo = jnp.minimum(i1, i2)
    hi = jnp.maximum(i1, i2)
    base = jnp.where(lo == 0, 0, jnp.where(lo == 1, 3, 5))
    cls_ref[...] = bg * N_PAIRS + base + hi - lo - 1
    first_lo = i1 < i2
    glo_ref[...] = jnp.where(first_lo, g1, g2)
    ghi_ref[...] = jnp.where(first_lo, g2, g1)


def _route(logits, router_b):
    t, ne = logits.shape
    tt = min(2048, t)
    col = lambda i: (0, i)
    cls, glo, ghi = pl.pallas_call(
        _route_body,
        grid=(t // tt,),
        in_specs=[pl.BlockSpec((ne, tt), col), pl.BlockSpec((ne, 1), lambda i: (0, 0))],
        out_specs=[pl.BlockSpec((1, tt), col)] * 3,
        out_shape=[jax.ShapeDtypeStruct((1, t), jnp.int32), jax.ShapeDtypeStruct((1, t), F32),
                   jax.ShapeDtypeStruct((1, t), F32)],
        compiler_params=_cp("parallel"),
        name="moe_route",
    )(logits.T, router_b.reshape(ne, 1).astype(F32))
    return cls[0], glo[0], ghi[0]


def _moe_body(e1_ref, e2_ref, valid_ref, h_ref, g_ref, wg1_ref, wg2_ref, wd1_ref, wd2_ref, o_ref):
    i = pl.program_id(0)

    @pl.when(valid_ref[i] == 1)
    def _():
        h_lo, h_hi = (v.astype(BF16) for v in _unpack_halves(h_ref[...]))
        half = h_lo.shape[1]
        g = g_ref[...]

        def expert(wg_ref, wd_ref):
            gu = _dot(h_lo, wg_ref[0, 0, :half, :]) + _dot(h_hi, wg_ref[0, 0, half:, :])
            gt = gu[:, :D_EXPERT]
            act = (gt * jax.nn.sigmoid(gt) * gu[:, D_EXPERT:]).astype(BF16)
            return _dot(act, wd_ref[0, 0])

        y = g[:, 0:1] * expert(wg1_ref, wd1_ref) + g[:, 1:2] * expert(wg2_ref, wd2_ref)
        o_ref[...] = _pack_halves(y)

    @pl.when(valid_ref[i] == 0)
    def _():
        o_ref[...] = jnp.zeros_like(o_ref)


_PAIR_LO = (0, 0, 0, 1, 1, 2)
_PAIR_HI = (1, 2, 3, 2, 3, 3)


def _moe(h, cls, glo, ghi, wgu, wdn, layer):
    t = h.shape[0]
    d = 2 * h.shape[1]
    tm = MOE_TILE
    n_tiles = t // tm + N_CLASSES
    rows = n_tiles * tm
    onehot = (cls[:, None] == jnp.arange(N_CLASSES)[None, :]).astype(jnp.int32)
    csum = jnp.cumsum(onehot, 0)
    rank = jnp.take_along_axis(csum, cls[:, None], 1)[:, 0] - 1
    counts = csum[-1]
    padded = -(-counts // tm) * tm
    ends = jnp.cumsum(padded)
    dest = (ends - padded)[cls] + rank
    src = (jnp.arange(rows, dtype=jnp.int32) % t).at[dest].set(jnp.arange(t, dtype=jnp.int32))
    tile_start = jnp.arange(n_tiles, dtype=jnp.int32) * tm
    valid = (tile_start < ends[-1]).astype(jnp.int32)
    last_cls = jnp.max(jnp.where(counts > 0, jnp.arange(N_CLASSES), 0))
    tile_cls = jnp.minimum(jnp.searchsorted(ends, tile_start, side="right"), last_cls).astype(jnp.int32)
    grp = tile_cls // N_PAIRS
    pair = tile_cls % N_PAIRS
    e1 = (grp * EXPERTS_PER_GROUP + jnp.asarray(_PAIR_LO, jnp.int32)[pair]).astype(jnp.int32)
    e2 = (grp * EXPERTS_PER_GROUP + jnp.asarray(_PAIR_HI, jnp.int32)[pair]).astype(jnp.int32)
    hs = _gather_rows(h, src)
    gs = jnp.stack([glo, ghi], 1).at[src].get(mode="promise_in_bounds")
    ys = pl.pallas_call(
        _moe_body,
        grid_spec=pltpu.PrefetchScalarGridSpec(
            num_scalar_prefetch=3,
            grid=(n_tiles,),
            in_specs=[
                pl.BlockSpec((tm, d // 2), lambda i, e1, e2, va: (i, 0)),
                pl.BlockSpec((tm, 2), lambda i, e1, e2, va: (i, 0)),
                pl.BlockSpec((1, 1, d, 2 * D_EXPERT), lambda i, e1, e2, va: (layer, e1[i], 0, 0)),
                pl.BlockSpec((1, 1, d, 2 * D_EXPERT), lambda i, e1, e2, va: (layer, e2[i], 0, 0)),
                pl.BlockSpec((1, 1, D_EXPERT, d), lambda i, e1, e2, va: (layer, e1[i], 0, 0)),
                pl.BlockSpec((1, 1, D_EXPERT, d), lambda i, e1, e2, va: (layer, e2[i], 0, 0)),
            ],
            out_specs=pl.BlockSpec((tm, d // 2), lambda i, e1, e2, va: (i, 0)),
        ),
        out_shape=jax.ShapeDtypeStruct((rows, d // 2), jnp.uint32),
        compiler_params=_cp("arbitrary"),
        name="moe_experts",
    )(e1, e2, valid, hs, gs, wgu, wgu, wdn, wdn)
    return _gather_rows(ys, dest.astype(jnp.int32))


def _resid_ln_body(x_ref, y_ref, mod_ref, g_ref, b_ref, o_ref):
    m = mod_ref[0]
    y = jnp.concatenate(_unpack_halves(y_ref[...]), 1)
    o_ref[...] = _layernorm(DN_ALPHA * x_ref[...] + (1.0 + m[5:6]) * y, g_ref[...], b_ref[...])


def _resid_ln(x2, y, mod, ln_g, ln_b, seq):
    t, d = x2.shape
    tm = 512
    per_b = seq // tm
    row = lambda i: (i, 0)
    const = lambda i: (0, 0)
    return pl.pallas_call(
        _resid_ln_body,
        grid=(t // tm,),
        in_specs=[pl.BlockSpec((tm, d), row), pl.BlockSpec((tm, d // 2), row),
                  pl.BlockSpec((1, 6, d), lambda i: (i // per_b, 0, 0)),
                  pl.BlockSpec((1, d), const), pl.BlockSpec((1, d), const)],
        out_specs=pl.BlockSpec((tm, d), row),
        out_shape=jax.ShapeDtypeStruct((t, d), F32),
        compiler_params=_cp("parallel"),
        name="resid_ln",
    )(x2, y, mod, ln_g.reshape(1, d), ln_b.reshape(1, d))


def _front1_body(x_ref, mod_ref, w_ref, zc0_ref, zc1_ref, zc2_ref, zr_ref, h_scr, res_scr, *, dils, tm):
    j = pl.program_id(1)
    n_att = 3 * len(dils)
    half = w_ref.shape[1] // 2

    @pl.when(j == 0)
    def _():
        m = mod_ref[0]
        h_scr[...] = (x_ref[...] * (1.0 + m[1:2]) + m[0:1]).astype(BF16)

    def column_half(c):
        return _dot(h_scr[...], w_ref[:, c * half:(c + 1) * half])

    @pl.when(j >= n_att)
    def _():
        for c in range(2):
            zr_ref[:, c * half:(c + 1) * half] = column_half(c).astype(BF16)

    for g, (o_ref, dil) in enumerate(zip((zc0_ref, zc1_ref, zc2_ref), dils)):
        @pl.when(j // 3 == g)
        def _(o_ref=o_ref, dil=dil):
            if dil == 1:
                for c in range(2):
                    o_ref[0, 0, :, c * half:(c + 1) * half] = column_half(c).astype(BF16)
                return
            per_half = half // LANES
            for c in range(2):
                res = column_half(c)
                for s in range(per_half):
                    res_scr[c * per_half + s] = res[:, s * LANES:(s + 1) * LANES]
            rows = tm // dil
            for r in range(dil):
                for s in range(2 * per_half):
                    o_ref[0, r, :, s * LANES:(s + 1) * LANES] = (
                        res_scr[s, pl.ds(r, rows, stride=dil), :].astype(BF16))


def _front1(x2, mod, od_w_in, batch, seq):
    t, d = x2.shape
    tm = min(1024, seq)
    tn = DIL_WIDTH
    per_b = seq // tm
    n_in = od_w_in.shape[1]
    n = -(-n_in // tn) * tn
    w = jnp.pad(od_w_in, ((0, 0), (0, n - n_in))).astype(BF16)
    dils = tuple(dil for _, dil in DIL_PATTERNS)
    n_att = 3 * len(dils)
    n_r = n // tn - n_att

    def att_spec(g, dil):
        return pl.BlockSpec((1, dil, tm // dil, tn),
                            lambda i, j: (i // per_b, 0, i % per_b, jnp.clip(j - 3 * g, 0, 2)))

    return pl.pallas_call(
        functools.partial(_front1_body, dils=dils, tm=tm),
        grid=(t // tm, n // tn),
        in_specs=[pl.BlockSpec((tm, d), lambda i, j: (i, 0)),
                  pl.BlockSpec((1, 6, d), lambda i, j: (i // per_b, 0, 0)),
                  pl.BlockSpec((d, tn), lambda i, j: (0, j))],
        out_specs=[att_spec(g, dil) for g, dil in enumerate(dils)]
        + [pl.BlockSpec((tm, tn), lambda i, j: (i, jnp.clip(j - n_att, 0, n_r - 1)))],
        out_shape=[jax.ShapeDtypeStruct((batch, dil, seq // dil, 3 * tn), BF16) for dil in dils]
        + [jax.ShapeDtypeStruct((t, n_r * tn), BF16)],
        scratch_shapes=[pltpu.VMEM((tm, d), BF16), pltpu.VMEM((tn // LANES, tm, LANES), F32)],
        compiler_params=_cp("parallel", "arbitrary"),
        name="front1",
    )(x2, mod, w)


def _dil_body(q_ref, kp_ref, kc_ref, vp_ref, vc_ref, bias_ref, o_ref, lse_ref):
    n = pl.program_id(2)
    span = DIL_SPAN
    q = q_ref[0, 0]
    k = jnp.concatenate([kp_ref[0, 0], kc_ref[0, 0]], 0)
    v = jnp.concatenate([vp_ref[0, 0], vc_ref[0, 0]], 0)
    qi = lax.broadcasted_iota(jnp.int32, (span, 2 * span), 0)
    ku = lax.broadcasted_iota(jnp.int32, (span, 2 * span), 1)
    valid = (ku >= qi) & (ku <= qi + span) & ((n >= 1) | (ku >= span))
    lane = lax.broadcasted_iota(jnp.int32, (span, LANES), 1)
    first = lane < DIL_HEAD_DIM
    lse_all = jnp.zeros((span, LANES), F32)
    scale = DIL_HEAD_DIM ** -0.5
    for hp in range(DIL_HEADS // 2):
        sl = slice(hp * LANES, (hp + 1) * LANES)
        q2, k2, v2 = q[:, sl], k[:, sl], v[:, sl]
        outs = []
        for hh in range(2):
            head = 2 * hp + hh
            qm = jnp.where(first if hh == 0 else jnp.logical_not(first), q2, jnp.zeros_like(q2))
            s = _dot_nt(qm, k2) * scale + bias_ref[head]
            s = jnp.where(valid, s, NEG_INF)
            m = jnp.max(s, -1, keepdims=True)
            p = jnp.exp(s - m)
            l = jnp.sum(p, -1, keepdims=True)
            outs.append(_dot(p.astype(BF16), v2) / l)
            lse_all = jnp.where(lane == head, m + jnp.log(l), lse_all)
        o_ref[0, 0, :, sl] = jnp.where(first, outs[0], outs[1]).astype(BF16)
    lse_ref[0, 0] = lse_all


def _t5_bucket(dist):
    exact = T5_BUCKETS // 2
    logd = jnp.log(jnp.maximum(dist, 1).astype(F32) / exact) / math.log(T5_MAX_DIST / exact)
    large = jnp.minimum(exact + (logd * (T5_BUCKETS - exact)).astype(jnp.int32), T5_BUCKETS - 1)
    return jnp.where(dist < exact, dist, large)


def _dilated_group(zc, rel_bias, gi, dil, batch, seq):
    span = DIL_SPAN
    l_stream = seq // dil
    nb = l_stream // span
    qi = jnp.arange(span)[:, None]
    ku = jnp.arange(2 * span)[None, :]
    delta = jnp.clip(span + qi - ku, 0, span) * dil
    onehot = (_t5_bucket(delta)[None] == jnp.arange(T5_BUCKETS)[:, None, None]).astype(F32)
    table = rel_bias.astype(F32)[:, gi * DIL_HEADS:(gi + 1) * DIL_HEADS]
    bias = jnp.einsum("kh,kqu->hqu", table, onehot, precision=lax.Precision.HIGHEST)

    def cur(j):
        return lambda b, r, n: (b, r, n, j)

    def prev(j):
        return lambda b, r, n: (b, r, jnp.maximum(n - 1, 0), j)

    blk = (1, 1, span, DIL_WIDTH)
    return pl.pallas_call(
        _dil_body,
        grid=(batch, dil, nb),
        in_specs=[pl.BlockSpec(blk, cur(0)), pl.BlockSpec(blk, prev(1)), pl.BlockSpec(blk, cur(1)),
                  pl.BlockSpec(blk, prev(2)), pl.BlockSpec(blk, cur(2)),
                  pl.BlockSpec((DIL_HEADS, span, 2 * span), lambda b, r, n: (0, 0, 0))],
        out_specs=[pl.BlockSpec(blk, cur(0)), pl.BlockSpec((1, 1, span, LANES), cur(0))],
        out_shape=[jax.ShapeDtypeStruct((batch, dil, l_stream, DIL_WIDTH), BF16),
                   jax.ShapeDtypeStruct((batch, dil, l_stream, LANES), F32)],
        compiler_params=_cp("parallel", "parallel", "arbitrary"),
        name=f"dilated_attention_{dil}",
    )(zc, zc, zc, zc, zc, bias)


def _dil_merge_body(o1_ref, o2_ref, o3_ref, l1_ref, l2_ref, l3_ref, e_ref, out_ref, o_scr, l1_scr, l2_scr,
                    l3_scr, *, dils, tm):
    def unstream(ref, scr, dil):
        rows = tm // dil
        n_slab = scr.shape[0]
        for r in range(dil):
            x = ref[0, r].astype(F32)
            for c in range(n_slab):
                scr[c, pl.ds(r, rows, stride=dil), :] = x[:, c * LANES:(c + 1) * LANES]
        return jnp.concatenate([scr[c] for c in range(n_slab)], 1)

    lses = [unstream(ref, scr, dil) for ref, scr, dil in
            zip((l1_ref, l2_ref, l3_ref), (l1_scr, l2_scr, l3_scr), dils)]
    top = jnp.maximum(jnp.maximum(lses[0], lses[1]), lses[2])
    num = None
    den = None
    for o_ref, lse, dil in zip((o1_ref, o2_ref, o3_ref), lses, dils):
        w = _dot_exact_rhs(jnp.exp(lse - top), e_ref[...], 2)
        o = unstream(o_ref, o_scr, dil)
        num = w * o if num is None else num + w * o
        den = w if den is None else den + w
    out_ref[...] = (num / den).astype(BF16)


def _dil_merge(outs, lses, batch, seq):
    tm = 512
    per_b = seq // tm
    dils = tuple(dil for _, dil in DIL_PATTERNS)
    head_of_lane = jnp.arange(DIL_WIDTH) // DIL_HEAD_DIM
    expand = (jnp.arange(LANES)[:, None] == head_of_lane[None, :]).astype(BF16)

    def spec(dil, width):
        return pl.BlockSpec((1, dil, tm // dil, width), lambda i: (i // per_b, 0, i % per_b, 0))

    return pl.pallas_call(
        functools.partial(_dil_merge_body, dils=dils, tm=tm),
        grid=(batch * per_b,),
        in_specs=[spec(dil, DIL_WIDTH) for dil in dils] + [spec(dil, LANES) for dil in dils]
        + [pl.BlockSpec((LANES, DIL_WIDTH), lambda i: (0, 0))],
        out_specs=pl.BlockSpec((tm, DIL_WIDTH), lambda i: (i, 0)),
        out_shape=jax.ShapeDtypeStruct((batch * seq, DIL_WIDTH), BF16),
        scratch_shapes=[pltpu.VMEM((DIL_WIDTH // LANES, tm, LANES), F32)] + [pltpu.VMEM((1, tm, LANES), F32)] * 3,
        compiler_params=_cp("parallel"),
        name="dilated_merge",
    )(*outs, *lses, expand)


def _rwkv_body(zr_ref, zk_ref, zv_ref, zm_ref, mur_ref, muk_ref, muv_ref, mum_ref, w0_ref, w2_ref,
               a0_ref, a2_ref, g2_ref, kk_ref, ka_ref, rk_ref, lng_ref, lnb_ref, o_ref,
               st_ref, pr_ref, pk_ref, pv_ref, pm_ref, xr_s, xk_s, xv_s, lw_s, as_s, g_s, cum_s):
    lc = RW_CHUNK
    n_slabs = st_ref.shape[0]

    @pl.when(pl.program_id(1) == 0)
    def _():
        st_ref[...] = jnp.zeros_like(st_ref)
        pr_ref[...] = jnp.zeros_like(pr_ref)
        pk_ref[...] = jnp.zeros_like(pk_ref)
        pv_ref[...] = jnp.zeros_like(pv_ref)
        pm_ref[...] = jnp.zeros_like(pm_ref)

    def shifted(z_ref, prev_ref, mu_ref):
        z = z_ref[...].astype(F32)
        first_row = lax.broadcasted_iota(jnp.int32, z.shape, 0) == 0
        prev = jnp.where(first_row, prev_ref[...], pltpu.roll(z, 1, 0))
        prev_ref[...] = z[lc - 1:lc, :]
        return z + (prev - z) * mu_ref[...]

    xr_s[...] = shifted(zr_ref, pr_ref, mur_ref)
    xk_s[...] = shifted(zk_ref, pk_ref, muk_ref)
    xv_s[...] = shifted(zv_ref, pv_ref, muv_ref)
    xm = shifted(zm_ref, pm_ref, mum_ref)
    lora_in = xm[:, :LANES]
    wl = w0_ref[...] + _dot(jnp.tanh(lora_in).astype(BF16), w2_ref[...])
    w_log = -(jnp.maximum(-wl, 0.0) + jnp.log(1.0 + jnp.exp(-jnp.abs(wl)))) - 0.5
    lw = -jnp.exp(w_log)
    lw_s[...] = lw
    as_s[...] = jax.nn.sigmoid(a0_ref[...] + _dot(lora_in.astype(BF16), a2_ref[...]))
    g_s[...] = _dot(jax.nn.sigmoid(xm[:, LANES:]).astype(BF16), g2_ref[...])
    ti = lax.broadcasted_iota(jnp.int32, (lc, lc), 0)
    si = lax.broadcasted_iota(jnp.int32, (lc, lc), 1)
    cum_s[...] = _dot_exact_lhs((si <= ti).astype(BF16), lw, 3)

    row = lax.broadcasted_iota(jnp.int32, (2 * lc, 2 * lc), 0)
    colm = lax.broadcasted_iota(jnp.int32, (2 * lc, 2 * lc), 1)
    same_head = (row >= lc) == (colm >= lc)
    t_in = row & (lc - 1)
    s_in = colm & (lc - 1)
    strict = same_head & (s_in < t_in)
    incl = same_head & (s_in <= t_in)
    eye = row == colm
    ones_bd = same_head.astype(BF16)
    lane = lax.broadcasted_iota(jnp.int32, (lc, LANES), 1)
    first = lane < RW_HEAD

    def seg(x, parts=2):
        return _dot_exact_rhs(x, ones_bd, parts)

    def stack_heads(x):
        return jnp.concatenate([jnp.where(first, x, 0.0), jnp.where(first, 0.0, x)], 0)

    slabs = range(n_slabs)
    sls = [slice(hp * LANES, (hp + 1) * LANES) for hp in slabs]
    kk_raw = [xk_s[:, sl] * kk_ref[:, sl] for sl in sls]
    kk_ss = [seg(kk * kk) for kk in kk_raw]
    lhs, rhs, bk_t, v_bs, k2s, pl_cols = [], [], [], [], [], []
    for hp, sl in enumerate(sls):
        a_sig = as_s[:, sl]
        cum = cum_s[:, sl]
        kk = kk_raw[hp] / jnp.maximum(jnp.sqrt(kk_ss[hp]), 1e-12)
        k2 = xk_s[:, sl] * (1.0 + (a_sig - 1.0) * ka_ref[:, sl])
        b_p = kk * a_sig
        cl = cum[lc - 1:lc, :]
        e_neg = jnp.exp(-cum)
        e_end = jnp.exp(cl - cum)
        at = -kk * jnp.exp(cum - lw_s[:, sl])
        rt = xr_s[:, sl] * jnp.exp(cum)
        lhs.append(jnp.concatenate([stack_heads(at), stack_heads(rt)], 0).astype(BF16))
        bt = (b_p * e_neg).astype(BF16)
        kt = (k2 * e_neg).astype(BF16)
        rhs.append(jnp.concatenate([bt, bt, kt, kt], 0))
        bk_t.append(jnp.concatenate([b_p * e_end, k2 * e_end], 0).T.astype(BF16))
        v_bs.append(xv_s[:, sl].astype(BF16))
        k2s.append(k2)
        pl_cols.append(jnp.sum(jnp.where(eye, jnp.broadcast_to(jnp.exp(cl), (2 * lc, LANES)), 0.0), 1,
                               keepdims=True))
    qms = [_dot_nt(lhs[hp], rhs[hp]) for hp in slabs]
    states = [st_ref[hp] for hp in slabs]
    arss = [_dot(lhs[hp], states[hp].astype(BF16)) for hp in slabs]
    a_abs = [jnp.where(strict, qm[:2 * lc, :2 * lc], 0.0).astype(BF16) for qm in qms]
    xs = [jnp.where(eye, 1.0, 0.0) + a.astype(F32) for a in a_abs]
    ps = [_dot(a, a).astype(BF16) for a in a_abs]
    n_double = int(math.log2(lc)) - 1
    for it in range(n_double - 1):
        res = [_dot(ps[hp], jnp.concatenate([xs[hp].astype(BF16), ps[hp]], 1)) for hp in slabs]
        xs = [xs[hp] + res[hp][:, :2 * lc] for hp in slabs]
        ps = [res[hp][:, 2 * lc:].astype(BF16) for hp in slabs]
    xs = [xs[hp] + _dot(ps[hp], xs[hp].astype(BF16)) for hp in slabs]
    vss = [jnp.concatenate([v_b, v_b], 0) for v_b in v_bs]
    gmats = [_dot(jnp.where(strict, qms[hp][:2 * lc, 2 * lc:], 0.0).astype(BF16), vss[hp]) + arss[hp][:2 * lc]
             for hp in slabs]
    uss = [_dot(xs[hp].astype(BF16), gmats[hp].astype(BF16)) for hp in slabs]
    incl2 = jnp.concatenate([incl, incl], 1)
    yss = [_dot(jnp.where(incl2, qms[hp][2 * lc:, :], 0.0).astype(BF16),
                jnp.concatenate([uss[hp].astype(BF16), vss[hp]], 0)) + arss[hp][2 * lc:] for hp in slabs]
    us_p = [jnp.where(first, us[:lc], us[lc:]) for us in uss]
    ys_p = [jnp.where(first, ys[:lc], ys[lc:]) for ys in yss]
    upds = [_dot(bk_t[hp], jnp.concatenate([us_p[hp].astype(BF16), v_bs[hp]], 0)) for hp in slabs]
    for hp in slabs:
        st_ref[hp] = pl_cols[hp] * states[hp] + jnp.where(same_head, upds[hp], 0.0)
    means = [seg(y) * (1.0 / RW_HEAD) for y in ys_p]
    dys = [ys_p[hp] - means[hp] for hp in slabs]
    variances = [seg(dy * dy, 1) * (1.0 / RW_HEAD) for dy in dys]
    bonus = [seg(xr_s[:, sl] * k2s[hp] * rk_ref[:, sl], 1) for hp, sl in enumerate(sls)]
    for hp, sl in enumerate(sls):
        yn = dys[hp] * lax.rsqrt(variances[hp] + RW_GN_EPS) * lng_ref[:, sl] + lnb_ref[:, sl]
        o_ref[:, sl] = ((yn + bonus[hp] * xv_s[:, sl]) * g_s[:, sl]).astype(BF16)


def _rwkv(z, batch, seq, mu, w0, w2, a0, a2, g2, k_k, k_a, r_k, lnx_g, lnx_b):
    t = z.shape[0]
    lc = RW_CHUNK
    nc = seq // lc
    width = w0.shape[0]
    n_slabs = width // LANES
    col0 = 0
    misc_blk = 3 * width // RW_MISC
    n_lora = RW_LORA_W + RW_LORA_A + RW_LORA_G
    mu_m = jnp.pad(mu[3 * width:], (0, RW_MISC - n_lora)).reshape(1, RW_MISC)
    w2p = jnp.pad(w2, ((0, LANES - RW_LORA_W), (0, 0))).astype(BF16)
    a2p = jnp.pad(a2, ((RW_LORA_W, LANES - RW_LORA_W - RW_LORA_A), (0, 0))).astype(BF16)
    g2p = jnp.pad(g2, ((0, RW_MISC - LANES - RW_LORA_G), (0, 0))).astype(BF16)
    vec = lambda a: a.reshape(1, width).astype(F32)
    const = lambda b, c: (0, 0)
    wide = lambda: pl.BlockSpec((1, width), const)
    return pl.pallas_call(
        _rwkv_body,
        grid=(batch, nc),
        in_specs=[
            pl.BlockSpec((lc, width), lambda b, c: (b * nc + c, col0)),
            pl.BlockSpec((lc, width), lambda b, c: (b * nc + c, col0 + 1)),
            pl.BlockSpec((lc, width), lambda b, c: (b * nc + c, col0 + 2)),
            pl.BlockSpec((lc, RW_MISC), lambda b, c: (b * nc + c, misc_blk)),
            wide(), wide(), wide(), pl.BlockSpec((1, RW_MISC), const),
            wide(), pl.BlockSpec((LANES, width), const),
            wide(), pl.BlockSpec((LANES, width), const),
            pl.BlockSpec((RW_MISC - LANES, width), const),
            wide(), wide(), wide(), wide(), wide(),
        ],
        out_specs=pl.BlockSpec((lc, width), lambda b, c: (b * nc + c, 0)),
        out_shape=jax.ShapeDtypeStruct((t, width), BF16),
        scratch_shapes=[pltpu.VMEM((n_slabs, LANES, LANES), F32),
                        pltpu.VMEM((1, width), F32), pltpu.VMEM((1, width), F32),
                        pltpu.VMEM((1, width), F32), pltpu.VMEM((1, RW_MISC), F32)]
        + [pltpu.VMEM((lc, width), F32)] * 7,
        compiler_params=_cp("parallel", "arbitrary"),
        name="rwkv7",
    )(z, z, z, z, vec(mu[:width]), vec(mu[width:2 * width]), vec(mu[2 * width:3 * width]), mu_m,
      vec(w0), w2p, vec(a0), a2p, g2p, vec(k_k), vec(k_a), vec(r_k), vec(lnx_g), vec(lnx_b))


def kernel(x, c, ada_w, ada_b, ln_mix_g, ln_mix_b, ln_ffn_g, ln_ffn_b, router_w, router_b, moe_w_gate_up, moe_w_down, rel_bias, ev_w_in, mla_q_norm, mla_w_uq, mla_kv_norm, mla_w_ukv, s5_lambda_re, s5_lambda_im, s5_b_re, s5_b_im, s5_c_re, s5_c_im, s5_d, s5_log_dt, s5_w_glu, ev_w_out, od_w_in, rw_mu, rw_w0, rw_w2, rw_a0, rw_a2, rw_g2, rw_k_k, rw_k_a, rw_r_k, rw_lnx_g, rw_lnx_b, od_w_out):
    batch, seq, d = x.shape
    assert seq % 2048 == 0 and d % LANES == 0
    x2 = x.reshape(batch * seq, d)
    mods = _ada(c, ada_w, ada_b)
    wgu_all = _cast_bf16(moe_w_gate_up, 2 * moe_w_gate_up.shape[-2])
    wdn_all = _cast_bf16(moe_w_down, 4 * moe_w_down.shape[-2])
    for layer in range(DEPTH):
        mod = mods[layer]
        if layer % 2 == 0:
            e = layer // 2
            q, k, v, u = _front0(x2, mod, seq, ev_w_in[e], mla_q_norm[e], mla_w_uq[e], mla_kv_norm[e],
                                 mla_w_ukv[e])
            att = _flash(q, k, v, batch, seq)
            n_steps = max(1, math.ceil(math.log2(seq // S5_CHUNK)))
            mats = _s5_matrices(s5_lambda_re[e], s5_lambda_im[e], s5_b_re[e], s5_b_im[e], s5_c_re[e],
                                s5_c_im[e], s5_log_dt[e], n_steps)
            ssm = _s5_glu(_s5_scan(u, batch, seq, mats), u, s5_d[e], s5_w_glu[e])
            x2, h, logits = _outproj(att, ssm, ev_w_out[e], x2, mod, ln_mix_g[layer], ln_mix_b[layer],
                                     router_w, seq)
        else:
            o = layer // 2
            *zcs, z = _front1(x2, mod, od_w_in[o], batch, seq)
            outs, lses = [], []
            for gi, (win, dil) in enumerate(DIL_PATTERNS):
                assert win // dil == DIL_SPAN
                og, lg = _dilated_group(zcs[gi], rel_bias, gi, dil, batch, seq)
                outs.append(og)
                lses.append(lg)
            att = _dil_merge(outs, lses, batch, seq)
            tm_out = _rwkv(z, batch, seq, rw_mu[o], rw_w0[o], rw_w2[o], rw_a0[o], rw_a2[o], rw_g2[o],
                           rw_k_k[o], rw_k_a[o], rw_r_k[o], rw_lnx_g[o], rw_lnx_b[o])
            x2, h, logits = _outproj(att, tm_out, od_w_out[o], x2, mod, ln_mix_g[layer], ln_mix_b[layer],
                                     router_w, seq)
        cls, glo, ghi = _route(logits, router_b)
        y = _moe(h, cls, glo, ghi, wgu_all, wdn_all, layer)
        x2 = _resid_ln(x2, y, mod, ln_ffn_g[layer], ln_ffn_b[layer], seq)
    return x2.reshape(batch, seq, d)
```
